```python
import jax, jax.numpy as jnp
from jax import lax
import numpy as np

D_MODEL = 1024
BATCH = 8
SEQ = 2048
DEPTH = 4
DEC_BATCH = 32
DEC_SEQ = 1
PAST_LEN = 8192
PAGE_SIZE = 128

HEAD_DIM = 64
ROT_DIM = HEAD_DIM // 4
ROPE_THETA = 500000.0
N_BRANCH = 4
BRANCH_W = D_MODEL // N_BRANCH
NSA_HEADS = BRANCH_W // HEAD_DIM
CMP_BLOCK = 32
CMP_STRIDE = 16
SLC_BLOCK = 64
N_SELECT = 16
WINDOW = 512
PHI_HIDDEN = 256
RWKV_HEADS = BRANCH_W // HEAD_DIM
RWKV_DECAY_LORA = 64
RWKV_AAA_LORA = 64
RWKV_GATE_LORA = 160
RWKV_IN = 3 * BRANCH_W + RWKV_DECAY_LORA + RWKV_AAA_LORA + RWKV_GATE_LORA
RWKV_OFFSETS = (BRANCH_W, 2 * BRANCH_W, 3 * BRANCH_W, 3 * BRANCH_W + RWKV_DECAY_LORA,
                3 * BRANCH_W + RWKV_DECAY_LORA + RWKV_AAA_LORA)
RWKV_GN_EPS = 64e-5
MOBA_HEADS = BRANCH_W // HEAD_DIM
MOBA_BLOCK = 256
MOBA_TOPK = 3
CROSS_HEADS = BRANCH_W // HEAD_DIM
N_MEM = 256
D_FF = -(-(8 * D_MODEL) // (3 * 256)) * 256
IN_SPLITS = (BRANCH_W, 6 * HEAD_DIM, 3 * NSA_HEADS, RWKV_IN, 3 * BRANCH_W, BRANCH_W, N_BRANCH * D_MODEL)
IN_OFFSETS = tuple(int(v) for v in np.cumsum(IN_SPLITS)[:-1])
N_IN = int(sum(IN_SPLITS))
SLC_CHUNK = 128
MOBA_CHUNK = 32
WIN_CHUNK = 128
INVALID_POS = -(2 ** 30)
NEG = -1e30

kernel_name = 'hybrid_nsa_rwkv7_moba_decoder_step'


def rmsnorm(x, g, eps=1e-6):
    xf = x.astype(jnp.float32)
    y = xf * lax.rsqrt(jnp.mean(xf * xf, axis=-1, keepdims=True) + eps)
    return (y * g.astype(jnp.float32)).astype(x.dtype)


def masked_softmax(s, mask):
    s = jnp.where(mask, s.astype(jnp.float32), NEG)
    p = jnp.where(mask, jnp.exp(s - jnp.max(s, axis=-1, keepdims=True)), 0.0)
    return p / jnp.maximum(jnp.sum(p, axis=-1, keepdims=True), 1e-30)


def rope(x, pos):
    half = ROT_DIM // 2
    inv = ROPE_THETA ** (-jnp.arange(half, dtype=jnp.float32) / half)
    ang = pos.astype(jnp.float32)[:, None] * inv[None, :]
    shape = (pos.shape[0],) + (1,) * (x.ndim - 3) + (half,)
    cos, sin = jnp.cos(ang).reshape(shape), jnp.sin(ang).reshape(shape)
    x1 = x[..., :half].astype(jnp.float32)
    x2 = x[..., half:ROT_DIM].astype(jnp.float32)
    rot = jnp.concatenate([x1 * cos - x2 * sin, x2 * cos + x1 * sin], axis=-1)
    return jnp.concatenate([rot.astype(x.dtype), x[..., ROT_DIM:]], axis=-1)


def map_query_chunks(fn, chunk, *xs):
    t = xs[0].shape[1]
    c = chunk if t % chunk == 0 else t
    n = t // c
    if n == 1:
        return fn(*xs)
    split = tuple(jnp.moveaxis(a.reshape((a.shape[0], n, c) + a.shape[2:]), 1, 0) for a in xs)
    out = jnp.moveaxis(lax.map(lambda args: fn(*args), split), 0, 1)
    return out.reshape((out.shape[0], t) + out.shape[3:])


def gather_pages(pool, page_table):
    g = pool[page_table]
    return g.reshape((g.shape[0], -1) + g.shape[3:])


def window_attention(q, qpos, kv, kpos, pos0):
    B, T = q.shape[:2]
    off = kv.shape[1] - T
    kvp = jnp.pad(kv, ((0, 0), (WINDOW, 0), (0, 0), (0, 0)))
    kposp = jnp.concatenate([jnp.full((WINDOW,), INVALID_POS, jnp.int32), kpos.astype(jnp.int32)])
    scale = HEAD_DIM ** -0.5

    def chunk(qc, qpc):
        c = qc.shape[1]
        start = qpc[0, 0] - pos0 + off
        kvc = lax.dynamic_slice_in_dim(kvp, start, WINDOW + c, axis=1)
        kpc = lax.dynamic_slice_in_dim(kposp, start, WINDOW + c, axis=0)
        d = qpc[0][:, None] - kpc[None, :]
        mask = ((d >= 0) & (d <= WINDOW))[None, :, None, :]
        s = jnp.einsum('bchd,bkd->bchk', qc, kvc[:, :, 0]) * scale
        pr = masked_softmax(s, mask)
        return jnp.einsum('bchk,bkd->bchd', pr.astype(kvc.dtype), kvc[:, :, 1])

    return map_query_chunks(chunk, WIN_CHUNK, q, qpos[None])


def nsa_mixer(q, kv_new, gate_logits, qpos, pos0, past_cmp, past_slc, past_win, p):
    B, T, H, _ = q.shape
    scale = HEAD_DIM ** -0.5
    kv = kv_new.reshape(B, T, 3, 2, HEAD_DIM)
    new_cmp = kv[:, :, 0]
    new_slc = jnp.stack([rope(kv[:, :, 1, 0], qpos), kv[:, :, 1, 1]], axis=2)
    new_win = jnp.stack([rope(kv[:, :, 2, 0], qpos), kv[:, :, 2, 1]], axis=2)
    qr = rope(q, qpos)

    full_cmp = jnp.concatenate([past_cmp.astype(kv.dtype), new_cmp], axis=1)
    L = full_cmp.shape[1]
    n_cmp = (L - CMP_BLOCK) // CMP_STRIDE + 1
    starts = np.arange(n_cmp) * CMP_STRIDE
    blocks = full_cmp[:, starts[:, None] + np.arange(CMP_BLOCK)[None, :]] + p['nsa_pe']
    flat = jnp.moveaxis(blocks, 3, 1).reshape(B, 2, n_cmp, CMP_BLOCK * HEAD_DIM)
    hid = jax.nn.silu(jnp.einsum('bjnf,jfe->bjne', flat, p['nsa_phi_w1']))
    cmp_kv = jnp.einsum('bjne,jed->bjnd', hid, p['nsa_phi_w2'])
    s_cmp = jnp.einsum('bthd,bnd->bthn', q, cmp_kv[:, 0]) * scale
    cmp_mask = ((starts + CMP_BLOCK - 1)[None, :] <= qpos[:, None])[None, :, None, :]
    p_cmp = masked_softmax(s_cmp, cmp_mask)
    o_cmp = jnp.einsum('bthn,bnd->bthd', p_cmp.astype(cmp_kv.dtype), cmp_kv[:, 1])

    full_slc = jnp.concatenate([past_slc.astype(kv.dtype), new_slc], axis=1)
    n_slc = -(-L // SLC_BLOCK)
    full_slc = jnp.pad(full_slc, ((0, 0), (0, n_slc * SLC_BLOCK - L), (0, 0), (0, 0)))
    slc_blocks = full_slc.reshape(B, n_slc, SLC_BLOCK, 2, HEAD_DIM)
    blk_start = np.arange(n_slc) * SLC_BLOCK
    cover = ((starts[:, None] < blk_start[None, :] + SLC_BLOCK)
             & (starts[:, None] + CMP_BLOCK > blk_start[None, :])).astype(np.float32)
    importance = jnp.einsum('bthn,nj->btj', p_cmp, cover)
    blk = jnp.arange(n_slc)
    own = qpos // SLC_BLOCK
    causal = blk_start[None, :] <= qpos[:, None]
    forced = (blk[None, :] == 0) | (blk[None, :] == own[:, None]) | (blk[None, :] == own[:, None] - 1)
    score = jnp.where(forced, -NEG, jnp.where(causal, importance, NEG))
    _, sel = lax.top_k(score, min(N_SELECT, n_slc))
    sel_valid = jnp.take_along_axis(jnp.broadcast_to(causal[None], score.shape), sel, axis=-1)

    def slc_chunk(qc, qpc, selc, validc):
        c = qc.shape[1]
        g = slc_blocks[jnp.arange(B)[:, None, None], selc]
        kpos = selc[..., None] * SLC_BLOCK + jnp.arange(SLC_BLOCK)
        mask = (validc[..., None] & (kpos <= qpc[0][None, :, None, None])).reshape(B, c, 1, -1)
        g = g.reshape(B, c, -1, 2, HEAD_DIM)
        s = jnp.einsum('bchd,bckd->bchk', qc, g[:, :, :, 0]) * scale
        pr = masked_softmax(s, mask)
        return jnp.einsum('bchk,bckd->bchd', pr.astype(g.dtype), g[:, :, :, 1])

    o_slc = map_query_chunks(slc_chunk, SLC_CHUNK, qr, qpos[None], sel, sel_valid)

    full_win = jnp.concatenate([past_win.astype(kv.dtype), new_win], axis=1)
    pw = past_win.shape[1]
    kpos_win = pos0 - pw + jnp.arange(pw + T, dtype=jnp.int32)
    o_win = window_attention(qr, qpos, full_win, kpos_win, pos0)
    new_buf = full_win[:, full_win.shape[1] - min(WINDOW, full_win.shape[1]):]

    g = jax.nn.sigmoid(gate_logits.astype(jnp.float32)).reshape(B, T, H, 3, 1)
    o = g[:, :, :, 0] * o_cmp + g[:, :, :, 1] * o_slc + g[:, :, :, 2] * o_win
    return o.reshape(B, T, H * HEAD_DIM).astype(q.dtype), new_cmp, new_slc, new_buf


def rwkv_mixer(z, shift_prev, s0, p):
    B, T, _ = z.shape
    H, N = RWKV_HEADS, HEAD_DIM
    z_prev = jnp.concatenate([shift_prev[:, None, :].astype(z.dtype), z[:, :-1]], axis=1)
    zm = z + (z_prev - z) * p['rwkv_mu']
    r, k, v, zw, za, zg = jnp.split(zm, RWKV_OFFSETS, axis=-1)
    w = -jax.nn.softplus(-(p['rwkv_w0'] + jnp.tanh(zw) @ p['rwkv_w2'])) - 0.5
    decay = jnp.exp(-jnp.exp(w.astype(jnp.float32)))
    a = jax.nn.sigmoid(p['rwkv_a0'] + za @ p['rwkv_a2'])
    g = jax.nn.sigmoid(zg) @ p['rwkv_g2']
    kk = (k * p['rwkv_k_k']).reshape(B, T, H, N).astype(jnp.float32)
    kk = kk * lax.rsqrt(jnp.maximum(jnp.sum(kk * kk, axis=-1, keepdims=True), 1e-24))
    k = k * (1.0 + (a - 1.0) * p['rwkv_k_a'])

    def heads(t):
        return t.reshape(B, T, H, N).astype(jnp.float32)

    rh, kh, vh, ah, wh = heads(r), heads(k), heads(v), heads(a), heads(decay)

    def step(S, inp):
        r_t, w_t, k_t, v_t, kk_t, a_t = inp
        sk = jnp.einsum('bhvk,bhk->bhv', S, kk_t)
        S = S * w_t[:, :, None, :] - sk[..., None] * (kk_t * a_t)[:, :, None, :] + v_t[..., None] * k_t[:, :, None, :]
        return S, jnp.einsum('bhvk,bhk->bhv', S, r_t)

    xs = tuple(jnp.swapaxes(t, 0, 1) for t in (rh, wh, kh, vh, kk, ah))
    S, ys = lax.scan(step, s0.astype(jnp.float32), xs)
    y = jnp.swapaxes(ys, 0, 1)
    mu = jnp.mean(y, axis=-1, keepdims=True)
    var = jnp.mean(jnp.square(y - mu), axis=-1, keepdims=True)
    y = ((y - mu) * lax.rsqrt(var + RWKV_GN_EPS)).reshape(B, T, BRANCH_W) * p['rwkv_ln_w'] + p['rwkv_ln_b']
    bonus = jnp.sum(rh * kh * p['rwkv_r_k'].reshape(H, N), axis=-1, keepdims=True) * vh
    out = ((y + bonus.reshape(B, T, BRANCH_W)) * g).astype(z.dtype)
    return out, z[:, -1], S.astype(s0.dtype)


def moba_mixer(q, k, v, qpos, past):
    B, T, H, _ = q.shape
    scale = HEAD_DIM ** -0.5
    qr = rope(q, qpos)
    new_rows = jnp.stack([rope(k, qpos), v], axis=2)
    full = jnp.concatenate([past.astype(new_rows.dtype), new_rows], axis=1)
    L = full.shape[1]
    nb = -(-L // MOBA_BLOCK)
    full = jnp.pad(full, ((0, 0), (0, nb * MOBA_BLOCK - L), (0, 0), (0, 0), (0, 0)))
    kv_blocks = jnp.moveaxis(full.reshape(B, nb, MOBA_BLOCK, 2, H, HEAD_DIM), 4, 1)
    means = jnp.mean(kv_blocks[..., 0, :].astype(jnp.float32), axis=3)
    gate = jnp.einsum('bthd,bhjd->bthj', qr.astype(jnp.float32), means)
    own = qpos // MOBA_BLOCK
    past_blk = (jnp.arange(nb)[None, :] < own[:, None])[None, :, None, :]
    _, top = lax.top_k(jnp.where(past_blk, gate, NEG), min(MOBA_TOPK, nb))
    top_valid = jnp.take_along_axis(jnp.broadcast_to(past_blk, gate.shape), top, axis=-1)
    idx = jnp.concatenate([top, jnp.broadcast_to(own[None, :, None, None], (B, T, H, 1))], axis=-1)
    valid = jnp.concatenate([top_valid, jnp.ones((B, T, H, 1), bool)], axis=-1)

    def chunk(qc, qpc, idxc, validc):
        c = qc.shape[1]
        g = kv_blocks[jnp.arange(B)[:, None, None, None], jnp.arange(H)[None, None, :, None], idxc]
        kpos = idxc[..., None] * MOBA_BLOCK + jnp.arange(MOBA_BLOCK)
        mask = (validc[..., None] & (kpos <= qpc[0][None, :, None, None, None])).reshape(B, c, H, -1)
        g = g.reshape(B, c, H, -1, 2, HEAD_DIM)
        s = jnp.einsum('bchd,bchkd->bchk', qc, g[..., 0, :]) * scale
        pr = masked_softmax(s, mask)
        return jnp.einsum('bchk,bchkd->bchd', pr.astype(g.dtype), g[..., 1, :])

    o = map_query_chunks(chunk, MOBA_CHUNK, qr, qpos[None], idx, valid)
    return o.reshape(B, T, H * HEAD_DIM), new_rows


def cross_attention(q, mem_kv):
    s = jnp.einsum('bthd,bmhd->bthm', q, mem_kv[:, :, 0]) * HEAD_DIM ** -0.5
    pr = jax.nn.softmax(s.astype(jnp.float32), axis=-1)
    o = jnp.einsum('bthm,bmhd->bthd', pr.astype(mem_kv.dtype), mem_kv[:, :, 1])
    return o.reshape(q.shape[0], q.shape[1], -1)


def decoder_layer(x, pos0, mem_kv, past_cmp, past_slc, past_win, past_moba, rwkv_state, shift_state, p):
    B, T, _ = x.shape
    qpos = pos0 + jnp.arange(T, dtype=jnp.int32)
    h = rmsnorm(x, p['g_pre_mix'])
    z = h @ p['w_in']
    nsa_q, nsa_kv, nsa_g, rwkv_z, moba_qkv, cross_q, merge_g = jnp.split(z, IN_OFFSETS, axis=-1)

    def heads(t):
        return t.reshape(B, T, -1, HEAD_DIM)

    o_nsa, new_cmp, new_slc, new_win = nsa_mixer(heads(nsa_q), nsa_kv, nsa_g, qpos, pos0,
                                                 past_cmp, past_slc, past_win, p)
    o_rwkv, new_shift, new_state = rwkv_mixer(rwkv_z, shift_state, rwkv_state, p)
    mq, mk, mv = jnp.split(moba_qkv, 3, axis=-1)
    o_moba, new_moba = moba_mixer(heads(mq), heads(mk), heads(mv), qpos, past_moba)
    o_cross = cross_attention(heads(cross_q), mem_kv)
    o = jnp.stack([o_nsa, o_rwkv, o_moba, o_cross], axis=2)
    gates = jax.nn.sigmoid(merge_g).reshape(B, T, N_BRANCH, D_MODEL)
    merged = jnp.sum(gates * jnp.einsum('btnc,ncd->btnd', o, p['w_branch']), axis=2)
    x = x + rmsnorm(merged @ p['w_out'], p['g_post_mix'])
    h2 = rmsnorm(x, p['g_pre_ffn'])
    f = (jax.nn.silu(h2 @ p['w_ffn_gate']) * (h2 @ p['w_ffn_up'])) @ p['w_ffn_down']
    x = x + rmsnorm(f, p['g_post_ffn'])
    return x, (new_cmp, new_slc, new_win, new_moba, new_state, new_shift)


def setup_inputs(seed: int = 0) -> dict:
    key = jax.random.key(seed)
    ks = iter(jax.random.split(key, 48))

    def nrm(shape, scale):
        return jax.random.normal(next(ks), shape, jnp.float32) * scale

    L = DEPTH
    n_pages = PAST_LEN // PAGE_SIZE
    n_used = DEC_BATCH * n_pages
    n_pool = n_used + max(1, n_used // 4)
    pw = min(WINDOW, PAST_LEN)
    return {
        'x_prompt': nrm((BATCH, SEQ, D_MODEL), 1.0),
        'x_sample': nrm((DEC_BATCH, DEC_SEQ, D_MODEL), 1.0),
        'cache_nsa_cmp': nrm((L, n_pool, PAGE_SIZE, 2, HEAD_DIM), 1.0),
        'cache_nsa_slc': nrm((L, n_pool, PAGE_SIZE, 2, HEAD_DIM), 1.0),
        'cache_nsa_win': nrm((L, DEC_BATCH, pw, 2, HEAD_DIM), 1.0),
        'cache_moba': nrm((L, n_pool, PAGE_SIZE, 2, MOBA_HEADS, HEAD_DIM), 1.0),
        'cache_mem': nrm((L, DEC_BATCH, N_MEM, 2, CROSS_HEADS, HEAD_DIM), 1.0),
        'state_rwkv': nrm((L, DEC_BATCH, RWKV_HEADS, HEAD_DIM, HEAD_DIM), 1.0),
        'state_shift': nrm((L, DEC_BATCH, RWKV_IN), 1.0),
        'page_table': jax.random.permutation(next(ks), n_pool)[:n_used].reshape(DEC_BATCH, n_pages).astype(jnp.int32),
        'mem_prompt': nrm((BATCH, N_MEM, D_MODEL), 1.0),
        'g_pre_mix': 1.0 + nrm((L, D_MODEL), 0.02),
        'g_post_mix': 1.0 + nrm((L, D_MODEL), 0.02),
        'g_pre_ffn': 1.0 + nrm((L, D_MODEL), 0.02),
        'g_post_ffn': 1.0 + nrm((L, D_MODEL), 0.02),
        'g_mem': 1.0 + nrm((L, D_MODEL), 0.02),
        'w_in': nrm((L, D_MODEL, N_IN), D_MODEL ** -0.5),
        'nsa_pe': nrm((L, CMP_BLOCK, 2, HEAD_DIM), 0.1),
        'nsa_phi_w1': nrm((L, 2, CMP_BLOCK * HEAD_DIM, PHI_HIDDEN), (CMP_BLOCK * HEAD_DIM) ** -0.5),
        'nsa_phi_w2': nrm((L, 2, PHI_HIDDEN, HEAD_DIM), PHI_HIDDEN ** -0.5),
        'rwkv_mu': jax.random.uniform(next(ks), (L, RWKV_IN), jnp.float32),
        'rwkv_w0': jax.random.uniform(next(ks), (L, BRANCH_W), jnp.float32, -6.0, 1.0),
        'rwkv_w2': nrm((L, RWKV_DECAY_LORA, BRANCH_W), 0.5 * RWKV_DECAY_LORA ** -0.5),
        'rwkv_a0': nrm((L, BRANCH_W), 0.1),
        'rwkv_a2': nrm((L, RWKV_AAA_LORA, BRANCH_W), 0.5 * RWKV_AAA_LORA ** -0.5),
        'rwkv_g2': nrm((L, RWKV_GATE_LORA, BRANCH_W), RWKV_GATE_LORA ** -0.5),
        'rwkv_k_k': 0.85 + nrm((L, BRANCH_W), 0.02),
        'rwkv_k_a': 1.0 + nrm((L, BRANCH_W), 0.02),
        'rwkv_r_k': nrm((L, BRANCH_W), 0.1),
        'rwkv_ln_w': 1.0 + nrm((L, BRANCH_W), 0.02),
        'rwkv_ln_b': nrm((L, BRANCH_W), 0.02),
        'w_mem_kv': nrm((L, D_MODEL, 2 * CROSS_HEADS * HEAD_DIM), D_MODEL ** -0.5),
        'w_branch': nrm((L, N_BRANCH, BRANCH_W, D_MODEL), BRANCH_W ** -0.5),
        'w_out': nrm((L, D_MODEL, D_MODEL), D_MODEL ** -0.5),
        'w_ffn_gate': nrm((L, D_MODEL, D_FF), D_MODEL ** -0.5),
        'w_ffn_up': nrm((L, D_MODEL, D_FF), D_MODEL ** -0.5),
        'w_ffn_down': nrm((L, D_FF, D_MODEL), D_FF ** -0.5),
    }


def reference(x_prompt, x_sample, cache_nsa_cmp, cache_nsa_slc, cache_nsa_win, cache_moba, cache_mem,
              state_rwkv, state_shift, page_table, mem_prompt,
              g_pre_mix, g_post_mix, g_pre_ffn, g_post_ffn, g_mem, w_in, nsa_pe, nsa_phi_w1, nsa_phi_w2,
              rwkv_mu, rwkv_w0, rwkv_w2, rwkv_a0, rwkv_a2, rwkv_g2, rwkv_k_k, rwkv_k_a, rwkv_r_k,
              rwkv_ln_w, rwkv_ln_b, w_mem_kv, w_branch, w_out, w_ffn_gate, w_ffn_up, w_ffn_down):
    B = x_prompt.shape[0]
    past_len = page_table.shape[1] * PAGE_SIZE
    dt = x_prompt.dtype
    empty_kv = jnp.zeros((B, 0, 2, HEAD_DIM), dt)
    empty_moba = jnp.zeros((B, 0, 2, MOBA_HEADS, HEAD_DIM), dt)
    zero_state = jnp.zeros((B, RWKV_HEADS, HEAD_DIM, HEAD_DIM), jnp.float32)
    zero_shift = jnp.zeros((B, RWKV_IN), dt)
    xp, xs = x_prompt, x_sample
    outs_p, outs_s, mem_out = [], [], []
    for l in range(DEPTH):
        prm = {'g_pre_mix': g_pre_mix[l], 'g_post_mix': g_post_mix[l], 'g_pre_ffn': g_pre_ffn[l],
               'g_post_ffn': g_post_ffn[l], 'w_in': w_in[l], 'nsa_pe': nsa_pe[l],
               'nsa_phi_w1': nsa_phi_w1[l], 'nsa_phi_w2': nsa_phi_w2[l], 'rwkv_mu': rwkv_mu[l],
               'rwkv_w0': rwkv_w0[l], 'rwkv_w2': rwkv_w2[l], 'rwkv_a0': rwkv_a0[l], 'rwkv_a2': rwkv_a2[l],
               'rwkv_g2': rwkv_g2[l], 'rwkv_k_k': rwkv_k_k[l], 'rwkv_k_a': rwkv_k_a[l],
               'rwkv_r_k': rwkv_r_k[l], 'rwkv_ln_w': rwkv_ln_w[l], 'rwkv_ln_b': rwkv_ln_b[l],
               'w_branch': w_branch[l], 'w_out': w_out[l], 'w_ffn_gate': w_ffn_gate[l],
               'w_ffn_up': w_ffn_up[l], 'w_ffn_down': w_ffn_down[l]}
        mem_kv = (rmsnorm(mem_prompt, g_mem[l]) @ w_mem_kv[l]).reshape(B, N_MEM, 2, CROSS_HEADS, HEAD_DIM)
        xp, new_p = decoder_layer(xp, 0, mem_kv, empty_kv, empty_kv, empty_kv, empty_moba,
                                  zero_state, zero_shift, prm)
        xs, new_s = decoder_layer(xs, past_len, cache_mem[l],
                                  gather_pages(cache_nsa_cmp[l], page_table),
                                  gather_pages(cache_nsa_slc[l], page_table),
                                  cache_nsa_win[l],
                                  gather_pages(cache_moba[l], page_table),
                                  state_rwkv[l], state_shift[l], prm)
        outs_p.append(new_p)
        outs_s.append(new_s)
        mem_out.append(mem_kv)

    def stack_layers(outs, i):
        return jnp.stack([o[i] for o in outs], axis=0)

    return (xp, xs,
            stack_layers(outs_p, 0), stack_layers(outs_s, 0),
            stack_layers(outs_p, 1), stack_layers(outs_s, 1),
            stack_layers(outs_p, 2), stack_layers(outs_s, 2),
            stack_layers(outs_p, 3), stack_layers(outs_s, 3),
            jnp.stack(mem_out, axis=0),
            stack_layers(outs_p, 4), stack_layers(outs_s, 4),
            stack_layers(outs_p, 5), stack_layers(outs_s, 5))
```

```python
import functools

import jax
import jax.numpy as jnp
import numpy as np
from jax import lax
from jax.experimental import pallas as pl
from jax.experimental.pallas import tpu as pltpu

F32 = jnp.float32
BF16 = jnp.bfloat16

HEAD_DIM = 64
N_HEADS = 4
BRANCH_W = N_HEADS * HEAD_DIM
ROT_HALF = 8
ROPE_THETA = 500000.0
CMP_BLOCK = 32
CMP_STRIDE = 16
SLC_BLOCK = 64
N_SELECT = 16
WINDOW = 512
MOBA_BLOCK = 256
MOBA_TOPK = 3
RWKV_IN = 1056
RWKV_PAD = 1152
NSA_PAD = 768
RWKV_GN_EPS = 64e-5
RWKV_CHUNK = 64
NEG = -1e30
SCALE = HEAD_DIM ** -0.5
LANES = 128
VMEM_LIMIT = 48 * 1024 * 1024


def _iota(shape, dim):
    return lax.broadcasted_iota(jnp.int32, shape, dim)


def _dot(a, b):
    return lax.dot_general(a, b, (((1,), (0,)), ((), ())), preferred_element_type=F32)


def _dot_nt(a, b):
    return lax.dot_general(a, b, (((1,), (1,)), ((), ())), preferred_element_type=F32)


def _dot_tn(a, b):
    return lax.dot_general(a, b, (((0,), (0,)), ((), ())), preferred_element_type=F32)


def _split(x):
    hi = x.astype(BF16)
    lo = (x - hi.astype(F32)).astype(BF16)
    return hi, lo


def _dot3(a, b, dot=_dot):
    ah, al = _split(a)
    bh, bl = _split(b)
    return dot(ah, bh) + dot(al, bh) + dot(ah, bl)


def _dot2_exact_rhs(a, b_bf16):
    ah, al = _split(a)
    return _dot(ah, b_bf16) + _dot(al, b_bf16)


def _sigmoid(x):
    return 1.0 / (1.0 + jnp.exp(-x))


def _rms(x, g, eps=1e-6):
    return x * lax.rsqrt(jnp.mean(x * x, axis=-1, keepdims=True) + eps) * g


def _msoftmax(s, mask):
    s = jnp.where(mask, s, NEG)
    p = jnp.where(mask, jnp.exp(s - jnp.max(s, axis=-1, keepdims=True)), 0.0)
    return p / jnp.maximum(jnp.sum(p, axis=-1, keepdims=True), 1e-30)


def _flash_step(q, k, v, mask, m, l, acc):
    s = jnp.where(mask, _dot_nt(q, k), NEG)
    m_new = jnp.maximum(m, jnp.max(s, axis=-1, keepdims=True))
    p = jnp.where(mask, jnp.exp(s - m_new), 0.0)
    alpha = jnp.exp(m - m_new)
    l = alpha * l + jnp.sum(p, axis=-1, keepdims=True)
    acc = alpha * acc + _dot(p.astype(BF16), v)
    return m_new, l, acc


def _rope(x, tab, half_only=False):
    c, sa, sb = tab[0], tab[1], tab[2]
    if half_only:
        first = _iota(x.shape, 1) < HEAD_DIM
        c = jnp.where(first, c, 1.0)
        sa = jnp.where(first, sa, 0.0)
        sb = jnp.where(first, sb, 0.0)
    return x * c + pltpu.roll(x, LANES - ROT_HALF, 1) * sa + pltpu.roll(x, ROT_HALF, 1) * sb


def _rope_heads(x, tab):
    return jnp.concatenate([_rope(x[:, :LANES], tab), _rope(x[:, LANES:], tab)], axis=1)


def _rank_lt(score, n_real, k):
    lane = _iota(score.shape, 1)
    cnt = jnp.zeros(score.shape, jnp.int32)
    for i in range(n_real):
        ci = score[:, i:i + 1]
        beats = (ci > score) | ((ci == score) & (lane > i))
        cnt = cnt + beats.astype(jnp.int32)
    return cnt < k


def _pick_tile(n, target, mult=8):
    if n <= target:
        return n
    for t in range(target, 0, -1):
        if n % t == 0 and t % mult == 0:
            return t
    return n


def _params(*sem):
    return pltpu.CompilerParams(dimension_semantics=sem, vmem_limit_bytes=VMEM_LIMIT)


def _rms_cast_kernel(x_ref, g_ref, o_ref):
    o_ref[...] = _rms(x_ref[...], g_ref[...]).astype(BF16)


def rms_cast(x, g):
    m, d = x.shape
    tm = _pick_tile(m, 1024, 16)
    return pl.pallas_call(
        _rms_cast_kernel, grid=(m // tm,),
        in_specs=[pl.BlockSpec((tm, d), lambda i: (i, 0)), pl.BlockSpec((1, d), lambda i: (0, 0))],
        out_specs=pl.BlockSpec((tm, d), lambda i: (i, 0)),
        out_shape=jax.ShapeDtypeStruct((m, d), BF16),
        compiler_params=_params("parallel"), name="rms_cast",
    )(x, g.reshape(1, d))


def _mm_kernel(a_ref, b_ref, o_ref):
    o_ref[...] = _dot(a_ref[...], b_ref[...]).astype(o_ref.dtype)


def mm(a, b, out_dtype=F32):
    m, k = a.shape
    n = b.shape[1]
    tm = _pick_tile(m, 1024, 16)
    tn = n if n <= 1536 else _pick_tile(n, 1024, LANES)
    return pl.pallas_call(
        _mm_kernel, grid=(m // tm, n // tn),
        in_specs=[pl.BlockSpec((tm, k), lambda i, j: (i, 0)), pl.BlockSpec((k, tn), lambda i, j: (0, j))],
        out_specs=pl.BlockSpec((tm, tn), lambda i, j: (i, j)),
        out_shape=jax.ShapeDtypeStruct((m, n), out_dtype),
        compiler_params=_params("parallel", "parallel"), name="mm",
    )(a, b)


def _ffn_kernel(x_ref, g1_ref, wg_ref, wu_ref, wd_ref, g2_ref, o_ref, h_ref, acc_ref):
    f = pl.program_id(1)

    @pl.when(f == 0)
    def _():
        h_ref[...] = _rms(x_ref[...], g1_ref[...]).astype(BF16)
        acc_ref[...] = jnp.zeros_like(acc_ref)

    h = h_ref[...]
    gate = _dot(h, wg_ref[...])
    up = _dot(h, wu_ref[...])
    act = (gate * _sigmoid(gate) * up).astype(BF16)
    acc_ref[...] += _dot(act, wd_ref[...])

    @pl.when(f == pl.num_programs(1) - 1)
    def _():
        o_ref[...] = x_ref[...] + _rms(acc_ref[...], g2_ref[...])


def ffn(x, g_pre, wg, wu, wd, g_post):
    m, d = x.shape
    dff = wg.shape[1]
    tm = _pick_tile(m, 512, 16)
    tf = _pick_tile(dff, 256, LANES)
    return pl.pallas_call(
        _ffn_kernel, grid=(m // tm, dff // tf),
        in_specs=[pl.BlockSpec((tm, d), lambda i, f: (i, 0)),
                  pl.BlockSpec((1, d), lambda i, f: (0, 0)),
                  pl.BlockSpec((d, tf), lambda i, f: (0, f)),
                  pl.BlockSpec((d, tf), lambda i, f: (0, f)),
                  pl.BlockSpec((tf, d), lambda i, f: (f, 0)),
                  pl.BlockSpec((1, d), lambda i, f: (0, 0))],
        out_specs=pl.BlockSpec((tm, d), lambda i, f: (i, 0)),
        out_shape=jax.ShapeDtypeStruct((m, d), F32),
        scratch_shapes=[pltpu.VMEM((tm, d), BF16), pltpu.VMEM((tm, d), F32)],
        compiler_params=_params("parallel", "arbitrary"), name="ffn",
    )(x, g_pre.reshape(1, d), wg, wu, wd, g_post.reshape(1, d))


def _merge_kernel(x_ref, o0_ref, o1_ref, o2_ref, o3_ref, gate_ref, wb_ref, wo_ref, g_ref, out_ref):
    d = x_ref.shape[1]
    merged = None
    for b, o_ref in enumerate((o0_ref, o1_ref, o2_ref, o3_ref)):
        term = _sigmoid(gate_ref[:, b * d:(b + 1) * d]) * _dot(o_ref[...], wb_ref[b])
        merged = term if merged is None else merged + term
    y = _dot(merged.astype(BF16), wo_ref[...])
    out_ref[...] = x_ref[...] + _rms(y, g_ref[...])


def merge(x, outs, gates, w_branch, w_out, g_post):
    m, d = x.shape
    bw = outs[0].shape[1]
    tm = _pick_tile(m, 256, 16)
    row = lambda i: (i, 0)
    return pl.pallas_call(
        _merge_kernel, grid=(m // tm,),
        in_specs=[pl.BlockSpec((tm, d), row)] + [pl.BlockSpec((tm, bw), row)] * 4 + [
            pl.BlockSpec((tm, 4 * d), row),
            pl.BlockSpec((4, bw, d), lambda i: (0, 0, 0)),
            pl.BlockSpec((d, d), lambda i: (0, 0)),
            pl.BlockSpec((1, d), lambda i: (0, 0))],
        out_specs=pl.BlockSpec((tm, d), row),
        out_shape=jax.ShapeDtypeStruct((m, d), F32),
        compiler_params=_params("parallel"), name="merge",
    )(x, *outs, gates, w_branch, w_out, g_post.reshape(1, d))


def _nsa_rope_kernel(z_ref, tab_ref, slc_ref, win_ref):
    z = z_ref[0]
    tab = tab_ref[...]
    slc_ref[0] = _rope(z[:, 384:512], tab, half_only=True)
    win_ref[0] = _rope(z[:, 512:640], tab, half_only=True)


def nsa_rope(z_nsa, tab):
    b, t, w = z_nsa.shape
    tt = _pick_tile(t, 512)
    out = jax.ShapeDtypeStruct((b, t, LANES), F32)
    return pl.pallas_call(
        _nsa_rope_kernel, grid=(b, t // tt),
        in_specs=[pl.BlockSpec((1, tt, w), lambda i, j: (i, j, 0)),
                  pl.BlockSpec((3, tt, LANES), lambda i, j: (0, j, 0))],
        out_specs=[pl.BlockSpec((1, tt, LANES), lambda i, j: (i, j, 0))] * 2,
        out_shape=[out, out],
        compiler_params=_params("parallel", "parallel"), name="nsa_rope",
    )(z_nsa, tab)


def _moba_rope_kernel(z_ref, tab_ref, kv_ref, mean_ref):
    z = z_ref[0]
    kr = _rope_heads(z[:, BRANCH_W:2 * BRANCH_W], tab_ref[...])
    kv_ref[0] = jnp.concatenate([kr, z[:, 2 * BRANCH_W:3 * BRANCH_W]], axis=1)
    mean_ref[0, 0] = jnp.sum(kr, axis=0, keepdims=True) * (1.0 / MOBA_BLOCK)


def moba_rope(z_moba, tab):
    b, t, w = z_moba.shape
    nb = t // MOBA_BLOCK
    return pl.pallas_call(
        _moba_rope_kernel, grid=(b, nb),
        in_specs=[pl.BlockSpec((1, MOBA_BLOCK, w), lambda i, j: (i, j, 0)),
                  pl.BlockSpec((3, MOBA_BLOCK, LANES), lambda i, j: (0, j, 0))],
        out_specs=[pl.BlockSpec((1, MOBA_BLOCK, 2 * BRANCH_W), lambda i, j: (i, j, 0)),
                   pl.BlockSpec((1, 1, 1, BRANCH_W), lambda i, j: (i, j, 0, 0))],
        out_shape=[jax.ShapeDtypeStruct((b, t, 2 * BRANCH_W), F32),
                   jax.ShapeDtypeStruct((b, nb, 1, BRANCH_W), F32)],
        compiler_params=_params("parallel", "parallel"), name="moba_rope",
    )(z_moba, tab)


def _compress(g, w1_ref, pe_ref, w2_ref):
    r = g.shape[0]
    ab = _dot(g.astype(BF16), w1_ref[...])
    pe = _dot(pe_ref[...].astype(BF16), w1_ref[...])
    half = ab.shape[1] // 2
    nxt = pltpu.roll(ab[:, half:], r - 1, 0)
    pre = ab[:, :half] + nxt + pe[0:1, :half] + pe[1:2, half:]
    hid = pre * _sigmoid(pre)
    return _dot(hid.astype(BF16), w2_ref[...])


def _compress_kernel(g_ref, w1_ref, pe_ref, w2_ref, o_ref):
    o_ref[0] = _compress(g_ref[0], w1_ref, pe_ref, w2_ref)


def compress(g, w1c, pe2, w2c):
    b, r, w = g.shape
    const = lambda i: (0, 0)
    return pl.pallas_call(
        _compress_kernel, grid=(b,),
        in_specs=[pl.BlockSpec((1, r, w), lambda i: (i, 0, 0)),
                  pl.BlockSpec(w1c.shape, const), pl.BlockSpec(pe2.shape, const),
                  pl.BlockSpec(w2c.shape, const)],
        out_specs=pl.BlockSpec((1, r, LANES), lambda i: (i, 0, 0)),
        out_shape=jax.ShapeDtypeStruct((b, r, LANES), F32),
        compiler_params=_params("parallel"), name="nsa_compress",
    )(g, w1c, pe2, w2c)


def _cmp_attention(q, ck, cv, cmask, rows_real=None):
    outs, psum = [], None
    for qh in q:
        p = _msoftmax(_dot_nt(qh.astype(BF16), ck), cmask)
        outs.append(_dot(p.astype(BF16), cv))
        psum = p if psum is None else psum + p
    return outs, psum


def _select_blocks(imp, t_pos, n_slc, k_sel):
    j = _iota(imp.shape, 1)
    own = lax.shift_right_logical(t_pos, 6)
    causal = (j * SLC_BLOCK <= t_pos) & (j < n_slc)
    forced = (j == 0) | (j == own) | (j == own - 1)
    score = jnp.where(forced, -NEG, jnp.where(causal, imp, NEG))
    return _rank_lt(score, n_slc, k_sel) & causal


def _nsa_attn_kernel(z_ref, tab_ref, cmp_ref, slc_ref, win_ref, cover_ref, expand_ref, o_ref,
                     *, tq, n_cmp, n_slc, k_sel):
    qi = pl.program_id(1)
    z = z_ref[0]
    q = z[:, :BRANCH_W] * SCALE
    qr = _rope_heads(q, tab_ref[...])
    t_pos = qi * tq + _iota((tq, 1), 0)
    heads = [slice(h * HEAD_DIM, (h + 1) * HEAD_DIM) for h in range(N_HEADS)]

    cmpkv = cmp_ref[0]
    ck = cmpkv[:, :HEAD_DIM].astype(BF16)
    cv = cmpkv[:, HEAD_DIM:].astype(BF16)
    n = _iota((1, cmpkv.shape[0]), 1)
    cmask = (n * CMP_STRIDE + (CMP_BLOCK - 1) <= t_pos) & (n < n_cmp)
    o_cmp, psum = _cmp_attention([q[:, hs] for hs in heads], ck, cv, cmask)
    imp = _dot2_exact_rhs(psum, cover_ref[...])
    sel = _select_blocks(imp, t_pos, n_slc, k_sel).astype(BF16)

    qrb = [qr[:, hs].astype(BF16) for hs in heads]
    init = tuple((jnp.full((tq, 1), NEG, F32), jnp.zeros((tq, 1), F32), jnp.zeros((tq, HEAD_DIM), F32))
                 for _ in heads)

    def slc_body(kt, carry):
        rows = slc_ref[0, pl.ds(kt * tq, tq), :]
        k = rows[:, :HEAD_DIM].astype(BF16)
        v = rows[:, HEAD_DIM:].astype(BF16)
        kpos = kt * tq + _iota((1, tq), 1)
        mask = (_dot(sel, expand_ref[:, pl.ds(pl.multiple_of(kt * tq, tq), tq)]) > 0.5) & (kpos <= t_pos)
        return tuple(_flash_step(qrb[h], k, v, mask, *carry[h]) for h in range(N_HEADS))

    def win_body(kt, carry):
        rows = win_ref[0, pl.ds(kt * tq, tq), :]
        k = rows[:, :HEAD_DIM].astype(BF16)
        v = rows[:, HEAD_DIM:].astype(BF16)
        d = t_pos - (kt * tq + _iota((1, tq), 1))
        mask = (d >= 0) & (d <= WINDOW)
        return tuple(_flash_step(qrb[h], k, v, mask, *carry[h]) for h in range(N_HEADS))

    slc = lax.fori_loop(0, qi + 1, slc_body, init)
    n_win = -(-WINDOW // tq)
    win = lax.fori_loop(jnp.maximum(qi - n_win, 0), qi + 1, win_body, init)

    g = _sigmoid(z[:, 640:640 + 3 * N_HEADS])
    outs = []
    for h in range(N_HEADS):
        o_slc = slc[h][2] / jnp.maximum(slc[h][1], 1e-30)
        o_win = win[h][2] / jnp.maximum(win[h][1], 1e-30)
        outs.append(g[:, 3 * h:3 * h + 1] * o_cmp[h] + g[:, 3 * h + 1:3 * h + 2] * o_slc
                    + g[:, 3 * h + 2:3 * h + 3] * o_win)
    o_ref[0] = jnp.concatenate(outs, axis=1).astype(o_ref.dtype)


def nsa_attn(z_nsa, tab, cmp_kv, new_slc, new_win, cover, expand, n_cmp, n_slc):
    b, t, w = z_nsa.shape
    tq = _pick_tile(t, 256)
    kern = functools.partial(_nsa_attn_kernel, tq=tq, n_cmp=n_cmp, n_slc=n_slc, k_sel=min(N_SELECT, n_slc))
    full = lambda i, j: (i, 0, 0)
    const = lambda i, j: (0, 0)
    return pl.pallas_call(
        kern, grid=(b, t // tq),
        in_specs=[pl.BlockSpec((1, tq, w), lambda i, j: (i, j, 0)),
                  pl.BlockSpec((3, tq, LANES), lambda i, j: (0, j, 0)),
                  pl.BlockSpec((1,) + cmp_kv.shape[1:], full),
                  pl.BlockSpec((1, t, LANES), full), pl.BlockSpec((1, t, LANES), full),
                  pl.BlockSpec(cover.shape, const), pl.BlockSpec(expand.shape, const)],
        out_specs=pl.BlockSpec((1, tq, BRANCH_W), lambda i, j: (i, j, 0)),
        out_shape=jax.ShapeDtypeStruct((b, t, BRANCH_W), BF16),
        compiler_params=_params("parallel", "parallel"), name="nsa_attn",
    )(z_nsa, tab, cmp_kv, new_slc, new_win, cover, expand)


def _moba_attn_kernel(z_ref, tab_ref, kv_ref, mean_ref, o_ref, *, tq, nb, topk):
    qi = pl.program_id(1)
    q = z_ref[0][:, :BRANCH_W] * SCALE
    qr = _rope_heads(q, tab_ref[...])
    t_pos = qi * tq + _iota((tq, 1), 0)
    own = lax.shift_right_logical(t_pos, 8)
    means = mean_ref[0]
    jb = _iota((1, means.shape[0]), 1)
    past = (jb < own) & (jb < nb)
    heads = [slice(h * HEAD_DIM, (h + 1) * HEAD_DIM) for h in range(N_HEADS)]

    sel = []
    for hs in heads:
        gate = _dot3(qr[:, hs], means[:, hs], _dot_nt)
        score = jnp.where(past, gate, NEG)
        sel.append((_rank_lt(score, nb, topk) & past).astype(F32))

    qrb = [qr[:, hs].astype(BF16) for hs in heads]
    init = tuple((jnp.full((tq, 1), NEG, F32), jnp.zeros((tq, 1), F32), jnp.zeros((tq, HEAD_DIM), F32))
                 for _ in heads)

    def body(kt, carry):
        rows = kv_ref[0, pl.ds(kt * tq, tq), :]
        kpos = kt * tq + _iota((1, tq), 1)
        out = []
        for h, hs in enumerate(heads):
            k = rows[:, hs].astype(BF16)
            v = rows[:, BRANCH_W + h * HEAD_DIM:BRANCH_W + (h + 1) * HEAD_DIM].astype(BF16)
            picked = jnp.sum(jnp.where(jb == kt, sel[h], 0.0), axis=1, keepdims=True) > 0.5
            is_own = own == kt
            mask = (is_own & (kpos <= t_pos)) | (jnp.logical_not(is_own) & picked)
            out.append(_flash_step(qrb[h], k, v, mask, *carry[h]))
        return tuple(out)

    res = lax.fori_loop(0, qi + 1, body, init)
    outs = [res[h][2] / jnp.maximum(res[h][1], 1e-30) for h in range(N_HEADS)]
    o_ref[0] = jnp.concatenate(outs, axis=1).astype(o_ref.dtype)


def moba_attn(z_moba, tab, new_moba, means):
    b, t, w = z_moba.shape
    assert t % MOBA_BLOCK == 0
    tq = MOBA_BLOCK
    nb = t // MOBA_BLOCK
    kern = functools.partial(_moba_attn_kernel, tq=tq, nb=nb, topk=min(MOBA_TOPK, nb))
    full = lambda i, j: (i, 0, 0)
    return pl.pallas_call(
        kern, grid=(b, t // tq),
        in_specs=[pl.BlockSpec((1, tq, w), lambda i, j: (i, j, 0)),
                  pl.BlockSpec((3, tq, LANES), lambda i, j: (0, j, 0)),
                  pl.BlockSpec((1, t, 2 * BRANCH_W), full),
                  pl.BlockSpec((1, nb, BRANCH_W), full)],
        out_specs=pl.BlockSpec((1, tq, BRANCH_W), lambda i, j: (i, j, 0)),
        out_shape=jax.ShapeDtypeStruct((b, t, BRANCH_W), BF16),
        compiler_params=_params("parallel", "parallel"), name="moba_attn",
    )(z_moba, tab, new_moba, means)


def _cross_attn_kernel(q_ref, mem_ref, o_ref):
    q = q_ref[0] * SCALE
    mem = mem_ref[0]
    outs = []
    for h in range(N_HEADS):
        hs = slice(h * HEAD_DIM, (h + 1) * HEAD_DIM)
        s = _dot_nt(q[:, hs].astype(BF16), mem[:, hs].astype(BF16))
        p = jnp.exp(s - jnp.max(s, axis=-1, keepdims=True))
        p = p / jnp.sum(p, axis=-1, keepdims=True)
        v = mem[:, BRANCH_W + h * HEAD_DIM:BRANCH_W + (h + 1) * HEAD_DIM].astype(BF16)
        outs.append(_dot(p.astype(BF16), v))
    o_ref[0] = jnp.concatenate(outs, axis=1).astype(o_ref.dtype)


def cross_attn(q, mem_kv):
    b, t, w = q.shape
    n_mem = mem_kv.shape[1]
    tq = _pick_tile(t, 512)
    return pl.pallas_call(
        _cross_attn_kernel, grid=(b, t // tq),
        in_specs=[pl.BlockSpec((1, tq, w), lambda i, j: (i, j, 0)),
                  pl.BlockSpec((1, n_mem, 2 * BRANCH_W), lambda i, j: (i, 0, 0))],
        out_specs=pl.BlockSpec((1, tq, BRANCH_W), lambda i, j: (i, j, 0)),
        out_shape=jax.ShapeDtypeStruct((b, t, BRANCH_W), BF16),
        compiler_params=_params("parallel", "parallel"), name="cross_attn",
    )(q, mem_kv)


def _head_sum(x, bd_ref):
    return _dot2_exact_rhs(x, bd_ref[...])


def _rwkv_prep(z, zprev, vec_ref, wa_ref, g2_ref, bd_ref):
    mu = vec_ref[0:1, :]
    zm = z + (zprev - z) * mu
    r = zm[:, 0:256]
    k = zm[:, 256:512]
    v = zm[:, 512:768]
    lora = zm[:, 768:896]
    lora = jnp.where(_iota(lora.shape, 1) < 64, jnp.tanh(lora), lora)
    wa = _dot(lora.astype(BF16), wa_ref[...])
    wl = vec_ref[1:2, 0:256] + wa[:, :256]
    x = -wl
    softplus = jnp.maximum(x, 0.0) + jnp.log(1.0 + jnp.exp(-jnp.abs(x)))
    logdecay = -jnp.exp(-softplus - 0.5)
    a = _sigmoid(vec_ref[1:2, 256:512] + wa[:, 256:])
    g = _dot(_sigmoid(zm[:, 896:1152]).astype(BF16), g2_ref[...])
    kkr = k * vec_ref[2:3, 0:256]
    kk = kkr * lax.rsqrt(jnp.maximum(_head_sum(kkr * kkr, bd_ref), 1e-24))
    k2 = k * (1.0 + (a - 1.0) * vec_ref[3:4, 0:256])
    bonus = _head_sum(r * k2 * vec_ref[4:5, 0:256], bd_ref) * v
    return r, k2, v, kk, kk * a, logdecay, g, bonus


def _rwkv_finish(y, bonus, g, vec_ref, bd_ref):
    mu = _head_sum(y, bd_ref) * (1.0 / HEAD_DIM)
    d = y - mu
    var = _head_sum(d * d, bd_ref) * (1.0 / HEAD_DIM)
    yn = d * lax.rsqrt(var + RWKV_GN_EPS) * vec_ref[5:6, 0:256] + vec_ref[6:7, 0:256]
    return (yn + bonus) * g


def _tri_inv(low):
    c = low.shape[0]
    eye = (_iota((c, c), 0) == _iota((c, c), 1)).astype(F32)
    t = eye - low
    lp = _dot3(low, low)
    n = 2
    while n < c:
        t = t + _dot3(t, lp)
        n *= 2
        if n < c:
            lp = _dot3(lp, lp)
    return t


def _rwkv_seq_kernel(z_ref, shift_ref, s0_ref, vec_ref, wa_ref, g2_ref, bd_ref, tri_ref,
                     o_ref, s_ref, last_ref, r_s, k_s, v_s, kk_s, b_s, ld_s, y_s, *, tt):
    ti = pl.program_id(1)

    @pl.when(ti == 0)
    def _():
        last_ref[...] = shift_ref[0]
        s_ref[0] = s0_ref[0]

    z = z_ref[0]
    rolled = pltpu.roll(z, 1, 0)
    zprev = jnp.where(_iota(z.shape, 0) == 0, last_ref[...], rolled)
    last_ref[...] = z[tt - 1:tt, :]
    r, k2, v, kk, b, ld, g, bonus = _rwkv_prep(z, zprev, vec_ref, wa_ref, g2_ref, bd_ref)
    r_s[...] = r
    k_s[...] = k2
    v_s[...] = v
    kk_s[...] = kk
    b_s[...] = b
    ld_s[...] = ld

    c = RWKV_CHUNK
    ri = _iota((c, c), 0)
    ci = _iota((c, c), 1)
    strict = ri > ci
    incl = ri >= ci

    def chunk(ck, carry):
        rows = pl.ds(pl.multiple_of(ck * c, c), c)
        ldc = ld_s[rows, :]
        cum = _dot3(tri_ref[...], ldc)
        p_in = jnp.exp(cum)
        p_ex = jnp.exp(cum - ldc)
        inv_p = jnp.exp(-cum)
        kkt = kk_s[rows, :] * p_ex
        bt = b_s[rows, :] * inv_p
        kt = k_s[rows, :] * inv_p
        rt = r_s[rows, :] * p_in
        vv = v_s[rows, :]
        p_end = p_in[c - 1:c, :]
        ys = []
        for h in range(N_HEADS):
            hs = slice(h * HEAD_DIM, (h + 1) * HEAD_DIM)
            s_h = s_ref[0, h]
            low = jnp.where(strict, _dot3(kkt[:, hs], bt[:, hs], _dot_nt), 0.0)
            gm = jnp.where(strict, _dot3(kkt[:, hs], kt[:, hs], _dot_nt), 0.0)
            arb = jnp.where(incl, _dot3(rt[:, hs], bt[:, hs], _dot_nt), 0.0)
            ark = jnp.where(incl, _dot3(rt[:, hs], kt[:, hs], _dot_nt), 0.0)
            rhs = _dot3(kkt[:, hs], s_h, _dot_nt) + _dot3(gm, vv[:, hs])
            u = _dot3(_tri_inv(low), rhs)
            ys.append(_dot3(rt[:, hs], s_h, _dot_nt) - _dot3(arb, u) + _dot3(ark, vv[:, hs]))
            s_ref[0, h] = (s_h - _dot3(u, bt[:, hs], _dot_tn) + _dot3(vv[:, hs], kt[:, hs], _dot_tn)) * p_end[:, hs]
        y_s[rows, :] = jnp.concatenate(ys, axis=1)
        return carry

    lax.fori_loop(0, tt // c, chunk, 0)
    o_ref[0] = _rwkv_finish(y_s[...], bonus, g, vec_ref, bd_ref).astype(o_ref.dtype)


def rwkv_seq(z, shift, s0, vec, wa, g2, bd, tri):
    b, t, w = z.shape
    tt = _pick_tile(t, 512, RWKV_CHUNK)
    assert tt % RWKV_CHUNK == 0
    const = lambda i, j: (0, 0)
    scr = pltpu.VMEM((tt, BRANCH_W), F32)
    return pl.pallas_call(
        functools.partial(_rwkv_seq_kernel, tt=tt), grid=(b, t // tt),
        in_specs=[pl.BlockSpec((1, tt, w), lambda i, j: (i, j, 0)),
                  pl.BlockSpec((1, 1, w), lambda i, j: (i, 0, 0)),
                  pl.BlockSpec((1, N_HEADS, HEAD_DIM, HEAD_DIM), lambda i, j: (i, 0, 0, 0)),
                  pl.BlockSpec(vec.shape, const), pl.BlockSpec(wa.shape, const),
                  pl.BlockSpec(g2.shape, const), pl.BlockSpec(bd.shape, const),
                  pl.BlockSpec(tri.shape, const)],
        out_specs=[pl.BlockSpec((1, tt, BRANCH_W), lambda i, j: (i, j, 0)),
                   pl.BlockSpec((1, N_HEADS, HEAD_DIM, HEAD_DIM), lambda i, j: (i, 0, 0, 0))],
        out_shape=[jax.ShapeDtypeStruct((b, t, BRANCH_W), BF16),
                   jax.ShapeDtypeStruct((b, N_HEADS, HEAD_DIM, HEAD_DIM), F32)],
        scratch_shapes=[pltpu.VMEM((1, w), F32)] + [scr] * 7,
        compiler_params=_params("parallel", "arbitrary"), name="rwkv_seq",
    )(z, shift, s0, vec, wa, g2, bd, tri)


def _rwkv_step_kernel(z_ref, shift_ref, s0_ref, vec_ref, wa_ref, g2_ref, bd_ref, o_ref, s_ref):
    rows = 8
    z = jnp.broadcast_to(z_ref[0], (rows, z_ref.shape[2]))
    zprev = jnp.broadcast_to(shift_ref[0], z.shape)
    r, k2, v, kk, b, ld, g, bonus = _rwkv_prep(z, zprev, vec_ref, wa_ref, g2_ref, bd_ref)
    decay = jnp.exp(ld)
    eye = _iota((HEAD_DIM, HEAD_DIM), 0) == _iota((HEAD_DIM, HEAD_DIM), 1)
    ys = []
    for h in range(N_HEADS):
        hs = slice(h * HEAD_DIM, (h + 1) * HEAD_DIM)
        s_h = s0_ref[0, h]
        sk = jnp.sum(s_h * kk[0:1, hs], axis=1, keepdims=True)
        v_col = jnp.sum(jnp.where(eye, v[0:1, hs], 0.0), axis=1, keepdims=True)
        s_new = s_h * decay[0:1, hs] - sk * b[0:1, hs] + v_col * k2[0:1, hs]
        s_ref[0, h] = s_new
        y_col = jnp.sum(s_new * r[0:1, hs], axis=1, keepdims=True)
        ys.append(jnp.sum(jnp.where(eye, y_col, 0.0), axis=0, keepdims=True))
    y = jnp.broadcast_to(jnp.concatenate(ys, axis=1), (rows, BRANCH_W))
    o_ref[0] = _rwkv_finish(y, bonus, g, vec_ref, bd_ref)[0:1].astype(o_ref.dtype)


def rwkv_step(z, shift, s0, vec, wa, g2, bd):
    b, _, w = z.shape
    const = lambda i: (0, 0)
    return pl.pallas_call(
        _rwkv_step_kernel, grid=(b,),
        in_specs=[pl.BlockSpec((1, 1, w), lambda i: (i, 0, 0)),
                  pl.BlockSpec((1, 1, w), lambda i: (i, 0, 0)),
                  pl.BlockSpec((1, N_HEADS, HEAD_DIM, HEAD_DIM), lambda i: (i, 0, 0, 0)),
                  pl.BlockSpec(vec.shape, const), pl.BlockSpec(wa.shape, const),
                  pl.BlockSpec(g2.shape, const), pl.BlockSpec(bd.shape, const)],
        out_specs=[pl.BlockSpec((1, 1, BRANCH_W), lambda i: (i, 0, 0)),
                   pl.BlockSpec((1, N_HEADS, HEAD_DIM, HEAD_DIM), lambda i: (i, 0, 0, 0))],
        out_shape=[jax.ShapeDtypeStruct((b, 1, BRANCH_W), BF16),
                   jax.ShapeDtypeStruct((b, N_HEADS, HEAD_DIM, HEAD_DIM), F32)],
        compiler_params=_params("parallel"), name="rwkv_step",
    )(z, shift, s0, vec, wa, g2, bd)


def _stack_heads(x):
    rows = [x[:, h * HEAD_DIM:(h + 1) * HEAD_DIM] for h in range(N_HEADS)]
    return jnp.concatenate(rows + [jnp.zeros((8 - N_HEADS, HEAD_DIM), x.dtype)], axis=0)


def _unstack_heads(x):
    return jnp.concatenate([x[h:h + 1, :] for h in range(N_HEADS)], axis=1)


def _nsa_sample_kernel(pt_ref, z_ref, tab_ref, win_ref, w1_ref, pe_ref, w2_ref, cover_ref, expand_ref,
                       cmp_hbm, slc_hbm, o_ref, slc_row_ref, win_row_ref, cbuf, sbuf, sem,
                       *, base, n_pages, page, past, n_cmp, n_slc, k_sel):
    bi = pl.program_id(0)
    lines = page // CMP_STRIDE

    def copies(p):
        pg = pt_ref[bi * n_pages + p] + base
        return (pltpu.make_async_copy(cmp_hbm.at[pg], cbuf.at[pl.ds(p * lines, lines), :], sem.at[0]),
                pltpu.make_async_copy(slc_hbm.at[pg], sbuf.at[pl.ds(p * page, page), :], sem.at[1]))

    for p in range(n_pages):
        for cp in copies(p):
            cp.start()

    z = z_ref[0]
    tab = tab_ref[...]
    q = z[:, :BRANCH_W] * SCALE
    q4 = _stack_heads(q)
    q4r_f32 = _stack_heads(_rope_heads(q, tab))
    q4r = q4r_f32.astype(BF16)
    slc_row = _rope(z[:, 384:512], tab, half_only=True)
    win_row = _rope(z[:, 512:640], tab, half_only=True)
    slc_row_ref[0] = slc_row
    win_row_ref[0] = win_row

    for p in range(n_pages):
        for cp in copies(p):
            cp.wait()

    cmpkv = _compress(cbuf[...], w1_ref, pe_ref, w2_ref)
    ck = cmpkv[:, :HEAD_DIM].astype(BF16)
    cv = cmpkv[:, HEAD_DIM:].astype(BF16)
    n = _iota((1, cmpkv.shape[0]), 1)
    cmask = (n * CMP_STRIDE + (CMP_BLOCK - 1) <= past) & (n < n_cmp)
    p_cmp = _msoftmax(_dot_nt(q4.astype(BF16), ck), cmask)
    o_cmp = _dot(p_cmp.astype(BF16), cv)
    psum = jnp.sum(jnp.where(_iota(p_cmp.shape, 0) < N_HEADS, p_cmp, 0.0), axis=0, keepdims=True)
    imp = _dot2_exact_rhs(jnp.broadcast_to(psum, p_cmp.shape), cover_ref[...])[0:1]
    t_pos = jnp.full((1, 1), past, jnp.int32)
    sel = _select_blocks(imp, t_pos, n_slc, k_sel)

    def with_new_key(s_past, mask, new_row, v_past):
        s_new = jnp.sum(q4r_f32 * new_row[:, :HEAD_DIM], axis=1, keepdims=True)
        s_past = jnp.where(mask, s_past, NEG)
        m = jnp.maximum(jnp.max(s_past, axis=1, keepdims=True), s_new)
        p_past = jnp.where(mask, jnp.exp(s_past - m), 0.0)
        p_new = jnp.exp(s_new - m)
        den = jnp.maximum(jnp.sum(p_past, axis=1, keepdims=True) + p_new, 1e-30)
        return _dot((p_past / den).astype(BF16), v_past) + (p_new / den) * new_row[:, HEAD_DIM:]

    sel_keys = _dot(jnp.broadcast_to(sel.astype(BF16), (8, sel.shape[1])), expand_ref[...])[0:1] > 0.5
    sk = sbuf[:, :HEAD_DIM].astype(BF16)
    sv = sbuf[:, HEAD_DIM:].astype(BF16)
    o_slc = with_new_key(_dot_nt(q4r, sk), sel_keys, slc_row, sv)
    wrows = win_ref[0]
    wk = wrows[:, :HEAD_DIM].astype(BF16)
    wv = wrows[:, HEAD_DIM:].astype(BF16)
    o_win = with_new_key(_dot_nt(q4r, wk), jnp.full((1, wrows.shape[0]), True), win_row, wv)

    g = _sigmoid(z[:, 640:640 + 3 * N_HEADS])
    outs = []
    for h in range(N_HEADS):
        outs.append(g[:, 3 * h:3 * h + 1] * o_cmp[h:h + 1] + g[:, 3 * h + 1:3 * h + 2] * o_slc[h:h + 1]
                    + g[:, 3 * h + 2:3 * h + 3] * o_win[h:h + 1])
    o_ref[0] = jnp.concatenate(outs, axis=1).astype(o_ref.dtype)


def nsa_sample(page_table, z_nsa, tab, win, w1c, pe2, w2c, cover, expand, cache_cmp, cache_slc,
               layer, n_pool, page, n_cmp, n_slc):
    b = z_nsa.shape[0]
    n_pages = page_table.shape[1]
    past = n_pages * page
    lines = page // CMP_STRIDE
    kern = functools.partial(_nsa_sample_kernel, base=layer * n_pool, n_pages=n_pages, page=page, past=past,
                             n_cmp=n_cmp, n_slc=n_slc, k_sel=min(N_SELECT, n_slc))
    const = lambda i, pt: (0, 0)
    row = lambda i, pt: (i, 0, 0)
    grid_spec = pltpu.PrefetchScalarGridSpec(
        num_scalar_prefetch=1, grid=(b,),
        in_specs=[pl.BlockSpec((1, 1, z_nsa.shape[2]), row),
                  pl.BlockSpec(tab.shape, lambda i, pt: (0, 0, 0)),
                  pl.BlockSpec((1,) + win.shape[1:], row),
                  pl.BlockSpec(w1c.shape, const), pl.BlockSpec(pe2.shape, const),
                  pl.BlockSpec(w2c.shape, const), pl.BlockSpec(cover.shape, const),
                  pl.BlockSpec(expand.shape, const),
                  pl.BlockSpec(memory_space=pl.ANY), pl.BlockSpec(memory_space=pl.ANY)],
        out_specs=[pl.BlockSpec((1, 1, BRANCH_W), row), pl.BlockSpec((1, 1, LANES), row),
                   pl.BlockSpec((1, 1, LANES), row)],
        scratch_shapes=[pltpu.VMEM((n_pages * lines, CMP_STRIDE * LANES), F32),
                        pltpu.VMEM((past, LANES), F32),
                        pltpu.SemaphoreType.DMA((2,))])
    return pl.pallas_call(
        kern, grid_spec=grid_spec,
        out_shape=[jax.ShapeDtypeStruct((b, 1, BRANCH_W), BF16),
                   jax.ShapeDtypeStruct((b, 1, LANES), F32), jax.ShapeDtypeStruct((b, 1, LANES), F32)],
        compiler_params=_params("arbitrary"), name="nsa_sample",
    )(page_table.reshape(-1), z_nsa, tab, win, w1c, pe2, w2c, cover, expand, cache_cmp, cache_slc)


def _moba_sample_kernel(pt_ref, z_ref, tab_ref, expand_ref, kv_hbm, o_ref, row_ref, buf, sem,
                        *, base, n_pages, page, nb, topk):
    bi = pl.program_id(0)

    def copy(p):
        pg = pt_ref[bi * n_pages + p] + base
        return pltpu.make_async_copy(kv_hbm.at[pg], buf.at[pl.ds(p * page, page), :], sem.at[0])

    for p in range(n_pages):
        copy(p).start()

    z = z_ref[0]
    tab = tab_ref[...]
    q = z[:, :BRANCH_W] * SCALE
    qr = _rope_heads(q, tab)
    k_new = _rope_heads(z[:, BRANCH_W:2 * BRANCH_W], tab)
    v_new = z[:, 2 * BRANCH_W:3 * BRANCH_W]
    row_ref[0] = jnp.concatenate([k_new, v_new], axis=1)

    for p in range(n_pages):
        copy(p).wait()

    n_past = nb - 1
    means = jnp.sum(buf[:, :BRANCH_W].reshape(n_past, MOBA_BLOCK, BRANCH_W), axis=1) * (1.0 / MOBA_BLOCK)
    outs = []
    for h in range(N_HEADS):
        hs = slice(h * HEAD_DIM, (h + 1) * HEAD_DIM)
        vs = slice(BRANCH_W + h * HEAD_DIM, BRANCH_W + (h + 1) * HEAD_DIM)
        q8 = jnp.broadcast_to(qr[:, hs], (8, HEAD_DIM))
        gate = _dot3(q8, means[:, hs], _dot_nt)[0:1]
        sel = _rank_lt(gate, n_past, topk)
        keys = _dot(jnp.broadcast_to(sel.astype(BF16), (8, n_past)), expand_ref[...])[0:1] > 0.5
        qb = q8.astype(BF16)
        s_past = jnp.where(keys, _dot_nt(qb, buf[:, hs].astype(BF16)), NEG)
        s_new = jnp.sum(q8 * k_new[:, hs], axis=1, keepdims=True)
        m = jnp.maximum(jnp.max(s_past, axis=1, keepdims=True), s_new)
        p_past = jnp.where(keys, jnp.exp(s_past - m), 0.0)
        p_new = jnp.exp(s_new - m)
        den = jnp.maximum(jnp.sum(p_past, axis=1, keepdims=True) + p_new, 1e-30)
        o = _dot((p_past / den).astype(BF16), buf[:, vs].astype(BF16))
        o = o + (p_new / den) * v_new[:, hs]
        outs.append(o[0:1])
    o_ref[0] = jnp.concatenate(outs, axis=1).astype(o_ref.dtype)


def moba_sample(page_table, z_moba, tab, expand, cache, layer, n_pool, page):
    b = z_moba.shape[0]
    n_pages = page_table.shape[1]
    past = n_pages * page
    assert past % MOBA_BLOCK == 0
    nb = past // MOBA_BLOCK + 1
    kern = functools.partial(_moba_sample_kernel, base=layer * n_pool, n_pages=n_pages, page=page, nb=nb,
                             topk=min(MOBA_TOPK, nb))
    row = lambda i, pt: (i, 0, 0)
    grid_spec = pltpu.PrefetchScalarGridSpec(
        num_scalar_prefetch=1, grid=(b,),
        in_specs=[pl.BlockSpec((1, 1, z_moba.shape[2]), row),
                  pl.BlockSpec(tab.shape, lambda i, pt: (0, 0, 0)),
                  pl.BlockSpec(expand.shape, lambda i, pt: (0, 0)),
                  pl.BlockSpec(memory_space=pl.ANY)],
        out_specs=[pl.BlockSpec((1, 1, BRANCH_W), row), pl.BlockSpec((1, 1, 2 * BRANCH_W), row)],
        scratch_shapes=[pltpu.VMEM((past, 2 * BRANCH_W), F32), pltpu.SemaphoreType.DMA((1,))])
    return pl.pallas_call(
        kern, grid_spec=grid_spec,
        out_shape=[jax.ShapeDtypeStruct((b, 1, BRANCH_W), BF16),
                   jax.ShapeDtypeStruct((b, 1, 2 * BRANCH_W), F32)],
        compiler_params=_params("arbitrary"), name="moba_sample",
    )(page_table.reshape(-1), z_moba, tab, expand, cache)


def _rope_table(pos):
    inv = ROPE_THETA ** (-jnp.arange(ROT_HALF, dtype=F32) / ROT_HALF)
    ang = pos.astype(F32)[:, None] * inv[None, :]
    cos, sin = jnp.cos(ang), jnp.sin(ang)
    t = pos.shape[0]
    ones = jnp.ones((t, HEAD_DIM - 2 * ROT_HALF), F32)
    zeros = jnp.zeros((t, HEAD_DIM - 2 * ROT_HALF), F32)
    z8 = jnp.zeros((t, ROT_HALF), F32)
    c = jnp.concatenate([cos, cos, ones], axis=1)
    sa = jnp.concatenate([-sin, z8, zeros], axis=1)
    sb = jnp.concatenate([z8, sin, zeros], axis=1)
    return jnp.stack([jnp.tile(c, (1, 2)), jnp.tile(sa, (1, 2)), jnp.tile(sb, (1, 2))], axis=0)


def _cover_matrix(n_rows, n_cmp, n_slc, width):
    starts = np.arange(n_rows) * CMP_STRIDE
    blk = np.arange(width) * SLC_BLOCK
    cover = (starts[:, None] < blk[None, :] + SLC_BLOCK) & (starts[:, None] + CMP_BLOCK > blk[None, :])
    cover &= (np.arange(n_rows)[:, None] < n_cmp) & (np.arange(width)[None, :] < n_slc)
    return jnp.asarray(cover, BF16)


def _expand_matrix(n_blocks, block, n_keys):
    e = np.arange(n_blocks)[:, None] == (np.arange(n_keys)[None, :] // block)
    return jnp.asarray(e, BF16)


def _round_up(n, m):
    return -(-n // m) * m


def _pad_cols(w, n):
    return jnp.pad(w, ((0, 0), (0, n - w.shape[1])))


def _layer_weights(l, w_in, nsa_pe, nsa_phi_w1, nsa_phi_w2, rwkv_mu, rwkv_w0, rwkv_w2, rwkv_a0, rwkv_a2,
                   rwkv_g2, rwkv_k_k, rwkv_k_a, rwkv_r_k, rwkv_ln_w, rwkv_ln_b, w_mem_kv, w_branch, w_out,
                   w_ffn_gate, w_ffn_up, w_ffn_down):
    wi = w_in[l]
    w = {
        'nsa': _pad_cols(wi[:, 0:652], NSA_PAD).astype(BF16),
        'rwkv': _pad_cols(wi[:, 652:1708], RWKV_PAD).astype(BF16),
        'moba': wi[:, 1708:2476].astype(BF16),
        'cross': wi[:, 2476:2732].astype(BF16),
        'merge': wi[:, 2732:].astype(BF16),
        'mem': w_mem_kv[l].astype(BF16),
        'branch': w_branch[l].astype(BF16),
        'out': w_out[l].astype(BF16),
        'ffn_gate': w_ffn_gate[l].astype(BF16),
        'ffn_up': w_ffn_up[l].astype(BF16),
        'ffn_down': w_ffn_down[l].astype(BF16),
    }
    w1 = nsa_phi_w1[l]
    hidden = w1.shape[2]
    w1r = w1.reshape(2, 2, CMP_STRIDE, HEAD_DIM, hidden)
    w1c = jnp.zeros((2, CMP_STRIDE, 2, HEAD_DIM, 2, hidden), F32)
    w1c = w1c.at[:, :, 0, :, 0, :].set(w1r[0]).at[:, :, 1, :, 1, :].set(w1r[1])
    w1c = w1c.reshape(2, CMP_STRIDE * 2 * HEAD_DIM, 2 * hidden)
    w['w1c'] = jnp.concatenate([w1c[0], w1c[1]], axis=1).astype(BF16)
    pe = nsa_pe[l].reshape(2, CMP_STRIDE * 2 * HEAD_DIM)
    w['pe2'] = jnp.pad(pe, ((0, 6), (0, 0)))
    w2 = nsa_phi_w2[l]
    w2c = jnp.zeros((2, hidden, 2, HEAD_DIM), F32).at[0, :, 0, :].set(w2[0]).at[1, :, 1, :].set(w2[1])
    w['w2c'] = w2c.reshape(2 * hidden, 2 * HEAD_DIM).astype(BF16)
    vec = jnp.zeros((8, RWKV_PAD), F32)
    vec = vec.at[0, :RWKV_IN].set(rwkv_mu[l])
    vec = vec.at[1, 0:256].set(rwkv_w0[l]).at[1, 256:512].set(rwkv_a0[l])
    for i, p in enumerate((rwkv_k_k, rwkv_k_a, rwkv_r_k, rwkv_ln_w, rwkv_ln_b)):
        vec = vec.at[2 + i, 0:256].set(p[l])
    w['rwkv_vec'] = vec
    wa = jnp.zeros((128, 512), F32).at[0:64, 0:256].set(rwkv_w2[l]).at[64:128, 256:512].set(rwkv_a2[l])
    w['rwkv_wa'] = wa.astype(BF16)
    w['rwkv_g2'] = jnp.pad(rwkv_g2[l], ((0, 256 - rwkv_g2.shape[1]), (0, 0))).astype(BF16)
    return w


def _mixers_prompt(h, w, tab, mem_kv, consts, b, t):
    z_nsa = mm(h, w['nsa']).reshape(b, t, NSA_PAD)
    z_rwkv = mm(h, w['rwkv']).reshape(b, t, RWKV_PAD)
    z_moba = mm(h, w['moba']).reshape(b, t, 3 * BRANCH_W)
    z_cross = mm(h, w['cross']).reshape(b, t, BRANCH_W)
    gates = mm(h, w['merge'])

    new_cmp = z_nsa[:, :, 256:384]
    new_slc, new_win = nsa_rope(z_nsa, tab)
    cmp_kv = compress(new_cmp.reshape(b, t // CMP_STRIDE, CMP_STRIDE * LANES), w['w1c'], w['pe2'], w['w2c'])
    o_nsa = nsa_attn(z_nsa, tab, cmp_kv, new_slc, new_win, consts['cover_p'], consts['expand_p'],
                     consts['n_cmp_p'], consts['n_slc_p'])

    zero_shift = jnp.zeros((b, 1, RWKV_PAD), F32)
    zero_state = jnp.zeros((b, N_HEADS, HEAD_DIM, HEAD_DIM), F32)
    o_rwkv, new_state = rwkv_seq(z_rwkv, zero_shift, zero_state, w['rwkv_vec'], w['rwkv_wa'], w['rwkv_g2'],
                                 consts['bd'], consts['tri'])
    new_shift = z_rwkv[:, t - 1, :RWKV_IN]

    new_moba, means = moba_rope(z_moba, tab)
    o_moba = moba_attn(z_moba, tab, new_moba, means.reshape(b, -1, BRANCH_W))
    o_cross = cross_attn(z_cross, mem_kv)

    outs = [o.reshape(b * t, BRANCH_W) for o in (o_nsa, o_rwkv, o_moba, o_cross)]
    win_keep = min(WINDOW, t)
    caches = (new_cmp.reshape(b, t, 2, HEAD_DIM), new_slc.reshape(b, t, 2, HEAD_DIM),
              new_win[:, t - win_keep:].reshape(b, win_keep, 2, HEAD_DIM),
              new_moba.reshape(b, t, 2, N_HEADS, HEAD_DIM), new_state, new_shift)
    return outs, gates, caches


def _mixers_sample(h, w, tab, l, page_table, caches_in, consts, b):
    cache_cmp, cache_slc, cache_win, cache_moba, cache_mem, state_rwkv, state_shift = caches_in
    n_pool, page = cache_cmp.shape[1], cache_cmp.shape[2]
    z_nsa = mm(h, w['nsa']).reshape(b, 1, NSA_PAD)
    z_rwkv = mm(h, w['rwkv']).reshape(b, 1, RWKV_PAD)
    z_moba = mm(h, w['moba']).reshape(b, 1, 3 * BRANCH_W)
    z_cross = mm(h, w['cross']).reshape(b, 1, BRANCH_W)
    gates = mm(h, w['merge'])

    win = cache_win[l].reshape(b, -1, LANES)
    o_nsa, slc_row, win_row = nsa_sample(
        page_table, z_nsa, tab, win, w['w1c'], w['pe2'], w['w2c'], consts['cover_s'], consts['expand_s'],
        cache_cmp.reshape(-1, page // CMP_STRIDE, CMP_STRIDE * LANES), cache_slc.reshape(-1, page, LANES),
        l, n_pool, page, consts['n_cmp_s'], consts['n_slc_s'])
    new_cmp = z_nsa[:, :, 256:384]
    full_win = jnp.concatenate([win, win_row], axis=1)
    keep = min(WINDOW, full_win.shape[1])
    new_buf = full_win[:, full_win.shape[1] - keep:]

    shift = jnp.pad(state_shift[l], ((0, 0), (0, RWKV_PAD - RWKV_IN))).reshape(b, 1, RWKV_PAD)
    o_rwkv, new_state = rwkv_step(z_rwkv, shift, state_rwkv[l], w['rwkv_vec'], w['rwkv_wa'], w['rwkv_g2'],
                                  consts['bd'])
    new_shift = z_rwkv[:, 0, :RWKV_IN]

    o_moba, moba_row = moba_sample(page_table, z_moba, tab, consts['expand_m'],
                                   cache_moba.reshape(-1, page, 2 * BRANCH_W), l, n_pool, page)
    o_cross = cross_attn(z_cross, cache_mem[l].reshape(b, -1, 2 * BRANCH_W))

    outs = [o.reshape(b, BRANCH_W) for o in (o_nsa, o_rwkv, o_moba, o_cross)]
    caches = (new_cmp.reshape(b, 1, 2, HEAD_DIM), slc_row.reshape(b, 1, 2, HEAD_DIM),
              new_buf.reshape(b, keep, 2, HEAD_DIM), moba_row.reshape(b, 1, 2, N_HEADS, HEAD_DIM),
              new_state, new_shift)
    return outs, gates, caches


def kernel(x_prompt, x_sample, cache_nsa_cmp, cache_nsa_slc, cache_nsa_win, cache_moba, cache_mem, state_rwkv, state_shift, page_table, mem_prompt, g_pre_mix, g_post_mix, g_pre_ffn, g_post_ffn, g_mem, w_in, nsa_pe, nsa_phi_w1, nsa_phi_w2, rwkv_mu, rwkv_w0, rwkv_w2, rwkv_a0, rwkv_a2, rwkv_g2, rwkv_k_k, rwkv_k_a, rwkv_r_k, rwkv_ln_w, rwkv_ln_b, w_mem_kv, w_branch, w_out, w_ffn_gate, w_ffn_up, w_ffn_down):
    bp, t, d = x_prompt.shape
    bs = x_sample.shape[0]
    assert x_sample.shape[1] == 1
    depth = w_in.shape[0]
    page = cache_nsa_cmp.shape[2]
    past = page_table.shape[1] * page
    n_mem = mem_prompt.shape[1]
    assert t % MOBA_BLOCK == 0 and past % MOBA_BLOCK == 0 and page % SLC_BLOCK == 0

    n_cmp_p = (t - CMP_BLOCK) // CMP_STRIDE + 1
    n_slc_p = -(-t // SLC_BLOCK)
    n_cmp_s = (past + 1 - CMP_BLOCK) // CMP_STRIDE + 1
    n_slc_s = -(-(past + 1) // SLC_BLOCK)
    consts = {
        'n_cmp_p': n_cmp_p, 'n_slc_p': n_slc_p, 'n_cmp_s': n_cmp_s, 'n_slc_s': n_slc_s,
        'cover_p': _cover_matrix(t // CMP_STRIDE, n_cmp_p, n_slc_p, _round_up(n_slc_p, LANES)),
        'expand_p': _expand_matrix(_round_up(n_slc_p, LANES), SLC_BLOCK, t),
        'cover_s': _cover_matrix(past // CMP_STRIDE, n_cmp_s, n_slc_s, _round_up(n_slc_s, LANES)),
        'expand_s': _expand_matrix(_round_up(n_slc_s, LANES), SLC_BLOCK, past),
        'expand_m': _expand_matrix(past // MOBA_BLOCK, MOBA_BLOCK, past),
        'bd': jnp.asarray(np.arange(BRANCH_W)[:, None] // HEAD_DIM == np.arange(BRANCH_W)[None, :] // HEAD_DIM, BF16),
        'tri': jnp.asarray(np.tril(np.ones((RWKV_CHUNK, RWKV_CHUNK), np.float32))),
    }
    tab_p = _rope_table(jnp.arange(t, dtype=jnp.int32))
    tab_s = _rope_table(jnp.full((1,), past, jnp.int32))

    xp = x_prompt.reshape(bp * t, d)
    xs = x_sample.reshape(bs, d)
    mem_flat = mem_prompt.reshape(bp * n_mem, d)
    sample_caches = (cache_nsa_cmp, cache_nsa_slc, cache_nsa_win, cache_moba, cache_mem, state_rwkv, state_shift)
    outs_p, outs_s, mem_out = [], [], []
    for l in range(depth):
        w = _layer_weights(l, w_in, nsa_pe, nsa_phi_w1, nsa_phi_w2, rwkv_mu, rwkv_w0, rwkv_w2, rwkv_a0,
                           rwkv_a2, rwkv_g2, rwkv_k_k, rwkv_k_a, rwkv_r_k, rwkv_ln_w, rwkv_ln_b, w_mem_kv,
                           w_branch, w_out, w_ffn_gate, w_ffn_up, w_ffn_down)
        mem_kv = mm(rms_cast(mem_flat, g_mem[l]), w['mem']).reshape(bp, n_mem, 2 * BRANCH_W)
        mem_out.append(mem_kv.reshape(bp, n_mem, 2, N_HEADS, HEAD_DIM))

        o_p, gates_p, new_p = _mixers_prompt(rms_cast(xp, g_pre_mix[l]), w, tab_p, mem_kv, consts, bp, t)
        xp = merge(xp, o_p, gates_p, w['branch'], w['out'], g_post_mix[l])
        xp = ffn(xp, g_pre_ffn[l], w['ffn_gate'], w['ffn_up'], w['ffn_down'], g_post_ffn[l])

        o_s, gates_s, new_s = _mixers_sample(rms_cast(xs, g_pre_mix[l]), w, tab_s, l, page_table, sample_caches,
                                             consts, bs)
        xs = merge(xs, o_s, gates_s, w['branch'], w['out'], g_post_mix[l])
        xs = ffn(xs, g_pre_ffn[l], w['ffn_gate'], w['ffn_up'], w['ffn_down'], g_post_ffn[l])
        outs_p.append(new_p)
        outs_s.append(new_s)

    def stack(outs, i):
        return jnp.stack([o[i] for o in outs], axis=0)

    return (xp.reshape(bp, t, d), xs.reshape(bs, 1, d),
            stack(outs_p, 0), stack(outs_s, 0), stack(outs_p, 1), stack(outs_s, 1),
            stack(outs_p, 2), stack(outs_s, 2), stack(outs_p, 3), stack(outs_s, 3),
            jnp.stack(mem_out, axis=0),
            stack(outs_p, 4), stack(outs_s, 4), stack(outs_p, 5), stack(outs_s, 5))
```

```python
import functools

import jax
import jax.numpy as jnp
import numpy as np
from jax import lax
from jax.experimental import pallas as pl
from jax.experimental.pallas import tpu as pltpu

F32 = jnp.float32
BF16 = jnp.bfloat16

HEAD_DIM = 64
N_HEADS = 4
BRANCH_W = N_HEADS * HEAD_DIM
ROT_HALF = 8
ROPE_THETA = 500000.0
CMP_BLOCK = 32
CMP_STRIDE = 16
SLC_BLOCK = 64
N_SELECT = 16
WINDOW = 512
MOBA_BLOCK = 256
MOBA_TOPK = 3
RWKV_IN = 1056
RWKV_PAD = 1152
NSA_PAD = 768
RWKV_GN_EPS = 64e-5
RWKV_CHUNK = 64
NEG = -1e30
SCALE = HEAD_DIM ** -0.5
LANES = 128
VMEM_LIMIT = 48 * 1024 * 1024


def _iota(shape, dim):
    return lax.broadcasted_iota(jnp.int32, shape, dim)


def _dot(a, b):
    return lax.dot_general(a, b, (((1,), (0,)), ((), ())), preferred_element_type=F32)


def _dot_nt(a, b):
    return lax.dot_general(a, b, (((1,), (1,)), ((), ())), preferred_element_type=F32)


def _dot_tn(a, b):
    return lax.dot_general(a, b, (((0,), (0,)), ((), ())), preferred_element_type=F32)


def _split(x):
    hi = x.astype(BF16)
    lo = (x - hi.astype(F32)).astype(BF16)
    return hi, lo


def _dot3(a, b, dot=_dot):
    ah, al = _split(a)
    bh, bl = _split(b)
    return dot(ah, bh) + dot(al, bh) + dot(ah, bl)


def _dot2_exact_rhs(a, b_bf16):
    ah, al = _split(a)
    return _dot(ah, b_bf16) + _dot(al, b_bf16)


def _sigmoid(x):
    return 1.0 / (1.0 + jnp.exp(-x))


def _rms(x, g, eps=1e-6):
    return x * lax.rsqrt(jnp.mean(x * x, axis=-1, keepdims=True) + eps) * g


def _msoftmax(s, mask):
    s = jnp.where(mask, s, NEG)
    p = jnp.where(mask, jnp.exp(s - jnp.max(s, axis=-1, keepdims=True)), 0.0)
    return p / jnp.maximum(jnp.sum(p, axis=-1, keepdims=True), 1e-30)


def _flash_step(q, k, v, mask, m, l, acc):
    s = jnp.where(mask, _dot_nt(q, k), NEG)
    m_new = jnp.maximum(m, jnp.max(s, axis=-1, keepdims=True))
    p = jnp.where(mask, jnp.exp(s - m_new), 0.0)
    alpha = jnp.exp(m - m_new)
    l = alpha * l + jnp.sum(p, axis=-1, keepdims=True)
    acc = alpha * acc + _dot(p.astype(BF16), v)
    return m_new, l, acc


def _rope(x, tab, half_only=False):
    c, sa, sb = tab[0], tab[1], tab[2]
    if half_only:
        first = _iota(x.shape, 1) < HEAD_DIM
        c = jnp.where(first, c, 1.0)
        sa = jnp.where(first, sa, 0.0)
        sb = jnp.where(first, sb, 0.0)
    return x * c + pltpu.roll(x, LANES - ROT_HALF, 1) * sa + pltpu.roll(x, ROT_HALF, 1) * sb


def _rope_heads(x, tab):
    return jnp.concatenate([_rope(x[:, :LANES], tab), _rope(x[:, LANES:], tab)], axis=1)


def _rank_lt(score, n_real, k):
    lane = _iota(score.shape, 1)
    cnt = jnp.zeros(score.shape, jnp.int32)
    for i in range(n_real):
        ci = score[:, i:i + 1]
        beats = (ci > score) | ((ci == score) & (lane > i))
        cnt = cnt + beats.astype(jnp.int32)
    return cnt < k


def _pick_tile(n, target, mult=8):
    if n <= target:
        return n
    for t in range(target, 0, -1):
        if n % t == 0 and t % mult == 0:
            return t
    return n


def _params(*sem):
    return pltpu.CompilerParams(dimension_semantics=sem, vmem_limit_bytes=VMEM_LIMIT)


def _rms_cast_kernel(x_ref, g_ref, o_ref):
    o_ref[...] = _rms(x_ref[...], g_ref[...]).astype(BF16)


def rms_cast(x, g):
    m, d = x.shape
    tm = _pick_tile(m, 1024, 16)
    return pl.pallas_call(
        _rms_cast_kernel, grid=(m // tm,),
        in_specs=[pl.BlockSpec((tm, d), lambda i: (i, 0)), pl.BlockSpec((1, d), lambda i: (0, 0))],
        out_specs=pl.BlockSpec((tm, d), lambda i: (i, 0)),
        out_shape=jax.ShapeDtypeStruct((m, d), BF16),
        compiler_params=_params("parallel"), name="rms_cast",
    )(x, g.reshape(1, d))


def _mm_kernel(a_ref, b_ref, o_ref, *, gate):
    y = _dot(a_ref[...], b_ref[...])
    o_ref[...] = (_sigmoid(y) if gate else y).astype(o_ref.dtype)


def mm(a, w, layer, col0, n, out_dtype=F32, gate=False):
    m, k = a.shape
    tm = _pick_tile(m, 1024, 16)
    tn = n if n <= 1536 else _pick_tile(n, 1024, LANES)
    assert col0 % tn == 0 and n % tn == 0
    c0 = col0 // tn
    return pl.pallas_call(
        functools.partial(_mm_kernel, gate=gate), grid=(m // tm, n // tn),
        in_specs=[pl.BlockSpec((tm, k), lambda i, j: (i, 0)),
                  pl.BlockSpec((None, k, tn), lambda i, j: (layer, 0, c0 + j))],
        out_specs=pl.BlockSpec((tm, tn), lambda i, j: (i, j)),
        out_shape=jax.ShapeDtypeStruct((m, n), out_dtype),
        compiler_params=_params("parallel", "parallel"), name="mm",
    )(a, w)


def _ffn_kernel(x_ref, g1_ref, wg_ref, wu_ref, wd_ref, g2_ref, o_ref, h_ref, acc_ref):
    f = pl.program_id(1)

    @pl.when(f == 0)
    def _():
        h_ref[...] = _rms(x_ref[...], g1_ref[...]).astype(BF16)
        acc_ref[...] = jnp.zeros_like(acc_ref)

    h = h_ref[...]
    gate = _dot(h, wg_ref[...])
    up = _dot(h, wu_ref[...])
    act = (gate * _sigmoid(gate) * up).astype(BF16)
    acc_ref[...] += _dot(act, wd_ref[...])

    @pl.when(f == pl.num_programs(1) - 1)
    def _():
        o_ref[...] = x_ref[...] + _rms(acc_ref[...], g2_ref[...])


def ffn(x, g_pre, wg, wu, wd, g_post, layer):
    m, d = x.shape
    dff = wg.shape[2]
    tm = _pick_tile(m, 512, 16)
    tf = _pick_tile(dff, 256, LANES)
    return pl.pallas_call(
        _ffn_kernel, grid=(m // tm, dff // tf),
        in_specs=[pl.BlockSpec((tm, d), lambda i, f: (i, 0)),
                  pl.BlockSpec((1, d), lambda i, f: (0, 0)),
                  pl.BlockSpec((None, d, tf), lambda i, f: (layer, 0, f)),
                  pl.BlockSpec((None, d, tf), lambda i, f: (layer, 0, f)),
                  pl.BlockSpec((None, tf, d), lambda i, f: (layer, f, 0)),
                  pl.BlockSpec((1, d), lambda i, f: (0, 0))],
        out_specs=pl.BlockSpec((tm, d), lambda i, f: (i, 0)),
        out_shape=jax.ShapeDtypeStruct((m, d), F32),
        scratch_shapes=[pltpu.VMEM((tm, d), BF16), pltpu.VMEM((tm, d), F32)],
        compiler_params=_params("parallel", "arbitrary"), name="ffn",
    )(x, g_pre.reshape(1, d), wg, wu, wd, g_post.reshape(1, d))


def _merge_kernel(x_ref, o0_ref, o1_ref, o2_ref, o3_ref, gate_ref, wb_ref, wo_ref, g_ref, out_ref):
    d = x_ref.shape[1]
    merged = None
    for b, o_ref in enumerate((o0_ref, o1_ref, o2_ref, o3_ref)):
        term = gate_ref[:, b * d:(b + 1) * d].astype(F32) * _dot(o_ref[...], wb_ref[b])
        merged = term if merged is None else merged + term
    y = _dot(merged.astype(BF16), wo_ref[...])
    out_ref[...] = x_ref[...] + _rms(y, g_ref[...])


def merge(x, outs, gates, w_branch, w_out, g_post, layer):
    m, d = x.shape
    bw = outs[0].shape[1]
    tm = _pick_tile(m, 512, 16)
    row = lambda i: (i, 0)
    return pl.pallas_call(
        _merge_kernel, grid=(m // tm,),
        in_specs=[pl.BlockSpec((tm, d), row)] + [pl.BlockSpec((tm, bw), row)] * 4 + [
            pl.BlockSpec((tm, 4 * d), row),
            pl.BlockSpec((None, 4, bw, d), lambda i: (layer, 0, 0, 0)),
            pl.BlockSpec((None, d, d), lambda i: (layer, 0, 0)),
            pl.BlockSpec((1, d), lambda i: (0, 0))],
        out_specs=pl.BlockSpec((tm, d), row),
        out_shape=jax.ShapeDtypeStruct((m, d), F32),
        compiler_params=_params("parallel"), name="merge",
    )(x, *outs, gates, w_branch, w_out, g_post.reshape(1, d))


def _nsa_rope_kernel(z_ref, tab_ref, slc_ref, win_ref):
    z = z_ref[0]
    tab = tab_ref[...]
    slc_ref[0] = _rope(z[:, 384:512], tab, half_only=True)
    win_ref[0] = _rope(z[:, 512:640], tab, half_only=True)


def nsa_rope(z_nsa, tab):
    b, t, w = z_nsa.shape
    tt = _pick_tile(t, 512)
    out = jax.ShapeDtypeStruct((b, t, LANES), F32)
    return pl.pallas_call(
        _nsa_rope_kernel, grid=(b, t // tt),
        in_specs=[pl.BlockSpec((1, tt, w), lambda i, j: (i, j, 0)),
                  pl.BlockSpec((3, tt, LANES), lambda i, j: (0, j, 0))],
        out_specs=[pl.BlockSpec((1, tt, LANES), lambda i, j: (i, j, 0))] * 2,
        out_shape=[out, out],
        compiler_params=_params("parallel", "parallel"), name="nsa_rope",
    )(z_nsa, tab)


def _moba_rope_kernel(z_ref, tab_ref, kv_ref, mean_ref):
    z = z_ref[0]
    kr = _rope_heads(z[:, BRANCH_W:2 * BRANCH_W], tab_ref[...])
    kv_ref[0] = jnp.concatenate([kr, z[:, 2 * BRANCH_W:3 * BRANCH_W]], axis=1)
    mean_ref[0, 0] = jnp.sum(kr, axis=0, keepdims=True) * (1.0 / MOBA_BLOCK)


def moba_rope(z_moba, tab):
    b, t, w = z_moba.shape
    nb = t // MOBA_BLOCK
    return pl.pallas_call(
        _moba_rope_kernel, grid=(b, nb),
        in_specs=[pl.BlockSpec((1, MOBA_BLOCK, w), lambda i, j: (i, j, 0)),
                  pl.BlockSpec((3, MOBA_BLOCK, LANES), lambda i, j: (0, j, 0))],
        out_specs=[pl.BlockSpec((1, MOBA_BLOCK, 2 * BRANCH_W), lambda i, j: (i, j, 0)),
                   pl.BlockSpec((1, 1, 1, BRANCH_W), lambda i, j: (i, j, 0, 0))],
        out_shape=[jax.ShapeDtypeStruct((b, t, 2 * BRANCH_W), F32),
                   jax.ShapeDtypeStruct((b, nb, 1, BRANCH_W), F32)],
        compiler_params=_params("parallel", "parallel"), name="moba_rope",
    )(z_moba, tab)


def _compress(g, w1_ref, pe_ref, w2_ref):
    r = g.shape[0]
    ab = _dot(g.astype(BF16), w1_ref[...])
    pe = _dot(pe_ref[...].astype(BF16), w1_ref[...])
    half = ab.shape[1] // 2
    nxt = pltpu.roll(ab[:, half:], r - 1, 0)
    pre = ab[:, :half] + nxt + pe[0:1, :half] + pe[1:2, half:]
    hid = pre * _sigmoid(pre)
    return _dot(hid.astype(BF16), w2_ref[...])


def _compress_kernel(g_ref, w1_ref, pe_ref, w2_ref, o_ref):
    o_ref[0] = _compress(g_ref[0], w1_ref, pe_ref, w2_ref)


def compress(g, w1c, pe2, w2c):
    b, r, w = g.shape
    const = lambda i: (0, 0)
    return pl.pallas_call(
        _compress_kernel, grid=(b,),
        in_specs=[pl.BlockSpec((1, r, w), lambda i: (i, 0, 0)),
                  pl.BlockSpec(w1c.shape, const), pl.BlockSpec(pe2.shape, const),
                  pl.BlockSpec(w2c.shape, const)],
        out_specs=pl.BlockSpec((1, r, LANES), lambda i: (i, 0, 0)),
        out_shape=jax.ShapeDtypeStruct((b, r, LANES), F32),
        compiler_params=_params("parallel"), name="nsa_compress",
    )(g, w1c, pe2, w2c)


def _cmp_attention(q, ck, cv, cmask, rows_real=None):
    outs, psum = [], None
    for qh in q:
        p = _msoftmax(_dot_nt(qh.astype(BF16), ck), cmask)
        outs.append(_dot(p.astype(BF16), cv))
        psum = p if psum is None else psum + p
    return outs, psum


def _select_blocks(imp, t_pos, n_slc, k_sel):
    j = _iota(imp.shape, 1)
    own = lax.shift_right_logical(t_pos, 6)
    causal = (j * SLC_BLOCK <= t_pos) & (j < n_slc)
    forced = (j == 0) | (j == own) | (j == own - 1)
    score = jnp.where(forced, -NEG, jnp.where(causal, imp, NEG))
    return _rank_lt(score, n_slc, k_sel) & causal


def _nsa_attn_kernel(z_ref, tab_ref, cmp_ref, slc_ref, win_ref, cover_ref, expand_ref, o_ref,
                     *, tq, n_cmp, n_slc, k_sel):
    qi = pl.program_id(1)
    z = z_ref[0]
    q = z[:, :BRANCH_W] * SCALE
    qr = _rope_heads(q, tab_ref[...])
    t_pos = qi * tq + _iota((tq, 1), 0)
    heads = [slice(h * HEAD_DIM, (h + 1) * HEAD_DIM) for h in range(N_HEADS)]

    cmpkv = cmp_ref[0]
    ck = cmpkv[:, :HEAD_DIM].astype(BF16)
    cv = cmpkv[:, HEAD_DIM:].astype(BF16)
    n = _iota((1, cmpkv.shape[0]), 1)
    cmask = (n * CMP_STRIDE + (CMP_BLOCK - 1) <= t_pos) & (n < n_cmp)
    o_cmp, psum = _cmp_attention([q[:, hs] for hs in heads], ck, cv, cmask)
    imp = _dot2_exact_rhs(psum, cover_ref[...])
    sel = _select_blocks(imp, t_pos, n_slc, k_sel).astype(BF16)

    qrb = [qr[:, hs].astype(BF16) for hs in heads]
    init = tuple((jnp.full((tq, 1), NEG, F32), jnp.zeros((tq, 1), F32), jnp.zeros((tq, HEAD_DIM), F32))
                 for _ in heads)

    def slc_body(kt, carry):
        rows = slc_ref[0, pl.ds(kt * tq, tq), :]
        k = rows[:, :HEAD_DIM].astype(BF16)
        v = rows[:, HEAD_DIM:].astype(BF16)
        kpos = kt * tq + _iota((1, tq), 1)
        mask = (_dot(sel, expand_ref[:, pl.ds(pl.multiple_of(kt * tq, tq), tq)]) > 0.5) & (kpos <= t_pos)
        return tuple(_flash_step(qrb[h], k, v, mask, *carry[h]) for h in range(N_HEADS))

    def win_body(kt, carry):
        rows = win_ref[0, pl.ds(kt * tq, tq), :]
        k = rows[:, :HEAD_DIM].astype(BF16)
        v = rows[:, HEAD_DIM:].astype(BF16)
        d = t_pos - (kt * tq + _iota((1, tq), 1))
        mask = (d >= 0) & (d <= WINDOW)
        return tuple(_flash_step(qrb[h], k, v, mask, *carry[h]) for h in range(N_HEADS))

    slc = lax.fori_loop(0, qi + 1, slc_body, init)
    n_win = -(-WINDOW // tq)
    win = lax.fori_loop(jnp.maximum(qi - n_win, 0), qi + 1, win_body, init)

    g = _sigmoid(z[:, 640:640 + 3 * N_HEADS])
    outs = []
    for h in range(N_HEADS):
        o_slc = slc[h][2] / jnp.maximum(slc[h][1], 1e-30)
        o_win = win[h][2] / jnp.maximum(win[h][1], 1e-30)
        outs.append(g[:, 3 * h:3 * h + 1] * o_cmp[h] + g[:, 3 * h + 1:3 * h + 2] * o_slc
                    + g[:, 3 * h + 2:3 * h + 3] * o_win)
    o_ref[0] = jnp.concatenate(outs, axis=1).astype(o_ref.dtype)


def nsa_attn(z_nsa, tab, cmp_kv, new_slc, new_win, cover, expand, n_cmp, n_slc):
    b, t, w = z_nsa.shape
    tq = _pick_tile(t, 256)
    kern = functools.partial(_nsa_attn_kernel, tq=tq, n_cmp=n_cmp, n_slc=n_slc, k_sel=min(N_SELECT, n_slc))
    full = lambda i, j: (i, 0, 0)
    const = lambda i, j: (0, 0)
    return pl.pallas_call(
        kern, grid=(b, t // tq),
        in_specs=[pl.BlockSpec((1, tq, w), lambda i, j: (i, j, 0)),
                  pl.BlockSpec((3, tq, LANES), lambda i, j: (0, j, 0)),
                  pl.BlockSpec((1,) + cmp_kv.shape[1:], full),
                  pl.BlockSpec((1, t, LANES), full), pl.BlockSpec((1, t, LANES), full),
                  pl.BlockSpec(cover.shape, const), pl.BlockSpec(expand.shape, const)],
        out_specs=pl.BlockSpec((1, tq, BRANCH_W), lambda i, j: (i, j, 0)),
        out_shape=jax.ShapeDtypeStruct((b, t, BRANCH_W), BF16),
        compiler_params=_params("parallel", "parallel"), name="nsa_attn",
    )(z_nsa, tab, cmp_kv, new_slc, new_win, cover, expand)


def _moba_attn_kernel(z_ref, tab_ref, kv_ref, mean_ref, o_ref, *, tq, nb, topk):
    qi = pl.program_id(1)
    q = z_ref[0][:, :BRANCH_W] * SCALE
    qr = _rope_heads(q, tab_ref[...])
    t_pos = qi * tq + _iota((tq, 1), 0)
    own = lax.shift_right_logical(t_pos, 8)
    means = mean_ref[0]
    jb = _iota((1, means.shape[0]), 1)
    past = (jb < own) & (jb < nb)
    heads = [slice(h * HEAD_DIM, (h + 1) * HEAD_DIM) for h in range(N_HEADS)]

    sel = []
    for hs in heads:
        gate = _dot3(qr[:, hs], means[:, hs], _dot_nt)
        score = jnp.where(past, gate, NEG)
        sel.append((_rank_lt(score, nb, topk) & past).astype(F32))

    qrb = [qr[:, hs].astype(BF16) for hs in heads]
    init = tuple((jnp.full((tq, 1), NEG, F32), jnp.zeros((tq, 1), F32), jnp.zeros((tq, HEAD_DIM), F32))
                 for _ in heads)

    def body(kt, carry):
        rows = kv_ref[0, pl.ds(kt * tq, tq), :]
        kpos = kt * tq + _iota((1, tq), 1)
        out = []
        for h, hs in enumerate(heads):
            k = rows[:, hs].astype(BF16)
            v = rows[:, BRANCH_W + h * HEAD_DIM:BRANCH_W + (h + 1) * HEAD_DIM].astype(BF16)
            picked = jnp.sum(jnp.where(jb == kt, sel[h], 0.0), axis=1, keepdims=True) > 0.5
            is_own = own == kt
            mask = (is_own & (kpos <= t_pos)) | (jnp.logical_not(is_own) & picked)
            out.append(_flash_step(qrb[h], k, v, mask, *carry[h]))
        return tuple(out)

    res = lax.fori_loop(0, qi + 1, body, init)
    outs = [res[h][2] / jnp.maximum(res[h][1], 1e-30) for h in range(N_HEADS)]
    o_ref[0] = jnp.concatenate(outs, axis=1).astype(o_ref.dtype)


def moba_attn(z_moba, tab, new_moba, means):
    b, t, w = z_moba.shape
    assert t % MOBA_BLOCK == 0
    tq = MOBA_BLOCK
    nb = t // MOBA_BLOCK
    kern = functools.partial(_moba_attn_kernel, tq=tq, nb=nb, topk=min(MOBA_TOPK, nb))
    full = lambda i, j: (i, 0, 0)
    return pl.pallas_call(
        kern, grid=(b, t // tq),
        in_specs=[pl.BlockSpec((1, tq, w), lambda i, j: (i, j, 0)),
                  pl.BlockSpec((3, tq, LANES), lambda i, j: (0, j, 0)),
                  pl.BlockSpec((1, t, 2 * BRANCH_W), full),
                  pl.BlockSpec((1, nb, BRANCH_W), full)],
        out_specs=pl.BlockSpec((1, tq, BRANCH_W), lambda i, j: (i, j, 0)),
        out_shape=jax.ShapeDtypeStruct((b, t, BRANCH_W), BF16),
        compiler_params=_params("parallel", "parallel"), name="moba_attn",
    )(z_moba, tab, new_moba, means)


def _cross_attn_kernel(q_ref, mem_ref, o_ref):
    q = q_ref[0] * SCALE
    mem = mem_ref[0]
    outs = []
    for h in range(N_HEADS):
        hs = slice(h * HEAD_DIM, (h + 1) * HEAD_DIM)
        s = _dot_nt(q[:, hs].astype(BF16), mem[:, hs].astype(BF16))
        p = jnp.exp(s - jnp.max(s, axis=-1, keepdims=True))
        p = p / jnp.sum(p, axis=-1, keepdims=True)
        v = mem[:, BRANCH_W + h * HEAD_DIM:BRANCH_W + (h + 1) * HEAD_DIM].astype(BF16)
        outs.append(_dot(p.astype(BF16), v))
    o_ref[0] = jnp.concatenate(outs, axis=1).astype(o_ref.dtype)


def cross_attn(q, mem_kv):
    b, t, w = q.shape
    n_mem = mem_kv.shape[1]
    tq = _pick_tile(t, 512)
    return pl.pallas_call(
        _cross_attn_kernel, grid=(b, t // tq),
        in_specs=[pl.BlockSpec((1, tq, w), lambda i, j: (i, j, 0)),
                  pl.BlockSpec((1, n_mem, 2 * BRANCH_W), lambda i, j: (i, 0, 0))],
        out_specs=pl.BlockSpec((1, tq, BRANCH_W), lambda i, j: (i, j, 0)),
        out_shape=jax.ShapeDtypeStruct((b, t, BRANCH_W), BF16),
        compiler_params=_params("parallel", "parallel"), name="cross_attn",
    )(q, mem_kv)


def _head_sum(x, bd_ref):
    return _dot2_exact_rhs(x, bd_ref[...])


def _rwkv_prep(z, zprev, vec_ref, wa_ref, g2_ref, bd_ref):
    mu = vec_ref[0:1, :]
    zm = z + (zprev - z) * mu
    r = zm[:, 0:256]
    k = zm[:, 256:512]
    v = zm[:, 512:768]
    lora = zm[:, 768:896]
    lora = jnp.where(_iota(lora.shape, 1) < 64, jnp.tanh(lora), lora)
    wa = _dot(lora.astype(BF16), wa_ref[...])
    wl = vec_ref[1:2, 0:256] + wa[:, :256]
    x = -wl
    softplus = jnp.maximum(x, 0.0) + jnp.log(1.0 + jnp.exp(-jnp.abs(x)))
    logdecay = -jnp.exp(-softplus - 0.5)
    a = _sigmoid(vec_ref[1:2, 256:512] + wa[:, 256:])
    g = _dot(_sigmoid(zm[:, 896:1152]).astype(BF16), g2_ref[...])
    kkr = k * vec_ref[2:3, 0:256]
    kk = kkr * lax.rsqrt(jnp.maximum(_head_sum(kkr * kkr, bd_ref), 1e-24))
    k2 = k * (1.0 + (a - 1.0) * vec_ref[3:4, 0:256])
    bonus = _head_sum(r * k2 * vec_ref[4:5, 0:256], bd_ref) * v
    return r, k2, v, kk, kk * a, logdecay, g, bonus


def _rwkv_finish(y, bonus, g, vec_ref, bd_ref):
    mu = _head_sum(y, bd_ref) * (1.0 / HEAD_DIM)
    d = y - mu
    var = _head_sum(d * d, bd_ref) * (1.0 / HEAD_DIM)
    yn = d * lax.rsqrt(var + RWKV_GN_EPS) * vec_ref[5:6, 0:256] + vec_ref[6:7, 0:256]
    return (yn + bonus) * g


def _dotc(a, b, dot=_dot):
    return dot(a.astype(BF16), b.astype(BF16))


def _tri_inv(low):
    c = low.shape[0]
    eye = (_iota((c, c), 0) == _iota((c, c), 1)).astype(F32)
    t = eye - low
    lp = _dotc(low, low)
    n = 2
    while n < c:
        t = t + _dotc(t, lp)
        n *= 2
        if n < c:
            lp = _dotc(lp, lp)
    return t


def _rwkv_seq_kernel(z_ref, shift_ref, s0_ref, vec_ref, wa_ref, g2_ref, bd_ref, tri_ref,
                     o_ref, s_ref, last_ref, r_s, k_s, v_s, kk_s, b_s, ld_s, y_s, *, tt):
    ti = pl.program_id(1)

    @pl.when(ti == 0)
    def _():
        last_ref[...] = shift_ref[0]
        s_ref[0] = s0_ref[0]

    z = z_ref[0]
    rolled = pltpu.roll(z, 1, 0)
    zprev = jnp.where(_iota(z.shape, 0) == 0, last_ref[...], rolled)
    last_ref[...] = z[tt - 1:tt, :]
    r, k2, v, kk, b, ld, g, bonus = _rwkv_prep(z, zprev, vec_ref, wa_ref, g2_ref, bd_ref)
    r_s[...] = r
    k_s[...] = k2
    v_s[...] = v
    kk_s[...] = kk
    b_s[...] = b
    ld_s[...] = ld

    c = RWKV_CHUNK
    ri = _iota((c, c), 0)
    ci = _iota((c, c), 1)
    strict = ri > ci
    incl = ri >= ci

    def chunk(ck, carry):
        rows = pl.ds(pl.multiple_of(ck * c, c), c)
        ldc = ld_s[rows, :]
        cum = _dot3(tri_ref[...], ldc)
        p_in = jnp.exp(cum)
        p_ex = jnp.exp(cum - ldc)
        inv_p = jnp.exp(-cum)
        kkt = kk_s[rows, :] * p_ex
        bt = b_s[rows, :] * inv_p
        kt = k_s[rows, :] * inv_p
        rt = r_s[rows, :] * p_in
        vv = v_s[rows, :]
        p_end = p_in[c - 1:c, :]
        ys = []
        for h in range(N_HEADS):
            hs = slice(h * HEAD_DIM, (h + 1) * HEAD_DIM)
            s_h = s_ref[0, h]
            lhs = jnp.concatenate([kkt[:, hs], rt[:, hs]], axis=0)
            blocks = _dotc(lhs, jnp.concatenate([bt[:, hs], kt[:, hs]], axis=0), _dot_nt)
            low = jnp.where(strict, blocks[:c, :c], 0.0)
            gm = jnp.where(strict, blocks[:c, c:], 0.0)
            a_rb = jnp.where(incl, blocks[c:, :c], 0.0)
            a_rk = jnp.where(incl, blocks[c:, c:], 0.0)
            proj = _dotc(lhs, s_h, _dot_nt)
            u = _dotc(_tri_inv(low), proj[:c] + _dotc(gm, vv[:, hs]))
            uv = jnp.concatenate([u, vv[:, hs]], axis=0)
            ys.append(proj[c:] + _dotc(jnp.concatenate([-a_rb, a_rk], axis=1), uv))
            upd = _dotc(uv, jnp.concatenate([-bt[:, hs], kt[:, hs]], axis=0), _dot_tn)
            s_ref[0, h] = (s_h + upd) * p_end[:, hs]
        y_s[rows, :] = jnp.concatenate(ys, axis=1)
        return carry

    lax.fori_loop(0, tt // c, chunk, 0)
    o_ref[0] = _rwkv_finish(y_s[...], bonus, g, vec_ref, bd_ref).astype(o_ref.dtype)


def rwkv_seq(z, shift, s0, vec, wa, g2, bd, tri):
    b, t, w = z.shape
    tt = _pick_tile(t, 512, RWKV_CHUNK)
    assert tt % RWKV_CHUNK == 0
    const = lambda i, j: (0, 0)
    scr = pltpu.VMEM((tt, BRANCH_W), F32)
    return pl.pallas_call(
        functools.partial(_rwkv_seq_kernel, tt=tt), grid=(b, t // tt),
        in_specs=[pl.BlockSpec((1, tt, w), lambda i, j: (i, j, 0)),
                  pl.BlockSpec((1, 1, w), lambda i, j: (i, 0, 0)),
                  pl.BlockSpec((1, N_HEADS, HEAD_DIM, HEAD_DIM), lambda i, j: (i, 0, 0, 0)),
                  pl.BlockSpec(vec.shape, const), pl.BlockSpec(wa.shape, const),
                  pl.BlockSpec(g2.shape, const), pl.BlockSpec(bd.shape, const),
                  pl.BlockSpec(tri.shape, const)],
        out_specs=[pl.BlockSpec((1, tt, BRANCH_W), lambda i, j: (i, j, 0)),
                   pl.BlockSpec((1, N_HEADS, HEAD_DIM, HEAD_DIM), lambda i, j: (i, 0, 0, 0))],
        out_shape=[jax.ShapeDtypeStruct((b, t, BRANCH_W), BF16),
                   jax.ShapeDtypeStruct((b, N_HEADS, HEAD_DIM, HEAD_DIM), F32)],
        scratch_shapes=[pltpu.VMEM((1, w), F32)] + [scr] * 7,
        compiler_params=_params("parallel", "arbitrary"), name="rwkv_seq",
    )(z, shift, s0, vec, wa, g2, bd, tri)


def _rwkv_step_kernel(z_ref, shift_ref, s0_ref, vec_ref, wa_ref, g2_ref, bd_ref, o_ref, s_ref):
    rows = 8
    z = jnp.broadcast_to(z_ref[0], (rows, z_ref.shape[2]))
    zprev = jnp.broadcast_to(shift_ref[0], z.shape)
    r, k2, v, kk, b, ld, g, bonus = _rwkv_prep(z, zprev, vec_ref, wa_ref, g2_ref, bd_ref)
    decay = jnp.exp(ld)
    eye = _iota((HEAD_DIM, HEAD_DIM), 0) == _iota((HEAD_DIM, HEAD_DIM), 1)
    ys = []
    for h in range(N_HEADS):
        hs = slice(h * HEAD_DIM, (h + 1) * HEAD_DIM)
        s_h = s0_ref[0, h]
        sk = jnp.sum(s_h * kk[0:1, hs], axis=1, keepdims=True)
        v_col = jnp.sum(jnp.where(eye, v[0:1, hs], 0.0), axis=1, keepdims=True)
        s_new = s_h * decay[0:1, hs] - sk * b[0:1, hs] + v_col * k2[0:1, hs]
        s_ref[0, h] = s_new
        y_col = jnp.sum(s_new * r[0:1, hs], axis=1, keepdims=True)
        ys.append(jnp.sum(jnp.where(eye, y_col, 0.0), axis=0, keepdims=True))
    y = jnp.broadcast_to(jnp.concatenate(ys, axis=1), (rows, BRANCH_W))
    o_ref[0] = _rwkv_finish(y, bonus, g, vec_ref, bd_ref)[0:1].astype(o_ref.dtype)


def rwkv_step(z, shift, s0, vec, wa, g2, bd):
    b, _, w = z.shape
    const = lambda i: (0, 0)
    return pl.pallas_call(
        _rwkv_step_kernel, grid=(b,),
        in_specs=[pl.BlockSpec((1, 1, w), lambda i: (i, 0, 0)),
                  pl.BlockSpec((1, 1, w), lambda i: (i, 0, 0)),
                  pl.BlockSpec((1, N_HEADS, HEAD_DIM, HEAD_DIM), lambda i: (i, 0, 0, 0)),
                  pl.BlockSpec(vec.shape, const), pl.BlockSpec(wa.shape, const),
                  pl.BlockSpec(g2.shape, const), pl.BlockSpec(bd.shape, const)],
        out_specs=[pl.BlockSpec((1, 1, BRANCH_W), lambda i: (i, 0, 0)),
                   pl.BlockSpec((1, N_HEADS, HEAD_DIM, HEAD_DIM), lambda i: (i, 0, 0, 0))],
        out_shape=[jax.ShapeDtypeStruct((b, 1, BRANCH_W), BF16),
                   jax.ShapeDtypeStruct((b, N_HEADS, HEAD_DIM, HEAD_DIM), F32)],
        compiler_params=_params("parallel"), name="rwkv_step",
    )(z, shift, s0, vec, wa, g2, bd)


def _stack_heads(x):
    rows = [x[:, h * HEAD_DIM:(h + 1) * HEAD_DIM] for h in range(N_HEADS)]
    return jnp.concatenate(rows + [jnp.zeros((8 - N_HEADS, HEAD_DIM), x.dtype)], axis=0)


def _unstack_heads(x):
    return jnp.concatenate([x[h:h + 1, :] for h in range(N_HEADS)], axis=1)


def _column(row):
    n = row.shape[1]
    eye = _iota((n, n), 0) == _iota((n, n), 1)
    return jnp.sum(jnp.where(eye, row, 0.0), axis=1, keepdims=True)


def _to_row(col):
    n = col.shape[0]
    eye = _iota((n, n), 0) == _iota((n, n), 1)
    return jnp.sum(jnp.where(eye, col, 0.0), axis=0, keepdims=True)


def _page_scores(k_page, q_cols, s_ref, n_pages):
    def body(p, c):
        for h, qc in enumerate(q_cols):
            s_ref[h, pl.ds(p, 1), :] = jnp.sum(k_page(p, h) * qc, axis=0, keepdims=True)
        return c

    lax.fori_loop(0, n_pages, body, 0)


def _page_values(v_page, p_ref, n_pages):
    def body(p, accs):
        return tuple(acc + v_page(p, h) * p_ref[h, pl.ds(p, 1), :] for h, acc in enumerate(accs))

    init = tuple(jnp.zeros((HEAD_DIM, p_ref.shape[2]), F32) for _ in range(N_HEADS))
    return [jnp.sum(a, axis=1, keepdims=True) for a in lax.fori_loop(0, n_pages, body, init)]


def _softmax_with_new(s, mask, s_new):
    s = jnp.where(mask, s, NEG)
    m = jnp.maximum(jnp.max(jnp.max(s, axis=1, keepdims=True), axis=0, keepdims=True), s_new)
    p = jnp.where(mask, jnp.exp(s - m), 0.0)
    p_new = jnp.exp(s_new - m)
    den = jnp.maximum(jnp.sum(jnp.sum(p, axis=1, keepdims=True), axis=0, keepdims=True) + p_new, 1e-30)
    return p / den, p_new / den


def _nsa_sample_kernel(pt_ref, z_ref, tab_ref, win_ref, w1_ref, pe_ref, w2_ref, cover_ref, pj_ref, hr_ref,
                       cmp_hbm, slc_hbm, o_ref, slc_row_ref, win_row_ref,
                       cbuf, sbuf, rows_s, lines_s, s_s, p_s, sem,
                       *, base, n_pages, page, past, n_cmp, n_slc, k_sel):
    bi = pl.program_id(0)
    n_lines = past // CMP_STRIDE

    def copies(p):
        pg = pt_ref[bi * n_pages + p] + base
        return (pltpu.make_async_copy(cmp_hbm.at[pg], cbuf.at[p], sem.at[0]),
                pltpu.make_async_copy(slc_hbm.at[pg], sbuf.at[p], sem.at[1]))

    for p in range(n_pages):
        for cp in copies(p):
            cp.start()

    z = z_ref[0]
    tab = tab_ref[...]
    q = z[:, :BRANCH_W] * SCALE
    q4 = _stack_heads(q)
    q4r_f32 = _stack_heads(_rope_heads(q, tab))
    q4r = q4r_f32.astype(BF16)
    slc_row = _rope(z[:, 384:512], tab, half_only=True)
    win_row = _rope(z[:, 512:640], tab, half_only=True)
    slc_row_ref[0] = slc_row
    win_row_ref[0] = win_row

    for p in range(n_pages):
        for cp in copies(p):
            cp.wait()

    def xpose(p, c):
        rows_s[pl.ds(pl.multiple_of(p * page, page), page), :] = cbuf[p].T
        return c

    lax.fori_loop(0, n_pages, xpose, 0)
    for l in range(CMP_STRIDE):
        lines_s[:, l * LANES:(l + 1) * LANES] = rows_s[pl.ds(l, n_lines, stride=CMP_STRIDE), :]

    cmpkv = _compress(lines_s[...], w1_ref, pe_ref, w2_ref)
    ck = cmpkv[:, :HEAD_DIM].astype(BF16)
    cv = cmpkv[:, HEAD_DIM:].astype(BF16)
    n = _iota((1, cmpkv.shape[0]), 1)
    cmask = (n * CMP_STRIDE + (CMP_BLOCK - 1) <= past) & (n < n_cmp)
    p_cmp = _msoftmax(_dot_nt(q4.astype(BF16), ck), cmask)
    o_cmp = _dot(p_cmp.astype(BF16), cv)
    psum = jnp.sum(jnp.where(_iota(p_cmp.shape, 0) < N_HEADS, p_cmp, 0.0), axis=0, keepdims=True)
    imp = _dot2_exact_rhs(jnp.broadcast_to(psum, p_cmp.shape), cover_ref[...])[0:1]
    t_pos = jnp.full((1, 1), past, jnp.int32)
    sel = _select_blocks(imp, t_pos, n_slc, k_sel)

    def with_new_key(s_past, mask, new_row, v_past):
        s_new = jnp.sum(q4r_f32 * new_row[:, :HEAD_DIM], axis=1, keepdims=True)
        s_past = jnp.where(mask, s_past, NEG)
        m = jnp.maximum(jnp.max(s_past, axis=1, keepdims=True), s_new)
        p_past = jnp.where(mask, jnp.exp(s_past - m), 0.0)
        p_new = jnp.exp(s_new - m)
        den = jnp.maximum(jnp.sum(p_past, axis=1, keepdims=True) + p_new, 1e-30)
        return _dot((p_past / den).astype(BF16), v_past) + (p_new / den) * new_row[:, HEAD_DIM:]

    sel_pages = _dot((pj_ref[...] * sel.astype(F32)).astype(BF16), hr_ref[...]) > 0.5
    _page_scores(lambda p, h: sbuf[p, 0], [_column(q4r_f32[h:h + 1, :]) for h in range(N_HEADS)], s_s, n_pages)
    p_new = []
    for h in range(N_HEADS):
        s_new = jnp.sum(q4r_f32[h:h + 1, :] * slc_row[:, :HEAD_DIM], axis=1, keepdims=True)
        p_s[h], pn = _softmax_with_new(s_s[h], sel_pages, s_new)
        p_new.append(pn)
    o_slc = [_to_row(col) + p_new[h] * slc_row[:, HEAD_DIM:]
             for h, col in enumerate(_page_values(lambda p, h: sbuf[p, 1], p_s, n_pages))]
    wrows = win_ref[0]
    wk = wrows[:, :HEAD_DIM].astype(BF16)
    wv = wrows[:, HEAD_DIM:].astype(BF16)
    o_win = with_new_key(_dot_nt(q4r, wk), jnp.full((1, wrows.shape[0]), True), win_row, wv)

    g = _sigmoid(z[:, 640:640 + 3 * N_HEADS])
    outs = []
    for h in range(N_HEADS):
        outs.append(g[:, 3 * h:3 * h + 1] * o_cmp[h:h + 1] + g[:, 3 * h + 1:3 * h + 2] * o_slc[h]
                    + g[:, 3 * h + 2:3 * h + 3] * o_win[h:h + 1])
    o_ref[0] = jnp.concatenate(outs, axis=1).astype(o_ref.dtype)


def nsa_sample(page_table, z_nsa, tab, win, w1c, pe2, w2c, cover, pj, hr, cache_cmp, cache_slc,
               layer, n_pool, page, n_cmp, n_slc):
    b = z_nsa.shape[0]
    n_pages = page_table.shape[1]
    past = n_pages * page
    kern = functools.partial(_nsa_sample_kernel, base=layer * n_pool, n_pages=n_pages, page=page, past=past,
                             n_cmp=n_cmp, n_slc=n_slc, k_sel=min(N_SELECT, n_slc))
    const = lambda i, pt: (0, 0)
    row = lambda i, pt: (i, 0, 0)
    grid_spec = pltpu.PrefetchScalarGridSpec(
        num_scalar_prefetch=1, grid=(b,),
        in_specs=[pl.BlockSpec((1, 1, z_nsa.shape[2]), row),
                  pl.BlockSpec(tab.shape, lambda i, pt: (0, 0, 0)),
                  pl.BlockSpec((1,) + win.shape[1:], row),
                  pl.BlockSpec(w1c.shape, const), pl.BlockSpec(pe2.shape, const),
                  pl.BlockSpec(w2c.shape, const), pl.BlockSpec(cover.shape, const),
                  pl.BlockSpec(pj.shape, const), pl.BlockSpec(hr.shape, const),
                  pl.BlockSpec(memory_space=pl.ANY), pl.BlockSpec(memory_space=pl.ANY)],
        out_specs=[pl.BlockSpec((1, 1, BRANCH_W), row), pl.BlockSpec((1, 1, LANES), row),
                   pl.BlockSpec((1, 1, LANES), row)],
        scratch_shapes=[pltpu.VMEM((n_pages, 2 * HEAD_DIM, page), F32),
                        pltpu.VMEM((n_pages, 2, HEAD_DIM, page), F32),
                        pltpu.VMEM((past, 2 * HEAD_DIM), F32),
                        pltpu.VMEM((past // CMP_STRIDE, CMP_STRIDE * LANES), F32),
                        pltpu.VMEM((N_HEADS, n_pages, page), F32),
                        pltpu.VMEM((N_HEADS, n_pages, page), F32),
                        pltpu.SemaphoreType.DMA((2,))])
    return pl.pallas_call(
        kern, grid_spec=grid_spec,
        out_shape=[jax.ShapeDtypeStruct((b, 1, BRANCH_W), BF16),
                   jax.ShapeDtypeStruct((b, 1, LANES), F32), jax.ShapeDtypeStruct((b, 1, LANES), F32)],
        compiler_params=_params("arbitrary"), name="nsa_sample",
    )(page_table.reshape(-1), z_nsa, tab, win, w1c, pe2, w2c, cover, pj, hr, cache_cmp, cache_slc)


def _moba_sample_kernel(pt_ref, z_ref, tab_ref, kv_hbm, o_ref, row_ref, buf, s_s, p_s, sem,
                        *, base, n_pages, pages_per_block, topk):
    bi = pl.program_id(0)

    def copy(p):
        pg = pt_ref[bi * n_pages + p] + base
        return pltpu.make_async_copy(kv_hbm.at[pg], buf.at[p], sem.at[0])

    for p in range(n_pages):
        copy(p).start()

    z = z_ref[0]
    tab = tab_ref[...]
    q = z[:, :BRANCH_W] * SCALE
    qr = _rope_heads(q, tab)
    k_new = _rope_heads(z[:, BRANCH_W:2 * BRANCH_W], tab)
    v_new = z[:, 2 * BRANCH_W:3 * BRANCH_W]
    row_ref[0] = jnp.concatenate([k_new, v_new], axis=1)

    for p in range(n_pages):
        copy(p).wait()

    heads = [slice(h * HEAD_DIM, (h + 1) * HEAD_DIM) for h in range(N_HEADS)]
    _page_scores(lambda p, h: buf[p, h], [_column(qr[:, hs]) for hs in heads], s_s, n_pages)
    pi = _iota((n_pages, n_pages), 0)
    pj = _iota((n_pages, n_pages), 1)
    shift = pages_per_block.bit_length() - 1
    same_block = lax.shift_right_logical(pi, shift) == lax.shift_right_logical(pj, shift)
    p_new = []
    for h, hs in enumerate(heads):
        s = s_s[h]
        page_sum = jnp.sum(s, axis=1, keepdims=True)
        page_sum_row = jnp.sum(jnp.where(pi == pj, page_sum, 0.0), axis=0, keepdims=True)
        gate_col = jnp.sum(jnp.where(same_block, page_sum_row, 0.0), axis=1, keepdims=True)
        gate_row = jnp.sum(jnp.where(same_block, page_sum, 0.0), axis=0, keepdims=True)
        beats = ((gate_row > gate_col) | ((gate_row == gate_col) & (pj < pi))) & jnp.logical_not(same_block)
        picked = jnp.sum(beats.astype(F32), axis=1, keepdims=True) < pages_per_block * topk - 0.5
        s_new = jnp.sum(qr[:, hs] * k_new[:, hs], axis=1, keepdims=True)
        p_s[h], pn = _softmax_with_new(s, picked, s_new)
        p_new.append(pn)
    cols = _page_values(lambda p, h: buf[p, N_HEADS + h], p_s, n_pages)
    outs = [_to_row(cols[h]) + p_new[h] * v_new[:, hs] for h, hs in enumerate(heads)]
    o_ref[0] = jnp.concatenate(outs, axis=1).astype(o_ref.dtype)


def moba_sample(page_table, z_moba, tab, cache, layer, n_pool, page):
    b = z_moba.shape[0]
    n_pages = page_table.shape[1]
    past = n_pages * page
    assert past % MOBA_BLOCK == 0 and MOBA_BLOCK % page == 0
    ppb = MOBA_BLOCK // page
    assert ppb & (ppb - 1) == 0
    kern = functools.partial(_moba_sample_kernel, base=layer * n_pool, n_pages=n_pages, pages_per_block=ppb,
                             topk=min(MOBA_TOPK, past // MOBA_BLOCK + 1))
    row = lambda i, pt: (i, 0, 0)
    grid_spec = pltpu.PrefetchScalarGridSpec(
        num_scalar_prefetch=1, grid=(b,),
        in_specs=[pl.BlockSpec((1, 1, z_moba.shape[2]), row),
                  pl.BlockSpec(tab.shape, lambda i, pt: (0, 0, 0)),
                  pl.BlockSpec(memory_space=pl.ANY)],
        out_specs=[pl.BlockSpec((1, 1, BRANCH_W), row), pl.BlockSpec((1, 1, 2 * BRANCH_W), row)],
        scratch_shapes=[pltpu.VMEM((n_pages, 2 * N_HEADS, HEAD_DIM, page), F32),
                        pltpu.VMEM((N_HEADS, n_pages, page), F32),
                        pltpu.VMEM((N_HEADS, n_pages, page), F32),
                        pltpu.SemaphoreType.DMA((1,))])
    return pl.pallas_call(
        kern, grid_spec=grid_spec,
        out_shape=[jax.ShapeDtypeStruct((b, 1, BRANCH_W), BF16),
                   jax.ShapeDtypeStruct((b, 1, 2 * BRANCH_W), F32)],
        compiler_params=_params("arbitrary"), name="moba_sample",
    )(page_table.reshape(-1), z_moba, tab, cache)


def _rope_table(pos):
    inv = ROPE_THETA ** (-jnp.arange(ROT_HALF, dtype=F32) / ROT_HALF)
    ang = pos.astype(F32)[:, None] * inv[None, :]
    cos, sin = jnp.cos(ang), jnp.sin(ang)
    t = pos.shape[0]
    ones = jnp.ones((t, HEAD_DIM - 2 * ROT_HALF), F32)
    zeros = jnp.zeros((t, HEAD_DIM - 2 * ROT_HALF), F32)
    z8 = jnp.zeros((t, ROT_HALF), F32)
    c = jnp.concatenate([cos, cos, ones], axis=1)
    sa = jnp.concatenate([-sin, z8, zeros], axis=1)
    sb = jnp.concatenate([z8, sin, zeros], axis=1)
    return jnp.stack([jnp.tile(c, (1, 2)), jnp.tile(sa, (1, 2)), jnp.tile(sb, (1, 2))], axis=0)


def _cover_matrix(n_rows, n_cmp, n_slc, width):
    starts = np.arange(n_rows) * CMP_STRIDE
    blk = np.arange(width) * SLC_BLOCK
    cover = (starts[:, None] < blk[None, :] + SLC_BLOCK) & (starts[:, None] + CMP_BLOCK > blk[None, :])
    cover &= (np.arange(n_rows)[:, None] < n_cmp) & (np.arange(width)[None, :] < n_slc)
    return jnp.asarray(cover, BF16)


def _expand_matrix(n_blocks, block, n_keys):
    e = np.arange(n_blocks)[:, None] == (np.arange(n_keys)[None, :] // block)
    return jnp.asarray(e, BF16)


def _round_up(n, m):
    return -(-n // m) * m


COL_NSA, COL_MOBA, COL_CROSS, COL_RWKV, COL_MERGE = 0, 768, 1536, 2304, 4096


def _pack_w_in(w_in):
    def zeros(n):
        return jnp.zeros(w_in.shape[:2] + (n,), w_in.dtype)

    parts = [w_in[..., 0:652], zeros(NSA_PAD - 652), w_in[..., 1708:2476], w_in[..., 2476:2732],
             zeros(COL_RWKV - COL_CROSS - BRANCH_W), w_in[..., 652:1708], zeros(RWKV_PAD - RWKV_IN),
             zeros(COL_MERGE - COL_RWKV - RWKV_PAD), w_in[..., 2732:]]
    return jnp.concatenate(parts, axis=-1).astype(BF16)


def _layer_weights(l, nsa_pe, nsa_phi_w1, nsa_phi_w2, rwkv_mu, rwkv_w0, rwkv_w2, rwkv_a0, rwkv_a2,
                   rwkv_g2, rwkv_k_k, rwkv_k_a, rwkv_r_k, rwkv_ln_w, rwkv_ln_b):
    w = {}
    w1 = nsa_phi_w1[l]
    hidden = w1.shape[2]
    w1r = w1.reshape(2, 2, CMP_STRIDE, HEAD_DIM, hidden)
    w1c = jnp.zeros((2, CMP_STRIDE, 2, HEAD_DIM, 2, hidden), F32)
    w1c = w1c.at[:, :, 0, :, 0, :].set(w1r[0]).at[:, :, 1, :, 1, :].set(w1r[1])
    w1c = w1c.reshape(2, CMP_STRIDE * 2 * HEAD_DIM, 2 * hidden)
    w['w1c'] = jnp.concatenate([w1c[0], w1c[1]], axis=1).astype(BF16)
    pe = nsa_pe[l].reshape(2, CMP_STRIDE * 2 * HEAD_DIM)
    w['pe2'] = jnp.pad(pe, ((0, 6), (0, 0)))
    w2 = nsa_phi_w2[l]
    w2c = jnp.zeros((2, hidden, 2, HEAD_DIM), F32).at[0, :, 0, :].set(w2[0]).at[1, :, 1, :].set(w2[1])
    w['w2c'] = w2c.reshape(2 * hidden, 2 * HEAD_DIM).astype(BF16)
    vec = jnp.zeros((8, RWKV_PAD), F32)
    vec = vec.at[0, :RWKV_IN].set(rwkv_mu[l])
    vec = vec.at[1, 0:256].set(rwkv_w0[l]).at[1, 256:512].set(rwkv_a0[l])
    for i, p in enumerate((rwkv_k_k, rwkv_k_a, rwkv_r_k, rwkv_ln_w, rwkv_ln_b)):
        vec = vec.at[2 + i, 0:256].set(p[l])
    w['rwkv_vec'] = vec
    wa = jnp.zeros((128, 512), F32).at[0:64, 0:256].set(rwkv_w2[l]).at[64:128, 256:512].set(rwkv_a2[l])
    w['rwkv_wa'] = wa.astype(BF16)
    w['rwkv_g2'] = jnp.pad(rwkv_g2[l], ((0, 256 - rwkv_g2.shape[1]), (0, 0))).astype(BF16)
    return w


def _project(h, w_all, l, d):
    z_nsa = mm(h, w_all, l, COL_NSA, NSA_PAD)
    z_rwkv = mm(h, w_all, l, COL_RWKV, RWKV_PAD)
    z_moba = mm(h, w_all, l, COL_MOBA, 3 * BRANCH_W)
    z_cross = mm(h, w_all, l, COL_CROSS, BRANCH_W)
    gates = mm(h, w_all, l, COL_MERGE, 4 * d, out_dtype=BF16, gate=True)
    return z_nsa, z_rwkv, z_moba, z_cross, gates


def _mixers_prompt(h, w_all, l, w, tab, mem_kv, consts, b, t):
    z_nsa, z_rwkv, z_moba, z_cross, gates = _project(h, w_all, l, h.shape[1])
    z_nsa = z_nsa.reshape(b, t, NSA_PAD)
    z_rwkv = z_rwkv.reshape(b, t, RWKV_PAD)
    z_moba = z_moba.reshape(b, t, 3 * BRANCH_W)
    z_cross = z_cross.reshape(b, t, BRANCH_W)

    new_cmp = z_nsa[:, :, 256:384]
    new_slc, new_win = nsa_rope(z_nsa, tab)
    cmp_kv = compress(new_cmp.reshape(b, t // CMP_STRIDE, CMP_STRIDE * LANES), w['w1c'], w['pe2'], w['w2c'])
    o_nsa = nsa_attn(z_nsa, tab, cmp_kv, new_slc, new_win, consts['cover_p'], consts['expand_p'],
                     consts['n_cmp_p'], consts['n_slc_p'])

    zero_shift = jnp.zeros((b, 1, RWKV_PAD), F32)
    zero_state = jnp.zeros((b, N_HEADS, HEAD_DIM, HEAD_DIM), F32)
    o_rwkv, new_state = rwkv_seq(z_rwkv, zero_shift, zero_state, w['rwkv_vec'], w['rwkv_wa'], w['rwkv_g2'],
                                 consts['bd'], consts['tri'])
    new_shift = z_rwkv[:, t - 1, :RWKV_IN]

    new_moba, means = moba_rope(z_moba, tab)
    o_moba = moba_attn(z_moba, tab, new_moba, means.reshape(b, -1, BRANCH_W))
    o_cross = cross_attn(z_cross, mem_kv)

    outs = [o.reshape(b * t, BRANCH_W) for o in (o_nsa, o_rwkv, o_moba, o_cross)]
    win_keep = min(WINDOW, t)
    caches = (new_cmp.reshape(b, t, 2, HEAD_DIM), new_slc.reshape(b, t, 2, HEAD_DIM),
              new_win[:, t - win_keep:].reshape(b, win_keep, 2, HEAD_DIM),
              new_moba.reshape(b, t, 2, N_HEADS, HEAD_DIM), new_state, new_shift)
    return outs, gates, caches


def _mixers_sample(h, w_all, l, w, tab, page_table, caches_in, consts, b):
    cmp_pages, slc_pages, cache_win, moba_pages, cache_mem, state_rwkv, state_shift, n_pool, page = caches_in
    z_nsa, z_rwkv, z_moba, z_cross, gates = _project(h, w_all, l, h.shape[1])
    z_nsa = z_nsa.reshape(b, 1, NSA_PAD)
    z_rwkv = z_rwkv.reshape(b, 1, RWKV_PAD)
    z_moba = z_moba.reshape(b, 1, 3 * BRANCH_W)
    z_cross = z_cross.reshape(b, 1, BRANCH_W)

    win = cache_win[l].reshape(b, -1, LANES)
    o_nsa, slc_row, win_row = nsa_sample(
        page_table, z_nsa, tab, win, w['w1c'], w['pe2'], w['w2c'], consts['cover_s'], consts['pj_s'],
        consts['hr_s'], cmp_pages, slc_pages, l, n_pool, page, consts['n_cmp_s'], consts['n_slc_s'])
    new_cmp = z_nsa[:, :, 256:384]
    full_win = jnp.concatenate([win, win_row], axis=1)
    keep = min(WINDOW, full_win.shape[1])
    new_buf = full_win[:, full_win.shape[1] - keep:]

    shift = jnp.pad(state_shift[l], ((0, 0), (0, RWKV_PAD - RWKV_IN))).reshape(b, 1, RWKV_PAD)
    o_rwkv, new_state = rwkv_step(z_rwkv, shift, state_rwkv[l], w['rwkv_vec'], w['rwkv_wa'], w['rwkv_g2'],
                                  consts['bd'])
    new_shift = z_rwkv[:, 0, :RWKV_IN]

    o_moba, moba_row = moba_sample(page_table, z_moba, tab, moba_pages, l, n_pool, page)
    o_cross = cross_attn(z_cross, cache_mem[l].reshape(b, -1, 2 * BRANCH_W))

    outs = [o.reshape(b, BRANCH_W) for o in (o_nsa, o_rwkv, o_moba, o_cross)]
    caches = (new_cmp.reshape(b, 1, 2, HEAD_DIM), slc_row.reshape(b, 1, 2, HEAD_DIM),
              new_buf.reshape(b, keep, 2, HEAD_DIM), moba_row.reshape(b, 1, 2, N_HEADS, HEAD_DIM),
              new_state, new_shift)
    return outs, gates, caches


def kernel(x_prompt, x_sample, cache_nsa_cmp, cache_nsa_slc, cache_nsa_win, cache_moba, cache_mem, state_rwkv, state_shift, page_table, mem_prompt, g_pre_mix, g_post_mix, g_pre_ffn, g_post_ffn, g_mem, w_in, nsa_pe, nsa_phi_w1, nsa_phi_w2, rwkv_mu, rwkv_w0, rwkv_w2, rwkv_a0, rwkv_a2, rwkv_g2, rwkv_k_k, rwkv_k_a, rwkv_r_k, rwkv_ln_w, rwkv_ln_b, w_mem_kv, w_branch, w_out, w_ffn_gate, w_ffn_up, w_ffn_down):
    bp, t, d = x_prompt.shape
    bs = x_sample.shape[0]
    assert x_sample.shape[1] == 1
    depth = w_in.shape[0]
    page = cache_nsa_cmp.shape[2]
    past = page_table.shape[1] * page
    n_mem = mem_prompt.shape[1]
    assert t % MOBA_BLOCK == 0 and past % MOBA_BLOCK == 0 and page % SLC_BLOCK == 0

    n_cmp_p = (t - CMP_BLOCK) // CMP_STRIDE + 1
    n_slc_p = -(-t // SLC_BLOCK)
    n_cmp_s = (past + 1 - CMP_BLOCK) // CMP_STRIDE + 1
    n_slc_s = -(-(past + 1) // SLC_BLOCK)
    consts = {
        'n_cmp_p': n_cmp_p, 'n_slc_p': n_slc_p, 'n_cmp_s': n_cmp_s, 'n_slc_s': n_slc_s,
        'cover_p': _cover_matrix(t // CMP_STRIDE, n_cmp_p, n_slc_p, _round_up(n_slc_p, LANES)),
        'expand_p': _expand_matrix(_round_up(n_slc_p, LANES), SLC_BLOCK, t),
        'cover_s': _cover_matrix(past // CMP_STRIDE, n_cmp_s, n_slc_s, _round_up(n_slc_s, LANES)),
        'pj_s': jnp.asarray(np.arange(_round_up(n_slc_s, LANES))[None, :] // (page // SLC_BLOCK)
                            == np.arange(past // page)[:, None], F32),
        'hr_s': jnp.asarray(np.arange(_round_up(n_slc_s, LANES))[:, None] % (page // SLC_BLOCK)
                            == np.arange(page)[None, :] // SLC_BLOCK, BF16),
        'bd': jnp.asarray(np.arange(BRANCH_W)[:, None] // HEAD_DIM == np.arange(BRANCH_W)[None, :] // HEAD_DIM, BF16),
        'tri': jnp.asarray(np.tril(np.ones((RWKV_CHUNK, RWKV_CHUNK), np.float32))),
    }
    tab_p = _rope_table(jnp.arange(t, dtype=jnp.int32))
    tab_s = _rope_table(jnp.full((1,), past, jnp.int32))

    xp = x_prompt.reshape(bp * t, d)
    xs = x_sample.reshape(bs, d)
    mem_flat = mem_prompt.reshape(bp * n_mem, d)
    n_pool = cache_nsa_cmp.shape[1]
    cmp_pages = jnp.transpose(cache_nsa_cmp, (0, 1, 3, 4, 2)).reshape(depth * n_pool, 2 * HEAD_DIM, page)
    slc_pages = jnp.transpose(cache_nsa_slc, (0, 1, 3, 4, 2)).reshape(depth * n_pool, 2, HEAD_DIM, page)
    moba_pages = jnp.transpose(cache_moba, (0, 1, 3, 4, 5, 2)).reshape(depth * n_pool, 2 * N_HEADS, HEAD_DIM, page)
    sample_caches = (cmp_pages, slc_pages, cache_nsa_win, moba_pages, cache_mem, state_rwkv, state_shift,
                     n_pool, page)
    w_all = _pack_w_in(w_in)
    w_mem, w_br, w_o = w_mem_kv.astype(BF16), w_branch.astype(BF16), w_out.astype(BF16)
    w_fg, w_fu, w_fd = w_ffn_gate.astype(BF16), w_ffn_up.astype(BF16), w_ffn_down.astype(BF16)
    outs_p, outs_s, mem_out = [], [], []
    for l in range(depth):
        w = _layer_weights(l, nsa_pe, nsa_phi_w1, nsa_phi_w2, rwkv_mu, rwkv_w0, rwkv_w2, rwkv_a0,
                           rwkv_a2, rwkv_g2, rwkv_k_k, rwkv_k_a, rwkv_r_k, rwkv_ln_w, rwkv_ln_b)
        mem_kv = mm(rms_cast(mem_flat, g_mem[l]), w_mem, l, 0, 2 * BRANCH_W).reshape(bp, n_mem, 2 * BRANCH_W)
        mem_out.append(mem_kv.reshape(bp, n_mem, 2, N_HEADS, HEAD_DIM))

        o_p, gates_p, new_p = _mixers_prompt(rms_cast(xp, g_pre_mix[l]), w_all, l, w, tab_p, mem_kv, consts, bp, t)
        xp = merge(xp, o_p, gates_p, w_br, w_o, g_post_mix[l], l)
        xp = ffn(xp, g_pre_ffn[l], w_fg, w_fu, w_fd, g_post_ffn[l], l)

        o_s, gates_s, new_s = _mixers_sample(rms_cast(xs, g_pre_mix[l]), w_all, l, w, tab_s, page_table,
                                             sample_caches, consts, bs)
        xs = merge(xs, o_s, gates_s, w_br, w_o, g_post_mix[l], l)
        xs = ffn(xs, g_pre_ffn[l], w_fg, w_fu, w_fd, g_post_ffn[l], l)
        outs_p.append(new_p)
        outs_s.append(new_s)

    def stack(outs, i):
        return jnp.stack([o[i] for o in outs], axis=0)

    return (xp.reshape(bp, t, d), xs.reshape(bs, 1, d),
            stack(outs_p, 0), stack(outs_s, 0), stack(outs_p, 1), stack(outs_s, 1),
            stack(outs_p, 2), stack(outs_s, 2), stack(outs_p, 3), stack(outs_s, 3),
            jnp.stack(mem_out, axis=0),
            stack(outs_p, 4), stack(outs_s, 4), stack(outs_p, 5), stack(outs_s, 5))
```

```python
import functools

import jax
import jax.numpy as jnp
import numpy as np
from jax import lax
from jax.experimental import pallas as pl
from jax.experimental.pallas import tpu as pltpu

F32 = jnp.float32
BF16 = jnp.bfloat16

HEAD_DIM = 64
N_HEADS = 4
BRANCH_W = N_HEADS * HEAD_DIM
ROT_HALF = 8
ROPE_THETA = 500000.0
CMP_BLOCK = 32
CMP_STRIDE = 16
SLC_BLOCK = 64
N_SELECT = 16
WINDOW = 512
MOBA_BLOCK = 256
MOBA_TOPK = 3
RWKV_IN = 1056
RWKV_PAD = 1152
NSA_PAD = 768
RWKV_GN_EPS = 64e-5
RWKV_CHUNK = 64
NEG = -1e30
SCALE = HEAD_DIM ** -0.5
LANES = 128
VMEM_LIMIT = 48 * 1024 * 1024


def _iota(shape, dim):
    return lax.broadcasted_iota(jnp.int32, shape, dim)


def _dot(a, b):
    return lax.dot_general(a, b, (((1,), (0,)), ((), ())), preferred_element_type=F32)


def _dot_nt(a, b):
    return lax.dot_general(a, b, (((1,), (1,)), ((), ())), preferred_element_type=F32)


def _dot_tn(a, b):
    return lax.dot_general(a, b, (((0,), (0,)), ((), ())), preferred_element_type=F32)


def _split(x):
    hi = x.astype(BF16)
    lo = (x - hi.astype(F32)).astype(BF16)
    return hi, lo


def _dot3(a, b, dot=_dot):
    ah, al = _split(a)
    bh, bl = _split(b)
    return dot(ah, bh) + dot(al, bh) + dot(ah, bl)


def _dot2_exact_rhs(a, b_bf16):
    ah, al = _split(a)
    return _dot(ah, b_bf16) + _dot(al, b_bf16)


def _sigmoid(x):
    return 1.0 / (1.0 + jnp.exp(-x))


def _rms(x, g, eps=1e-6):
    return x * lax.rsqrt(jnp.mean(x * x, axis=-1, keepdims=True) + eps) * g


def _msoftmax(s, mask):
    s = jnp.where(mask, s, NEG)
    p = jnp.where(mask, jnp.exp(s - jnp.max(s, axis=-1, keepdims=True)), 0.0)
    return p / jnp.maximum(jnp.sum(p, axis=-1, keepdims=True), 1e-30)


def _flash_step(q, k, v, mask, m, l, acc):
    s = jnp.where(mask, _dot_nt(q, k), NEG)
    m_new = jnp.maximum(m, jnp.max(s, axis=-1, keepdims=True))
    p = jnp.where(mask, jnp.exp(s - m_new), 0.0)
    alpha = jnp.exp(m - m_new)
    l = alpha * l + jnp.sum(p, axis=-1, keepdims=True)
    acc = alpha * acc + _dot(p.astype(BF16), v)
    return m_new, l, acc


def _rope(x, tab, half_only=False):
    c, sa, sb = tab[0], tab[1], tab[2]
    if half_only:
        first = _iota(x.shape, 1) < HEAD_DIM
        c = jnp.where(first, c, 1.0)
        sa = jnp.where(first, sa, 0.0)
        sb = jnp.where(first, sb, 0.0)
    return x * c + pltpu.roll(x, LANES - ROT_HALF, 1) * sa + pltpu.roll(x, ROT_HALF, 1) * sb


def _rope_heads(x, tab):
    return jnp.concatenate([_rope(x[:, :LANES], tab), _rope(x[:, LANES:], tab)], axis=1)


def _rank_lt(score, n_real, k):
    lane = _iota(score.shape, 1)
    cnt = jnp.zeros(score.shape, jnp.int32)
    for i in range(n_real):
        ci = score[:, i:i + 1]
        beats = (ci > score) | ((ci == score) & (lane > i))
        cnt = cnt + beats.astype(jnp.int32)
    return cnt < k


def _pick_tile(n, target, mult=8):
    if n <= target:
        return n
    for t in range(target, 0, -1):
        if n % t == 0 and t % mult == 0:
            return t
    return n


def _params(*sem):
    return pltpu.CompilerParams(dimension_semantics=sem, vmem_limit_bytes=VMEM_LIMIT)


def _rms_cast_kernel(x_ref, g_ref, o_ref):
    o_ref[...] = _rms(x_ref[...], g_ref[...]).astype(BF16)


def rms_cast(x, g):
    m, d = x.shape
    tm = _pick_tile(m, 1024, 16)
    return pl.pallas_call(
        _rms_cast_kernel, grid=(m // tm,),
        in_specs=[pl.BlockSpec((tm, d), lambda i: (i, 0)), pl.BlockSpec((1, d), lambda i: (0, 0))],
        out_specs=pl.BlockSpec((tm, d), lambda i: (i, 0)),
        out_shape=jax.ShapeDtypeStruct((m, d), BF16),
        compiler_params=_params("parallel"), name="rms_cast",
    )(x, g.reshape(1, d))


def _mm_kernel(a_ref, b_ref, o_ref, *, gate):
    y = _dot(a_ref[...], b_ref[...])
    o_ref[...] = (_sigmoid(y) if gate else y).astype(o_ref.dtype)


def mm(a, w, layer, col0, n, out_dtype=F32, gate=False):
    m, k = a.shape
    tm = _pick_tile(m, 1024, 16)
    tn = n if n <= 1536 else _pick_tile(n, 1024, LANES)
    assert col0 % tn == 0 and n % tn == 0
    c0 = col0 // tn
    return pl.pallas_call(
        functools.partial(_mm_kernel, gate=gate), grid=(m // tm, n // tn),
        in_specs=[pl.BlockSpec((tm, k), lambda i, j: (i, 0)),
                  pl.BlockSpec((None, k, tn), lambda i, j: (layer, 0, c0 + j))],
        out_specs=pl.BlockSpec((tm, tn), lambda i, j: (i, j)),
        out_shape=jax.ShapeDtypeStruct((m, n), out_dtype),
        compiler_params=_params("parallel", "parallel"), name="mm",
    )(a, w)


def _ffn_kernel(x_ref, g1_ref, wg_ref, wu_ref, wd_ref, g2_ref, o_ref, h_ref, acc_ref):
    f = pl.program_id(1)

    @pl.when(f == 0)
    def _():
        h_ref[...] = _rms(x_ref[...], g1_ref[...]).astype(BF16)
        acc_ref[...] = jnp.zeros_like(acc_ref)

    h = h_ref[...]
    gate = _dot(h, wg_ref[...])
    up = _dot(h, wu_ref[...])
    act = (gate * _sigmoid(gate) * up).astype(BF16)
    acc_ref[...] += _dot(act, wd_ref[...])

    @pl.when(f == pl.num_programs(1) - 1)
    def _():
        o_ref[...] = x_ref[...] + _rms(acc_ref[...], g2_ref[...])


def ffn(x, g_pre, wg, wu, wd, g_post, layer):
    m, d = x.shape
    dff = wg.shape[2]
    tm = _pick_tile(m, 512, 16)
    tf = _pick_tile(dff, 256, LANES)
    return pl.pallas_call(
        _ffn_kernel, grid=(m // tm, dff // tf),
        in_specs=[pl.BlockSpec((tm, d), lambda i, f: (i, 0)),
                  pl.BlockSpec((1, d), lambda i, f: (0, 0)),
                  pl.BlockSpec((None, d, tf), lambda i, f: (layer, 0, f)),
                  pl.BlockSpec((None, d, tf), lambda i, f: (layer, 0, f)),
                  pl.BlockSpec((None, tf, d), lambda i, f: (layer, f, 0)),
                  pl.BlockSpec((1, d), lambda i, f: (0, 0))],
        out_specs=pl.BlockSpec((tm, d), lambda i, f: (i, 0)),
        out_shape=jax.ShapeDtypeStruct((m, d), F32),
        scratch_shapes=[pltpu.VMEM((tm, d), BF16), pltpu.VMEM((tm, d), F32)],
        compiler_params=_params("parallel", "arbitrary"), name="ffn",
    )(x, g_pre.reshape(1, d), wg, wu, wd, g_post.reshape(1, d))


def _merge_kernel(x_ref, o0_ref, o1_ref, o2_ref, o3_ref, gate_ref, wb_ref, wo_ref, g_ref, out_ref):
    d = x_ref.shape[1]
    merged = None
    for b, o_ref in enumerate((o0_ref, o1_ref, o2_ref, o3_ref)):
        term = gate_ref[:, b * d:(b + 1) * d].astype(F32) * _dot(o_ref[...], wb_ref[b])
        merged = term if merged is None else merged + term
    y = _dot(merged.astype(BF16), wo_ref[...])
    out_ref[...] = x_ref[...] + _rms(y, g_ref[...])


def merge(x, outs, gates, w_branch, w_out, g_post, layer):
    m, d = x.shape
    bw = outs[0].shape[1]
    tm = _pick_tile(m, 512, 16)
    row = lambda i: (i, 0)
    return pl.pallas_call(
        _merge_kernel, grid=(m // tm,),
        in_specs=[pl.BlockSpec((tm, d), row)] + [pl.BlockSpec((tm, bw), row)] * 4 + [
            pl.BlockSpec((tm, 4 * d), row),
            pl.BlockSpec((None, 4, bw, d), lambda i: (layer, 0, 0, 0)),
            pl.BlockSpec((None, d, d), lambda i: (layer, 0, 0)),
            pl.BlockSpec((1, d), lambda i: (0, 0))],
        out_specs=pl.BlockSpec((tm, d), row),
        out_shape=jax.ShapeDtypeStruct((m, d), F32),
        compiler_params=_params("parallel"), name="merge",
    )(x, *outs, gates, w_branch, w_out, g_post.reshape(1, d))


def _nsa_rope_kernel(z_ref, tab_ref, slc_ref, win_ref):
    z = z_ref[0]
    tab = tab_ref[...]
    slc_ref[0] = _rope(z[:, 384:512], tab, half_only=True)
    win_ref[0] = _rope(z[:, 512:640], tab, half_only=True)


def nsa_rope(z_nsa, tab):
    b, t, w = z_nsa.shape
    tt = _pick_tile(t, 512)
    out = jax.ShapeDtypeStruct((b, t, LANES), F32)
    return pl.pallas_call(
        _nsa_rope_kernel, grid=(b, t // tt),
        in_specs=[pl.BlockSpec((1, tt, w), lambda i, j: (i, j, 0)),
                  pl.BlockSpec((3, tt, LANES), lambda i, j: (0, j, 0))],
        out_specs=[pl.BlockSpec((1, tt, LANES), lambda i, j: (i, j, 0))] * 2,
        out_shape=[out, out],
        compiler_params=_params("parallel", "parallel"), name="nsa_rope",
    )(z_nsa, tab)


def _moba_rope_kernel(z_ref, tab_ref, kv_ref, mean_ref):
    z = z_ref[0]
    kr = _rope_heads(z[:, BRANCH_W:2 * BRANCH_W], tab_ref[...])
    kv_ref[0] = jnp.concatenate([kr, z[:, 2 * BRANCH_W:3 * BRANCH_W]], axis=1)
    mean_ref[0, 0] = jnp.sum(kr, axis=0, keepdims=True) * (1.0 / MOBA_BLOCK)


def moba_rope(z_moba, tab):
    b, t, w = z_moba.shape
    nb = t // MOBA_BLOCK
    return pl.pallas_call(
        _moba_rope_kernel, grid=(b, nb),
        in_specs=[pl.BlockSpec((1, MOBA_BLOCK, w), lambda i, j: (i, j, 0)),
                  pl.BlockSpec((3, MOBA_BLOCK, LANES), lambda i, j: (0, j, 0))],
        out_specs=[pl.BlockSpec((1, MOBA_BLOCK, 2 * BRANCH_W), lambda i, j: (i, j, 0)),
                   pl.BlockSpec((1, 1, 1, BRANCH_W), lambda i, j: (i, j, 0, 0))],
        out_shape=[jax.ShapeDtypeStruct((b, t, 2 * BRANCH_W), F32),
                   jax.ShapeDtypeStruct((b, nb, 1, BRANCH_W), F32)],
        compiler_params=_params("parallel", "parallel"), name="moba_rope",
    )(z_moba, tab)


def _compress(g, w1_ref, pe_ref, w2_ref):
    r = g.shape[0]
    ab = _dot(g.astype(BF16), w1_ref[...])
    pe = _dot(pe_ref[...].astype(BF16), w1_ref[...])
    half = ab.shape[1] // 2
    nxt = pltpu.roll(ab[:, half:], r - 1, 0)
    pre = ab[:, :half] + nxt + pe[0:1, :half] + pe[1:2, half:]
    hid = pre * _sigmoid(pre)
    return _dot(hid.astype(BF16), w2_ref[...])


def _compress_kernel(g_ref, w1_ref, pe_ref, w2_ref, o_ref):
    o_ref[0] = _compress(g_ref[0], w1_ref, pe_ref, w2_ref)


def compress(g, w1c, pe2, w2c):
    b, r, w = g.shape
    const = lambda i: (0, 0)
    return pl.pallas_call(
        _compress_kernel, grid=(b,),
        in_specs=[pl.BlockSpec((1, r, w), lambda i: (i, 0, 0)),
                  pl.BlockSpec(w1c.shape, const), pl.BlockSpec(pe2.shape, const),
                  pl.BlockSpec(w2c.shape, const)],
        out_specs=pl.BlockSpec((1, r, LANES), lambda i: (i, 0, 0)),
        out_shape=jax.ShapeDtypeStruct((b, r, LANES), F32),
        compiler_params=_params("parallel"), name="nsa_compress",
    )(g, w1c, pe2, w2c)


def _rank_lt_rows(score, n_real, k):
    row = _iota(score.shape, 0)
    cnt = jnp.zeros(score.shape, jnp.int32)
    for i in range(n_real):
        ri = score[i:i + 1, :]
        beats = (ri > score) | ((ri == score) & (row > i))
        cnt = cnt + beats.astype(jnp.int32)
    return cnt < k


def _select_blocks(imp, t_pos, n_slc, k_sel):
    j = _iota(imp.shape, 1)
    own = lax.shift_right_logical(t_pos, 6)
    causal = (j * SLC_BLOCK <= t_pos) & (j < n_slc)
    forced = (j == 0) | (j == own) | (j == own - 1)
    score = jnp.where(forced, -NEG, jnp.where(causal, imp, NEG))
    return _rank_lt(score, n_slc, k_sel) & causal


def _nsa_attn_kernel(z_ref, tab_ref, cmp_ref, slc_ref, win_ref, cover_ref, expand_ref, o_ref,
                     *, tq, n_cmp, n_slc, k_sel):
    qi = pl.program_id(1)
    z = z_ref[0]
    q = z[:, :BRANCH_W] * SCALE
    qr = _rope_heads(q, tab_ref[...])
    t_pos = qi * tq + _iota((tq, 1), 0)
    heads = [slice(h * HEAD_DIM, (h + 1) * HEAD_DIM) for h in range(N_HEADS)]

    cmpkv = cmp_ref[0]
    ck = cmpkv[:, :HEAD_DIM].astype(BF16)
    cv = cmpkv[:, HEAD_DIM:].astype(BF16)
    t_row = qi * tq + _iota((1, tq), 1)
    n = _iota((cmpkv.shape[0], 1), 0)
    cmask = (n * CMP_STRIDE + (CMP_BLOCK - 1) <= t_row) & (n < n_cmp)
    o_cmp, psum = [], None
    for hs in heads:
        s = jnp.where(cmask, _dot_nt(ck, q[:, hs].astype(BF16)), NEG)
        p = jnp.where(cmask, jnp.exp(s - jnp.max(s, axis=0, keepdims=True)), 0.0)
        p = p / jnp.maximum(jnp.sum(p, axis=0, keepdims=True), 1e-30)
        o_cmp.append(_dot_tn(p.astype(BF16), cv))
        psum = p if psum is None else psum + p
    p_hi, p_lo = _split(psum)
    imp = _dot(cover_ref[...], p_hi) + _dot(cover_ref[...], p_lo)
    j = _iota(imp.shape, 0)
    own = lax.shift_right_logical(t_row, 6)
    causal = (j * SLC_BLOCK <= t_row) & (j < n_slc)
    forced = (j == 0) | (j == own) | (j == own - 1)
    score = jnp.where(forced, -NEG, jnp.where(causal, imp, NEG))
    sel_t = (_rank_lt_rows(score, n_slc, k_sel) & causal).astype(F32)
    sel = jnp.concatenate([sel_t, jnp.zeros((expand_ref.shape[0] - sel_t.shape[0], tq), F32)], axis=0).T
    sel = sel.astype(BF16)

    qrb = [qr[:, hs].astype(BF16) for hs in heads]
    init = tuple((jnp.full((tq, 1), NEG, F32), jnp.zeros((tq, 1), F32), jnp.zeros((tq, HEAD_DIM), F32))
                 for _ in heads)

    def slc_body(kt, carry):
        rows = slc_ref[0, pl.ds(kt * tq, tq), :]
        k = rows[:, :HEAD_DIM].astype(BF16)
        v = rows[:, HEAD_DIM:].astype(BF16)
        kpos = kt * tq + _iota((1, tq), 1)
        mask = (_dot(sel, expand_ref[:, pl.ds(pl.multiple_of(kt * tq, tq), tq)]) > 0.5) & (kpos <= t_pos)
        return tuple(_flash_step(qrb[h], k, v, mask, *carry[h]) for h in range(N_HEADS))

    def win_body(kt, carry):
        rows = win_ref[0, pl.ds(kt * tq, tq), :]
        k = rows[:, :HEAD_DIM].astype(BF16)
        v = rows[:, HEAD_DIM:].astype(BF16)
        d = t_pos - (kt * tq + _iota((1, tq), 1))
        mask = (d >= 0) & (d <= WINDOW)
        return tuple(_flash_step(qrb[h], k, v, mask, *carry[h]) for h in range(N_HEADS))

    slc = lax.fori_loop(0, qi + 1, slc_body, init)
    n_win = -(-WINDOW // tq)
    win = lax.fori_loop(jnp.maximum(qi - n_win, 0), qi + 1, win_body, init)

    g = _sigmoid(z[:, 640:640 + 3 * N_HEADS])
    outs = []
    for h in range(N_HEADS):
        o_slc = slc[h][2] / jnp.maximum(slc[h][1], 1e-30)
        o_win = win[h][2] / jnp.maximum(win[h][1], 1e-30)
        outs.append(g[:, 3 * h:3 * h + 1] * o_cmp[h] + g[:, 3 * h + 1:3 * h + 2] * o_slc
                    + g[:, 3 * h + 2:3 * h + 3] * o_win)
    o_ref[0] = jnp.concatenate(outs, axis=1).astype(o_ref.dtype)


def nsa_attn(z_nsa, tab, cmp_kv, new_slc, new_win, cover, expand, n_cmp, n_slc):
    b, t, w = z_nsa.shape
    tq = _pick_tile(t, 256)
    kern = functools.partial(_nsa_attn_kernel, tq=tq, n_cmp=n_cmp, n_slc=n_slc, k_sel=min(N_SELECT, n_slc))
    full = lambda i, j: (i, 0, 0)
    const = lambda i, j: (0, 0)
    return pl.pallas_call(
        kern, grid=(b, t // tq),
        in_specs=[pl.BlockSpec((1, tq, w), lambda i, j: (i, j, 0)),
                  pl.BlockSpec((3, tq, LANES), lambda i, j: (0, j, 0)),
                  pl.BlockSpec((1,) + cmp_kv.shape[1:], full),
                  pl.BlockSpec((1, t, LANES), full), pl.BlockSpec((1, t, LANES), full),
                  pl.BlockSpec(cover.shape, const), pl.BlockSpec(expand.shape, const)],
        out_specs=pl.BlockSpec((1, tq, BRANCH_W), lambda i, j: (i, j, 0)),
        out_shape=jax.ShapeDtypeStruct((b, t, BRANCH_W), BF16),
        compiler_params=_params("parallel", "parallel"), name="nsa_attn",
    )(z_nsa, tab, cmp_kv, new_slc, new_win, cover, expand)


def _moba_attn_kernel(z_ref, tab_ref, kv_ref, mean_ref, o_ref, *, tq, nb, topk):
    qi = pl.program_id(1)
    q = z_ref[0][:, :BRANCH_W] * SCALE
    qr = _rope_heads(q, tab_ref[...])
    t_pos = qi * tq + _iota((tq, 1), 0)
    own = lax.shift_right_logical(t_pos, 8)
    means = mean_ref[0]
    heads = [slice(h * HEAD_DIM, (h + 1) * HEAD_DIM) for h in range(N_HEADS)]

    t_row = qi * tq + _iota((1, tq), 1)
    past = (_iota((means.shape[0], 1), 0) < lax.shift_right_logical(t_row, 8))
    jb = _iota((1, LANES), 1)
    sel = []
    for hs in heads:
        gate = _dot3(means[:, hs], qr[:, hs], _dot_nt)
        picked_t = (_rank_lt_rows(jnp.where(past, gate, NEG), nb, topk) & past).astype(F32)
        sel.append(jnp.concatenate([picked_t, jnp.zeros((LANES - picked_t.shape[0], tq), F32)], axis=0).T)

    qrb = [qr[:, hs].astype(BF16) for hs in heads]
    init = tuple((jnp.full((tq, 1), NEG, F32), jnp.zeros((tq, 1), F32), jnp.zeros((tq, HEAD_DIM), F32))
                 for _ in heads)

    def body(kt, carry):
        rows = kv_ref[0, pl.ds(kt * tq, tq), :]
        kpos = kt * tq + _iota((1, tq), 1)
        out = []
        for h, hs in enumerate(heads):
            k = rows[:, hs].astype(BF16)
            v = rows[:, BRANCH_W + h * HEAD_DIM:BRANCH_W + (h + 1) * HEAD_DIM].astype(BF16)
            picked = jnp.sum(jnp.where(jb == kt, sel[h], 0.0), axis=1, keepdims=True) > 0.5
            is_own = own == kt
            mask = (is_own & (kpos <= t_pos)) | (jnp.logical_not(is_own) & picked)
            out.append(_flash_step(qrb[h], k, v, mask, *carry[h]))
        return tuple(out)

    res = lax.fori_loop(0, qi + 1, body, init)
    outs = [res[h][2] / jnp.maximum(res[h][1], 1e-30) for h in range(N_HEADS)]
    o_ref[0] = jnp.concatenate(outs, axis=1).astype(o_ref.dtype)


def moba_attn(z_moba, tab, new_moba, means):
    b, t, w = z_moba.shape
    assert t % MOBA_BLOCK == 0
    tq = MOBA_BLOCK
    nb = t // MOBA_BLOCK
    kern = functools.partial(_moba_attn_kernel, tq=tq, nb=nb, topk=min(MOBA_TOPK, nb))
    full = lambda i, j: (i, 0, 0)
    return pl.pallas_call(
        kern, grid=(b, t // tq),
        in_specs=[pl.BlockSpec((1, tq, w), lambda i, j: (i, j, 0)),
                  pl.BlockSpec((3, tq, LANES), lambda i, j: (0, j, 0)),
                  pl.BlockSpec((1, t, 2 * BRANCH_W), full),
                  pl.BlockSpec((1, nb, BRANCH_W), full)],
        out_specs=pl.BlockSpec((1, tq, BRANCH_W), lambda i, j: (i, j, 0)),
        out_shape=jax.ShapeDtypeStruct((b, t, BRANCH_W), BF16),
        compiler_params=_params("parallel", "parallel"), name="moba_attn",
    )(z_moba, tab, new_moba, means)


def _cross_attn_kernel(q_ref, mem_ref, o_ref):
    q = q_ref[0] * SCALE
    mem = mem_ref[0]
    outs = []
    for h in range(N_HEADS):
        hs = slice(h * HEAD_DIM, (h + 1) * HEAD_DIM)
        s = _dot_nt(q[:, hs].astype(BF16), mem[:, hs].astype(BF16))
        p = jnp.exp(s - jnp.max(s, axis=-1, keepdims=True))
        p = p / jnp.sum(p, axis=-1, keepdims=True)
        v = mem[:, BRANCH_W + h * HEAD_DIM:BRANCH_W + (h + 1) * HEAD_DIM].astype(BF16)
        outs.append(_dot(p.astype(BF16), v))
    o_ref[0] = jnp.concatenate(outs, axis=1).astype(o_ref.dtype)


def cross_attn(q, mem_kv):
    b, t, w = q.shape
    n_mem = mem_kv.shape[1]
    tq = _pick_tile(t, 512)
    return pl.pallas_call(
        _cross_attn_kernel, grid=(b, t // tq),
        in_specs=[pl.BlockSpec((1, tq, w), lambda i, j: (i, j, 0)),
                  pl.BlockSpec((1, n_mem, 2 * BRANCH_W), lambda i, j: (i, 0, 0))],
        out_specs=pl.BlockSpec((1, tq, BRANCH_W), lambda i, j: (i, j, 0)),
        out_shape=jax.ShapeDtypeStruct((b, t, BRANCH_W), BF16),
        compiler_params=_params("parallel", "parallel"), name="cross_attn",
    )(q, mem_kv)


def _head_sum(x, bd_ref):
    return _dot2_exact_rhs(x, bd_ref[...])


def _rwkv_prep(z, zprev, vec_ref, wa_ref, g2_ref, bd_ref):
    mu = vec_ref[0:1, :]
    zm = z + (zprev - z) * mu
    r = zm[:, 0:256]
    k = zm[:, 256:512]
    v = zm[:, 512:768]
    lora = zm[:, 768:896]
    lora = jnp.where(_iota(lora.shape, 1) < 64, jnp.tanh(lora), lora)
    wa = _dot(lora.astype(BF16), wa_ref[...])
    wl = vec_ref[1:2, 0:256] + wa[:, :256]
    x = -wl
    softplus = jnp.maximum(x, 0.0) + jnp.log(1.0 + jnp.exp(-jnp.abs(x)))
    logdecay = -jnp.exp(-softplus - 0.5)
    a = _sigmoid(vec_ref[1:2, 256:512] + wa[:, 256:])
    g = _dot(_sigmoid(zm[:, 896:1152]).astype(BF16), g2_ref[...])
    kkr = k * vec_ref[2:3, 0:256]
    kk = kkr * lax.rsqrt(jnp.maximum(_head_sum(kkr * kkr, bd_ref), 1e-24))
    k2 = k * (1.0 + (a - 1.0) * vec_ref[3:4, 0:256])
    bonus = _head_sum(r * k2 * vec_ref[4:5, 0:256], bd_ref) * v
    return r, k2, v, kk, kk * a, logdecay, g, bonus


def _rwkv_finish(y, bonus, g, vec_ref, bd_ref):
    mu = _head_sum(y, bd_ref) * (1.0 / HEAD_DIM)
    d = y - mu
    var = _head_sum(d * d, bd_ref) * (1.0 / HEAD_DIM)
    yn = d * lax.rsqrt(var + RWKV_GN_EPS) * vec_ref[5:6, 0:256] + vec_ref[6:7, 0:256]
    return (yn + bonus) * g


def _dotc(a, b, dot=_dot):
    return dot(a.astype(BF16), b.astype(BF16))


def _tri_inv_all(lows):
    c = lows[0].shape[0]
    eye = (_iota((c, c), 0) == _iota((c, c), 1)).astype(F32)
    ts = [eye - low for low in lows]
    lps = [_dotc(low, low) for low in lows]
    n = 2
    while n < c:
        ts = [t + _dotc(t, lp) for t, lp in zip(ts, lps)]
        n *= 2
        if n < c:
            lps = [_dotc(lp, lp) for lp in lps]
    return ts


RWKV_GROUP = 2


def _rwkv_seq_kernel(z_ref, shift_ref, s0_ref, vec_ref, wa_ref, g2_ref, bd_ref, tri_ref,
                     o_ref, s_ref, last_ref, r_s, k_s, v_s, kk_s, b_s, ld_s, y_s, pe_s, *, tt):
    ti = pl.program_id(1)

    @pl.when(ti == 0)
    def _():
        last_ref[...] = shift_ref[0]
        s_ref[0] = s0_ref[0]

    z = z_ref[0]
    rolled = pltpu.roll(z, 1, 0)
    zprev = jnp.where(_iota(z.shape, 0) == 0, last_ref[...], rolled)
    last_ref[...] = z[tt - 1:tt, :]
    r, k2, v, kk, b, ld, g, bonus = _rwkv_prep(z, zprev, vec_ref, wa_ref, g2_ref, bd_ref)
    r_s[...] = r
    k_s[...] = k2
    v_s[...] = v
    kk_s[...] = kk
    b_s[...] = b
    ld_s[...] = ld

    c = RWKV_CHUNK
    ri = _iota((c, c), 0)
    ci = _iota((c, c), 1)
    strict = ri > ci
    incl = ri >= ci

    eye = (ri == ci).astype(F32)
    heads = [slice(h * HEAD_DIM, (h + 1) * HEAD_DIM) for h in range(N_HEADS)]
    n_chunks = tt // c
    group = RWKV_GROUP if n_chunks % RWKV_GROUP == 0 else 1

    def prepare(gi, carry):
        probs = []
        for j in range(group):
            ck = gi * group + j
            rows = pl.ds(pl.multiple_of(ck * c, c), c)
            ldc = ld_s[rows, :]
            cum = _dot3(tri_ref[...], ldc)
            p_in = jnp.exp(cum)
            inv_p = jnp.exp(-cum)
            kkt = kk_s[rows, :] * jnp.exp(cum - ldc)
            bt = b_s[rows, :] * inv_p
            kt = k_s[rows, :] * inv_p
            rt = r_s[rows, :] * p_in
            vv = v_s[rows, :]
            p_end = p_in[c - 1:c, :]
            pe_s[pl.ds(ck, 1), :] = p_end
            for hs in heads:
                probs.append(dict(kkt=kkt[:, hs], bt=bt[:, hs], kt=kt[:, hs], rt=rt[:, hs], v=vv[:, hs],
                                  p_end=p_end[:, hs]))
        blocks = [_dotc(jnp.concatenate([p['kkt'], p['rt']], axis=0),
                        jnp.concatenate([p['bt'], p['kt']], axis=0), _dot_nt) for p in probs]
        lows = [jnp.where(strict, bl[:c, :c], 0.0) for bl in blocks]
        gvs = [_dotc(jnp.where(strict, bl[:c, c:], 0.0), p['v']) for bl, p in zip(blocks, probs)]
        tinvs = _tri_inv_all(lows)
        ke_ue = [_dotc(t, jnp.concatenate([p['kkt'], gv], axis=1)) for t, p, gv in zip(tinvs, probs, gvs)]
        corr = [_dotc(jnp.where(incl, bl[c:, :c], 0.0), ku) for bl, ku in zip(blocks, ke_ue)]
        avs = [_dotc(jnp.where(incl, bl[c:, c:], 0.0), p['v']) for bl, p in zip(blocks, probs)]
        xps = [_dotc(ku[:, :HEAD_DIM], p['bt'], _dot_tn) * p['p_end'] for ku, p in zip(ke_ue, probs)]
        bcs = [_dotc(jnp.concatenate([p['v'], ku[:, HEAD_DIM:]], axis=0),
                     jnp.concatenate([p['kt'], -p['bt']], axis=0), _dot_tn) * p['p_end']
               for ku, p in zip(ke_ue, probs)]
        res = [p['rt'] - cr[:, :HEAD_DIM] for p, cr in zip(probs, corr)]
        y0s = [av - cr[:, HEAD_DIM:] for av, cr in zip(avs, corr)]
        for j in range(group):
            rows = pl.ds(pl.multiple_of((gi * group + j) * c, c), c)
            sl = slice(j * N_HEADS, (j + 1) * N_HEADS)
            r_s[rows, :] = jnp.concatenate(res[sl], axis=1)
            v_s[rows, :] = jnp.concatenate(y0s[sl], axis=1)
            kk_s[rows, :] = jnp.concatenate(xps[sl], axis=1)
            b_s[rows, :] = jnp.concatenate(bcs[sl], axis=1)
        return carry

    lax.fori_loop(0, n_chunks // group, prepare, 0)

    def advance(ck, carry):
        rows = pl.ds(pl.multiple_of(ck * c, c), c)
        re, y0, xp, bc = r_s[rows, :], v_s[rows, :], kk_s[rows, :], b_s[rows, :]
        p_end = pe_s[pl.ds(ck, 1), :]
        states = [s_ref[0, h] for h in range(N_HEADS)]
        ys = [_dotc(re[:, hs], s_h, _dot_nt) + y0[:, hs] for hs, s_h in zip(heads, states)]
        for h, (hs, s_h) in enumerate(zip(heads, states)):
            s_ref[0, h] = s_h * p_end[:, hs] - _dotc(s_h, xp[:, hs]) + bc[:, hs]
        y_s[rows, :] = jnp.concatenate(ys, axis=1)
        return carry

    lax.fori_loop(0, n_chunks, advance, 0)
    o_ref[0] = _rwkv_finish(y_s[...], bonus, g, vec_ref, bd_ref).astype(o_ref.dtype)


def rwkv_seq(z, shift, s0, vec, wa, g2, bd, tri):
    b, t, w = z.shape
    tt = _pick_tile(t, 512, RWKV_CHUNK)
    assert tt % RWKV_CHUNK == 0
    const = lambda i, j: (0, 0)
    scr = pltpu.VMEM((tt, BRANCH_W), F32)
    return pl.pallas_call(
        functools.partial(_rwkv_seq_kernel, tt=tt), grid=(b, t // tt),
        in_specs=[pl.BlockSpec((1, tt, w), lambda i, j: (i, j, 0)),
                  pl.BlockSpec((1, 1, w), lambda i, j: (i, 0, 0)),
                  pl.BlockSpec((1, N_HEADS, HEAD_DIM, HEAD_DIM), lambda i, j: (i, 0, 0, 0)),
                  pl.BlockSpec(vec.shape, const), pl.BlockSpec(wa.shape, const),
                  pl.BlockSpec(g2.shape, const), pl.BlockSpec(bd.shape, const),
                  pl.BlockSpec(tri.shape, const)],
        out_specs=[pl.BlockSpec((1, tt, BRANCH_W), lambda i, j: (i, j, 0)),
                   pl.BlockSpec((1, N_HEADS, HEAD_DIM, HEAD_DIM), lambda i, j: (i, 0, 0, 0))],
        out_shape=[jax.ShapeDtypeStruct((b, t, BRANCH_W), BF16),
                   jax.ShapeDtypeStruct((b, N_HEADS, HEAD_DIM, HEAD_DIM), F32)],
        scratch_shapes=[pltpu.VMEM((1, w), F32)] + [scr] * 7 + [pltpu.VMEM((tt // RWKV_CHUNK, BRANCH_W), F32)],
        compiler_params=_params("parallel", "arbitrary"), name="rwkv_seq",
    )(z, shift, s0, vec, wa, g2, bd, tri)


def _rwkv_step_kernel(z_ref, shift_ref, s0_ref, vec_ref, wa_ref, g2_ref, bd_ref, o_ref, s_ref):
    rows = 8
    z = jnp.broadcast_to(z_ref[0], (rows, z_ref.shape[2]))
    zprev = jnp.broadcast_to(shift_ref[0], z.shape)
    r, k2, v, kk, b, ld, g, bonus = _rwkv_prep(z, zprev, vec_ref, wa_ref, g2_ref, bd_ref)
    decay = jnp.exp(ld)
    eye = _iota((HEAD_DIM, HEAD_DIM), 0) == _iota((HEAD_DIM, HEAD_DIM), 1)
    ys = []
    for h in range(N_HEADS):
        hs = slice(h * HEAD_DIM, (h + 1) * HEAD_DIM)
        s_h = s0_ref[0, h]
        sk = jnp.sum(s_h * kk[0:1, hs], axis=1, keepdims=True)
        v_col = jnp.sum(jnp.where(eye, v[0:1, hs], 0.0), axis=1, keepdims=True)
        s_new = s_h * decay[0:1, hs] - sk * b[0:1, hs] + v_col * k2[0:1, hs]
        s_ref[0, h] = s_new
        y_col = jnp.sum(s_new * r[0:1, hs], axis=1, keepdims=True)
        ys.append(jnp.sum(jnp.where(eye, y_col, 0.0), axis=0, keepdims=True))
    y = jnp.broadcast_to(jnp.concatenate(ys, axis=1), (rows, BRANCH_W))
    o_ref[0] = _rwkv_finish(y, bonus, g, vec_ref, bd_ref)[0:1].astype(o_ref.dtype)


def rwkv_step(z, shift, s0, vec, wa, g2, bd):
    b, _, w = z.shape
    const = lambda i: (0, 0)
    return pl.pallas_call(
        _rwkv_step_kernel, grid=(b,),
        in_specs=[pl.BlockSpec((1, 1, w), lambda i: (i, 0, 0)),
                  pl.BlockSpec((1, 1, w), lambda i: (i, 0, 0)),
                  pl.BlockSpec((1, N_HEADS, HEAD_DIM, HEAD_DIM), lambda i: (i, 0, 0, 0)),
                  pl.BlockSpec(vec.shape, const), pl.BlockSpec(wa.shape, const),
                  pl.BlockSpec(g2.shape, const), pl.BlockSpec(bd.shape, const)],
        out_specs=[pl.BlockSpec((1, 1, BRANCH_W), lambda i: (i, 0, 0)),
                   pl.BlockSpec((1, N_HEADS, HEAD_DIM, HEAD_DIM), lambda i: (i, 0, 0, 0))],
        out_shape=[jax.ShapeDtypeStruct((b, 1, BRANCH_W), BF16),
                   jax.ShapeDtypeStruct((b, N_HEADS, HEAD_DIM, HEAD_DIM), F32)],
        compiler_params=_params("parallel"), name="rwkv_step",
    )(z, shift, s0, vec, wa, g2, bd)


def _stack_heads(x):
    rows = [x[:, h * HEAD_DIM:(h + 1) * HEAD_DIM] for h in range(N_HEADS)]
    return jnp.concatenate(rows + [jnp.zeros((8 - N_HEADS, HEAD_DIM), x.dtype)], axis=0)


def _unstack_heads(x):
    return jnp.concatenate([x[h:h + 1, :] for h in range(N_HEADS)], axis=1)


def _column(row):
    n = row.shape[1]
    eye = _iota((n, n), 0) == _iota((n, n), 1)
    return jnp.sum(jnp.where(eye, row, 0.0), axis=1, keepdims=True)


def _to_row(col):
    n = col.shape[0]
    eye = _iota((n, n), 0) == _iota((n, n), 1)
    return jnp.sum(jnp.where(eye, col, 0.0), axis=0, keepdims=True)


def _page_scores(k_page, q_cols, s_ref, n_pages):
    def body(p, c):
        for h, qc in enumerate(q_cols):
            s_ref[h, pl.ds(p, 1), :] = jnp.sum(k_page(p, h) * qc, axis=0, keepdims=True)
        return c

    lax.fori_loop(0, n_pages, body, 0, unroll=4)


def _page_values(v_page, p_ref, n_pages):
    def body(p, accs):
        return tuple(acc + v_page(p, h) * p_ref[h, pl.ds(p, 1), :] for h, acc in enumerate(accs))

    init = tuple(jnp.zeros((HEAD_DIM, p_ref.shape[2]), F32) for _ in range(N_HEADS))
    return [jnp.sum(a, axis=1, keepdims=True) for a in lax.fori_loop(0, n_pages, body, init, unroll=4)]


def _softmax_with_new(s, mask, s_new):
    s = jnp.where(mask, s, NEG)
    m = jnp.maximum(jnp.max(jnp.max(s, axis=1, keepdims=True), axis=0, keepdims=True), s_new)
    p = jnp.where(mask, jnp.exp(s - m), 0.0)
    p_new = jnp.exp(s_new - m)
    den = jnp.maximum(jnp.sum(jnp.sum(p, axis=1, keepdims=True), axis=0, keepdims=True) + p_new, 1e-30)
    return p / den, p_new / den


def _nsa_sample_kernel(pt_ref, z_ref, tab_ref, win_ref, w1_ref, pe_ref, w2_ref, cover_ref, pj_ref, hr_ref,
                       cmp_hbm, slc_hbm, o_ref, slc_row_ref, win_row_ref,
                       cbuf, sbuf, rows_s, lines_s, s_s, p_s, sem,
                       *, base, n_pages, page, past, n_cmp, n_slc, k_sel):
    bi = pl.program_id(0)
    n_lines = past // CMP_STRIDE
    slot = lax.rem(bi, 2)

    def copies(seq, sl, p):
        pg = pt_ref[seq * n_pages + p] + base
        return (pltpu.make_async_copy(cmp_hbm.at[pg], cbuf.at[sl, p], sem.at[sl, 0]),
                pltpu.make_async_copy(slc_hbm.at[pg], sbuf.at[sl, p], sem.at[sl, 1]))

    def start_fetch(seq, sl):
        for p in range(n_pages):
            for cp in copies(seq, sl, p):
                cp.start()

    @pl.when(bi == 0)
    def _():
        start_fetch(0, 0)

    @pl.when(bi + 1 < pl.num_programs(0))
    def _():
        start_fetch(bi + 1, 1 - slot)

    z = z_ref[0]
    tab = tab_ref[...]
    q = z[:, :BRANCH_W] * SCALE
    q4 = _stack_heads(q)
    q4r_f32 = _stack_heads(_rope_heads(q, tab))
    q4r = q4r_f32.astype(BF16)
    slc_row = _rope(z[:, 384:512], tab, half_only=True)
    win_row = _rope(z[:, 512:640], tab, half_only=True)
    slc_row_ref[0] = slc_row
    win_row_ref[0] = win_row

    for p in range(n_pages):
        for cp in copies(bi, slot, p):
            cp.wait()

    def xpose(p, c):
        rows_s[pl.ds(pl.multiple_of(p * page, page), page), :] = cbuf[slot, p].T
        return c

    lax.fori_loop(0, n_pages, xpose, 0, unroll=8)
    for l in range(CMP_STRIDE):
        lines_s[:, l * LANES:(l + 1) * LANES] = rows_s[pl.ds(l, n_lines, stride=CMP_STRIDE), :]

    cmpkv = _compress(lines_s[...], w1_ref, pe_ref, w2_ref)
    ck = cmpkv[:, :HEAD_DIM].astype(BF16)
    cv = cmpkv[:, HEAD_DIM:].astype(BF16)
    n = _iota((1, cmpkv.shape[0]), 1)
    cmask = (n * CMP_STRIDE + (CMP_BLOCK - 1) <= past) & (n < n_cmp)
    p_cmp = _msoftmax(_dot_nt(q4.astype(BF16), ck), cmask)
    o_cmp = _dot(p_cmp.astype(BF16), cv)
    psum = jnp.sum(jnp.where(_iota(p_cmp.shape, 0) < N_HEADS, p_cmp, 0.0), axis=0, keepdims=True)
    imp = _dot2_exact_rhs(jnp.broadcast_to(psum, p_cmp.shape), cover_ref[...])[0:1]
    t_pos = jnp.full((1, 1), past, jnp.int32)
    sel = _select_blocks(imp, t_pos, n_slc, k_sel)

    def with_new_key(s_past, mask, new_row, v_past):
        s_new = jnp.sum(q4r_f32 * new_row[:, :HEAD_DIM], axis=1, keepdims=True)
        s_past = jnp.where(mask, s_past, NEG)
        m = jnp.maximum(jnp.max(s_past, axis=1, keepdims=True), s_new)
        p_past = jnp.where(mask, jnp.exp(s_past - m), 0.0)
        p_new = jnp.exp(s_new - m)
        den = jnp.maximum(jnp.sum(p_past, axis=1, keepdims=True) + p_new, 1e-30)
        return _dot((p_past / den).astype(BF16), v_past) + (p_new / den) * new_row[:, HEAD_DIM:]

    sel_pages = _dot((pj_ref[...] * sel.astype(F32)).astype(BF16), hr_ref[...]) > 0.5
    _page_scores(lambda p, h: sbuf[slot, p, 0], [_column(q4r_f32[h:h + 1, :]) for h in range(N_HEADS)], s_s,
                 n_pages)
    p_new = []
    for h in range(N_HEADS):
        s_new = jnp.sum(q4r_f32[h:h + 1, :] * slc_row[:, :HEAD_DIM], axis=1, keepdims=True)
        p_s[h], pn = _softmax_with_new(s_s[h], sel_pages, s_new)
        p_new.append(pn)
    o_slc = [_to_row(col) + p_new[h] * slc_row[:, HEAD_DIM:]
             for h, col in enumerate(_page_values(lambda p, h: sbuf[slot, p, 1], p_s, n_pages))]
    wrows = win_ref[0]
    wk = wrows[:, :HEAD_DIM].astype(BF16)
    wv = wrows[:, HEAD_DIM:].astype(BF16)
    o_win = with_new_key(_dot_nt(q4r, wk), jnp.full((1, wrows.shape[0]), True), win_row, wv)

    g = _sigmoid(z[:, 640:640 + 3 * N_HEADS])
    outs = []
    for h in range(N_HEADS):
        outs.append(g[:, 3 * h:3 * h + 1] * o_cmp[h:h + 1] + g[:, 3 * h + 1:3 * h + 2] * o_slc[h]
                    + g[:, 3 * h + 2:3 * h + 3] * o_win[h:h + 1])
    o_ref[0] = jnp.concatenate(outs, axis=1).astype(o_ref.dtype)


def nsa_sample(page_table, z_nsa, tab, win, w1c, pe2, w2c, cover, pj, hr, cache_cmp, cache_slc,
               layer, n_pool, page, n_cmp, n_slc):
    b = z_nsa.shape[0]
    n_pages = page_table.shape[1]
    past = n_pages * page
    kern = functools.partial(_nsa_sample_kernel, base=layer * n_pool, n_pages=n_pages, page=page, past=past,
                             n_cmp=n_cmp, n_slc=n_slc, k_sel=min(N_SELECT, n_slc))
    const = lambda i, pt: (0, 0)
    row = lambda i, pt: (i, 0, 0)
    grid_spec = pltpu.PrefetchScalarGridSpec(
        num_scalar_prefetch=1, grid=(b,),
        in_specs=[pl.BlockSpec((1, 1, z_nsa.shape[2]), row),
                  pl.BlockSpec(tab.shape, lambda i, pt: (0, 0, 0)),
                  pl.BlockSpec((1,) + win.shape[1:], row),
                  pl.BlockSpec(w1c.shape, const), pl.BlockSpec(pe2.shape, const),
                  pl.BlockSpec(w2c.shape, const), pl.BlockSpec(cover.shape, const),
                  pl.BlockSpec(pj.shape, const), pl.BlockSpec(hr.shape, const),
                  pl.BlockSpec(memory_space=pl.ANY), pl.BlockSpec(memory_space=pl.ANY)],
        out_specs=[pl.BlockSpec((1, 1, BRANCH_W), row), pl.BlockSpec((1, 1, LANES), row),
                   pl.BlockSpec((1, 1, LANES), row)],
        scratch_shapes=[pltpu.VMEM((2, n_pages, 2 * HEAD_DIM, page), F32),
                        pltpu.VMEM((2, n_pages, 2, HEAD_DIM, page), F32),
                        pltpu.VMEM((past, 2 * HEAD_DIM), F32),
                        pltpu.VMEM((past // CMP_STRIDE, CMP_STRIDE * LANES), F32),
                        pltpu.VMEM((N_HEADS, n_pages, page), F32),
                        pltpu.VMEM((N_HEADS, n_pages, page), F32),
                        pltpu.SemaphoreType.DMA((2, 2))])
    return pl.pallas_call(
        kern, grid_spec=grid_spec,
        out_shape=[jax.ShapeDtypeStruct((b, 1, BRANCH_W), BF16),
                   jax.ShapeDtypeStruct((b, 1, LANES), F32), jax.ShapeDtypeStruct((b, 1, LANES), F32)],
        compiler_params=_params("arbitrary"), name="nsa_sample",
    )(page_table.reshape(-1), z_nsa, tab, win, w1c, pe2, w2c, cover, pj, hr, cache_cmp, cache_slc)


def _moba_sample_kernel(pt_ref, z_ref, tab_ref, kv_hbm, o_ref, row_ref, buf, s_s, p_s, sem,
                        *, base, n_pages, pages_per_block, topk):
    bi = pl.program_id(0)
    slot = lax.rem(bi, 2)

    def copy(seq, sl, p):
        pg = pt_ref[seq * n_pages + p] + base
        return pltpu.make_async_copy(kv_hbm.at[pg], buf.at[sl, p], sem.at[sl])

    @pl.when(bi == 0)
    def _():
        for p in range(n_pages):
            copy(0, 0, p).start()

    @pl.when(bi + 1 < pl.num_programs(0))
    def _():
        for p in range(n_pages):
            copy(bi + 1, 1 - slot, p).start()

    z = z_ref[0]
    tab = tab_ref[...]
    q = z[:, :BRANCH_W] * SCALE
    qr = _rope_heads(q, tab)
    k_new = _rope_heads(z[:, BRANCH_W:2 * BRANCH_W], tab)
    v_new = z[:, 2 * BRANCH_W:3 * BRANCH_W]
    row_ref[0] = jnp.concatenate([k_new, v_new], axis=1)

    for p in range(n_pages):
        copy(bi, slot, p).wait()

    heads = [slice(h * HEAD_DIM, (h + 1) * HEAD_DIM) for h in range(N_HEADS)]
    _page_scores(lambda p, h: buf[slot, p, h], [_column(qr[:, hs]) for hs in heads], s_s, n_pages)
    pi = _iota((n_pages, n_pages), 0)
    pj = _iota((n_pages, n_pages), 1)
    shift = pages_per_block.bit_length() - 1
    same_block = lax.shift_right_logical(pi, shift) == lax.shift_right_logical(pj, shift)
    p_new = []
    for h, hs in enumerate(heads):
        s = s_s[h]
        page_sum = jnp.sum(s, axis=1, keepdims=True)
        page_sum_row = jnp.sum(jnp.where(pi == pj, page_sum, 0.0), axis=0, keepdims=True)
        gate_col = jnp.sum(jnp.where(same_block, page_sum_row, 0.0), axis=1, keepdims=True)
        gate_row = jnp.sum(jnp.where(same_block, page_sum, 0.0), axis=0, keepdims=True)
        beats = ((gate_row > gate_col) | ((gate_row == gate_col) & (pj < pi))) & jnp.logical_not(same_block)
        picked = jnp.sum(beats.astype(F32), axis=1, keepdims=True) < pages_per_block * topk - 0.5
        s_new = jnp.sum(qr[:, hs] * k_new[:, hs], axis=1, keepdims=True)
        p_s[h], pn = _softmax_with_new(s, picked, s_new)
        p_new.append(pn)
    cols = _page_values(lambda p, h: buf[slot, p, N_HEADS + h], p_s, n_pages)
    outs = [_to_row(cols[h]) + p_new[h] * v_new[:, hs] for h, hs in enumerate(heads)]
    o_ref[0] = jnp.concatenate(outs, axis=1).astype(o_ref.dtype)


def moba_sample(page_table, z_moba, tab, cache, layer, n_pool, page):
    b = z_moba.shape[0]
    n_pages = page_table.shape[1]
    past = n_pages * page
    assert past % MOBA_BLOCK == 0 and MOBA_BLOCK % page == 0
    ppb = MOBA_BLOCK // page
    assert ppb & (ppb - 1) == 0
    kern = functools.partial(_moba_sample_kernel, base=layer * n_pool, n_pages=n_pages, pages_per_block=ppb,
                             topk=min(MOBA_TOPK, past // MOBA_BLOCK + 1))
    row = lambda i, pt: (i, 0, 0)
    grid_spec = pltpu.PrefetchScalarGridSpec(
        num_scalar_prefetch=1, grid=(b,),
        in_specs=[pl.BlockSpec((1, 1, z_moba.shape[2]), row),
                  pl.BlockSpec(tab.shape, lambda i, pt: (0, 0, 0)),
                  pl.BlockSpec(memory_space=pl.ANY)],
        out_specs=[pl.BlockSpec((1, 1, BRANCH_W), row), pl.BlockSpec((1, 1, 2 * BRANCH_W), row)],
        scratch_shapes=[pltpu.VMEM((2, n_pages, 2 * N_HEADS, HEAD_DIM, page), F32),
                        pltpu.VMEM((N_HEADS, n_pages, page), F32),
                        pltpu.VMEM((N_HEADS, n_pages, page), F32),
                        pltpu.SemaphoreType.DMA((2,))])
    return pl.pallas_call(
        kern, grid_spec=grid_spec,
        out_shape=[jax.ShapeDtypeStruct((b, 1, BRANCH_W), BF16),
                   jax.ShapeDtypeStruct((b, 1, 2 * BRANCH_W), F32)],
        compiler_params=_params("arbitrary"), name="moba_sample",
    )(page_table.reshape(-1), z_moba, tab, cache)


def _rope_table(pos):
    inv = ROPE_THETA ** (-jnp.arange(ROT_HALF, dtype=F32) / ROT_HALF)
    ang = pos.astype(F32)[:, None] * inv[None, :]
    cos, sin = jnp.cos(ang), jnp.sin(ang)
    t = pos.shape[0]
    ones = jnp.ones((t, HEAD_DIM - 2 * ROT_HALF), F32)
    zeros = jnp.zeros((t, HEAD_DIM - 2 * ROT_HALF), F32)
    z8 = jnp.zeros((t, ROT_HALF), F32)
    c = jnp.concatenate([cos, cos, ones], axis=1)
    sa = jnp.concatenate([-sin, z8, zeros], axis=1)
    sb = jnp.concatenate([z8, sin, zeros], axis=1)
    return jnp.stack([jnp.tile(c, (1, 2)), jnp.tile(sa, (1, 2)), jnp.tile(sb, (1, 2))], axis=0)


def _cover_matrix(n_rows, n_cmp, n_slc, width):
    starts = np.arange(n_rows) * CMP_STRIDE
    blk = np.arange(width) * SLC_BLOCK
    cover = (starts[:, None] < blk[None, :] + SLC_BLOCK) & (starts[:, None] + CMP_BLOCK > blk[None, :])
    cover &= (np.arange(n_rows)[:, None] < n_cmp) & (np.arange(width)[None, :] < n_slc)
    return jnp.asarray(cover, BF16)


def _expand_matrix(n_blocks, block, n_keys):
    e = np.arange(n_blocks)[:, None] == (np.arange(n_keys)[None, :] // block)
    return jnp.asarray(e, BF16)


def _round_up(n, m):
    return -(-n // m) * m


COL_NSA, COL_MOBA, COL_CROSS, COL_RWKV, COL_MERGE = 0, 768, 1536, 2304, 4096


def _pack_w_in(w_in):
    def zeros(n):
        return jnp.zeros(w_in.shape[:2] + (n,), w_in.dtype)

    parts = [w_in[..., 0:652], zeros(NSA_PAD - 652), w_in[..., 1708:2476], w_in[..., 2476:2732],
             zeros(COL_RWKV - COL_CROSS - BRANCH_W), w_in[..., 652:1708], zeros(RWKV_PAD - RWKV_IN),
             zeros(COL_MERGE - COL_RWKV - RWKV_PAD), w_in[..., 2732:]]
    return jnp.concatenate(parts, axis=-1).astype(BF16)


def _layer_weights(l, nsa_pe, nsa_phi_w1, nsa_phi_w2, rwkv_mu, rwkv_w0, rwkv_w2, rwkv_a0, rwkv_a2,
                   rwkv_g2, rwkv_k_k, rwkv_k_a, rwkv_r_k, rwkv_ln_w, rwkv_ln_b):
    w = {}
    w1 = nsa_phi_w1[l]
    hidden = w1.shape[2]
    w1r = w1.reshape(2, 2, CMP_STRIDE, HEAD_DIM, hidden)
    w1c = jnp.zeros((2, CMP_STRIDE, 2, HEAD_DIM, 2, hidden), F32)
    w1c = w1c.at[:, :, 0, :, 0, :].set(w1r[0]).at[:, :, 1, :, 1, :].set(w1r[1])
    w1c = w1c.reshape(2, CMP_STRIDE * 2 * HEAD_DIM, 2 * hidden)
    w['w1c'] = jnp.concatenate([w1c[0], w1c[1]], axis=1).astype(BF16)
    pe = nsa_pe[l].reshape(2, CMP_STRIDE * 2 * HEAD_DIM)
    w['pe2'] = jnp.pad(pe, ((0, 6), (0, 0)))
    w2 = nsa_phi_w2[l]
    w2c = jnp.zeros((2, hidden, 2, HEAD_DIM), F32).at[0, :, 0, :].set(w2[0]).at[1, :, 1, :].set(w2[1])
    w['w2c'] = w2c.reshape(2 * hidden, 2 * HEAD_DIM).astype(BF16)
    vec = jnp.zeros((8, RWKV_PAD), F32)
    vec = vec.at[0, :RWKV_IN].set(rwkv_mu[l])
    vec = vec.at[1, 0:256].set(rwkv_w0[l]).at[1, 256:512].set(rwkv_a0[l])
    for i, p in enumerate((rwkv_k_k, rwkv_k_a, rwkv_r_k, rwkv_ln_w, rwkv_ln_b)):
        vec = vec.at[2 + i, 0:256].set(p[l])
    w['rwkv_vec'] = vec
    wa = jnp.zeros((128, 512), F32).at[0:64, 0:256].set(rwkv_w2[l]).at[64:128, 256:512].set(rwkv_a2[l])
    w['rwkv_wa'] = wa.astype(BF16)
    w['rwkv_g2'] = jnp.pad(rwkv_g2[l], ((0, 256 - rwkv_g2.shape[1]), (0, 0))).astype(BF16)
    return w


def _project(h, w_all, l, d):
    z_nsa = mm(h, w_all, l, COL_NSA, NSA_PAD)
    z_rwkv = mm(h, w_all, l, COL_RWKV, RWKV_PAD)
    z_moba = mm(h, w_all, l, COL_MOBA, 3 * BRANCH_W)
    z_cross = mm(h, w_all, l, COL_CROSS, BRANCH_W)
    gates = mm(h, w_all, l, COL_MERGE, 4 * d, out_dtype=BF16, gate=True)
    return z_nsa, z_rwkv, z_moba, z_cross, gates


def _mixers_prompt(h, w_all, l, w, tab, mem_kv, consts, b, t):
    z_nsa, z_rwkv, z_moba, z_cross, gates = _project(h, w_all, l, h.shape[1])
    z_nsa = z_nsa.reshape(b, t, NSA_PAD)
    z_rwkv = z_rwkv.reshape(b, t, RWKV_PAD)
    z_moba = z_moba.reshape(b, t, 3 * BRANCH_W)
    z_cross = z_cross.reshape(b, t, BRANCH_W)

    new_cmp = z_nsa[:, :, 256:384]
    new_slc, new_win = nsa_rope(z_nsa, tab)
    cmp_kv = compress(new_cmp.reshape(b, t // CMP_STRIDE, CMP_STRIDE * LANES), w['w1c'], w['pe2'], w['w2c'])
    o_nsa = nsa_attn(z_nsa, tab, cmp_kv, new_slc, new_win, consts['cover_p'], consts['expand_p'],
                     consts['n_cmp_p'], consts['n_slc_p'])

    zero_shift = jnp.zeros((b, 1, RWKV_PAD), F32)
    zero_state = jnp.zeros((b, N_HEADS, HEAD_DIM, HEAD_DIM), F32)
    o_rwkv, new_state = rwkv_seq(z_rwkv, zero_shift, zero_state, w['rwkv_vec'], w['rwkv_wa'], w['rwkv_g2'],
                                 consts['bd'], consts['tri'])
    new_shift = z_rwkv[:, t - 1, :RWKV_IN]

    new_moba, means = moba_rope(z_moba, tab)
    o_moba = moba_attn(z_moba, tab, new_moba, means.reshape(b, -1, BRANCH_W))
    o_cross = cross_attn(z_cross, mem_kv)

    outs = [o.reshape(b * t, BRANCH_W) for o in (o_nsa, o_rwkv, o_moba, o_cross)]
    win_keep = min(WINDOW, t)
    caches = (new_cmp.reshape(b, t, 2, HEAD_DIM), new_slc.reshape(b, t, 2, HEAD_DIM),
              new_win[:, t - win_keep:].reshape(b, win_keep, 2, HEAD_DIM),
              new_moba.reshape(b, t, 2, N_HEADS, HEAD_DIM), new_state, new_shift)
    return outs, gates, caches


def _mixers_sample(h, w_all, l, w, tab, page_table, caches_in, consts, b):
    cmp_pages, slc_pages, cache_win, moba_pages, cache_mem, state_rwkv, state_shift, n_pool, page = caches_in
    z_nsa, z_rwkv, z_moba, z_cross, gates = _project(h, w_all, l, h.shape[1])
    z_nsa = z_nsa.reshape(b, 1, NSA_PAD)
    z_rwkv = z_rwkv.reshape(b, 1, RWKV_PAD)
    z_moba = z_moba.reshape(b, 1, 3 * BRANCH_W)
    z_cross = z_cross.reshape(b, 1, BRANCH_W)

    win = cache_win[l].reshape(b, -1, LANES)
    o_nsa, slc_row, win_row = nsa_sample(
        page_table, z_nsa, tab, win, w['w1c'], w['pe2'], w['w2c'], consts['cover_s'], consts['pj_s'],
        consts['hr_s'], cmp_pages, slc_pages, l, n_pool, page, consts['n_cmp_s'], consts['n_slc_s'])
    new_cmp = z_nsa[:, :, 256:384]
    full_win = jnp.concatenate([win, win_row], axis=1)
    keep = min(WINDOW, full_win.shape[1])
    new_buf = full_win[:, full_win.shape[1] - keep:]

    shift = jnp.pad(state_shift[l], ((0, 0), (0, RWKV_PAD - RWKV_IN))).reshape(b, 1, RWKV_PAD)
    o_rwkv, new_state = rwkv_step(z_rwkv, shift, state_rwkv[l], w['rwkv_vec'], w['rwkv_wa'], w['rwkv_g2'],
                                  consts['bd'])
    new_shift = z_rwkv[:, 0, :RWKV_IN]

    o_moba, moba_row = moba_sample(page_table, z_moba, tab, moba_pages, l, n_pool, page)
    o_cross = cross_attn(z_cross, cache_mem[l].reshape(b, -1, 2 * BRANCH_W))

    outs = [o.reshape(b, BRANCH_W) for o in (o_nsa, o_rwkv, o_moba, o_cross)]
    caches = (new_cmp.reshape(b, 1, 2, HEAD_DIM), slc_row.reshape(b, 1, 2, HEAD_DIM),
              new_buf.reshape(b, keep, 2, HEAD_DIM), moba_row.reshape(b, 1, 2, N_HEADS, HEAD_DIM),
              new_state, new_shift)
    return outs, gates, caches


def kernel(x_prompt, x_sample, cache_nsa_cmp, cache_nsa_slc, cache_nsa_win, cache_moba, cache_mem, state_rwkv, state_shift, page_table, mem_prompt, g_pre_mix, g_post_mix, g_pre_ffn, g_post_ffn, g_mem, w_in, nsa_pe, nsa_phi_w1, nsa_phi_w2, rwkv_mu, rwkv_w0, rwkv_w2, rwkv_a0, rwkv_a2, rwkv_g2, rwkv_k_k, rwkv_k_a, rwkv_r_k, rwkv_ln_w, rwkv_ln_b, w_mem_kv, w_branch, w_out, w_ffn_gate, w_ffn_up, w_ffn_down):
    bp, t, d = x_prompt.shape
    bs = x_sample.shape[0]
    assert x_sample.shape[1] == 1
    depth = w_in.shape[0]
    page = cache_nsa_cmp.shape[2]
    past = page_table.shape[1] * page
    n_mem = mem_prompt.shape[1]
    assert t % MOBA_BLOCK == 0 and past % MOBA_BLOCK == 0 and page % SLC_BLOCK == 0

    n_cmp_p = (t - CMP_BLOCK) // CMP_STRIDE + 1
    n_slc_p = -(-t // SLC_BLOCK)
    n_cmp_s = (past + 1 - CMP_BLOCK) // CMP_STRIDE + 1
    n_slc_s = -(-(past + 1) // SLC_BLOCK)
    consts = {
        'n_cmp_p': n_cmp_p, 'n_slc_p': n_slc_p, 'n_cmp_s': n_cmp_s, 'n_slc_s': n_slc_s,
        'cover_p': _cover_matrix(t // CMP_STRIDE, n_cmp_p, n_slc_p, _round_up(n_slc_p, 8)).T,
        'expand_p': _expand_matrix(_round_up(n_slc_p, LANES), SLC_BLOCK, t),
        'cover_s': _cover_matrix(past // CMP_STRIDE, n_cmp_s, n_slc_s, _round_up(n_slc_s, LANES)),
        'pj_s': jnp.asarray(np.arange(_round_up(n_slc_s, LANES))[None, :] // (page // SLC_BLOCK)
                            == np.arange(past // page)[:, None], F32),
        'hr_s': jnp.asarray(np.arange(_round_up(n_slc_s, LANES))[:, None] % (page // SLC_BLOCK)
                            == np.arange(page)[None, :] // SLC_BLOCK, BF16),
        'bd': jnp.asarray(np.arange(BRANCH_W)[:, None] // HEAD_DIM == np.arange(BRANCH_W)[None, :] // HEAD_DIM, BF16),
        'tri': jnp.asarray(np.tril(np.ones((RWKV_CHUNK, RWKV_CHUNK), np.float32))),
    }
    tab_p = _rope_table(jnp.arange(t, dtype=jnp.int32))
    tab_s = _rope_table(jnp.full((1,), past, jnp.int32))

    xp = x_prompt.reshape(bp * t, d)
    xs = x_sample.reshape(bs, d)
    mem_flat = mem_prompt.reshape(bp * n_mem, d)
    n_pool = cache_nsa_cmp.shape[1]
    cmp_pages = jnp.transpose(cache_nsa_cmp, (0, 1, 3, 4, 2)).reshape(depth * n_pool, 2 * HEAD_DIM, page)
    slc_pages = jnp.transpose(cache_nsa_slc, (0, 1, 3, 4, 2)).reshape(depth * n_pool, 2, HEAD_DIM, page)
    moba_pages = jnp.transpose(cache_moba, (0, 1, 3, 4, 5, 2)).reshape(depth * n_pool, 2 * N_HEADS, HEAD_DIM, page)
    sample_caches = (cmp_pages, slc_pages, cache_nsa_win, moba_pages, cache_mem, state_rwkv, state_shift,
                     n_pool, page)
    w_all = _pack_w_in(w_in)
    w_mem, w_br, w_o = w_mem_kv.astype(BF16), w_branch.astype(BF16), w_out.astype(BF16)
    w_fg, w_fu, w_fd = w_ffn_gate.astype(BF16), w_ffn_up.astype(BF16), w_ffn_down.astype(BF16)
    outs_p, outs_s, mem_out = [], [], []
    for l in range(depth):
        w = _layer_weights(l, nsa_pe, nsa_phi_w1, nsa_phi_w2, rwkv_mu, rwkv_w0, rwkv_w2, rwkv_a0,
                           rwkv_a2, rwkv_g2, rwkv_k_k, rwkv_k_a, rwkv_r_k, rwkv_ln_w, rwkv_ln_b)
        mem_kv = mm(rms_cast(mem_flat, g_mem[l]), w_mem, l, 0, 2 * BRANCH_W).reshape(bp, n_mem, 2 * BRANCH_W)
        mem_out.append(mem_kv.reshape(bp, n_mem, 2, N_HEADS, HEAD_DIM))

        o_p, gates_p, new_p = _mixers_prompt(rms_cast(xp, g_pre_mix[l]), w_all, l, w, tab_p, mem_kv, consts, bp, t)
        xp = merge(xp, o_p, gates_p, w_br, w_o, g_post_mix[l], l)
        xp = ffn(xp, g_pre_ffn[l], w_fg, w_fu, w_fd, g_post_ffn[l], l)

        o_s, gates_s, new_s = _mixers_sample(rms_cast(xs, g_pre_mix[l]), w_all, l, w, tab_s, page_table,
                                             sample_caches, consts, bs)
        xs = merge(xs, o_s, gates_s, w_br, w_o, g_post_mix[l], l)
        xs = ffn(xs, g_pre_ffn[l], w_fg, w_fu, w_fd, g_post_ffn[l], l)
        outs_p.append(new_p)
        outs_s.append(new_s)

    def stack(outs, i):
        return jnp.stack([o[i] for o in outs], axis=0)

    return (xp.reshape(bp, t, d), xs.reshape(bs, 1, d),
            stack(outs_p, 0), stack(outs_s, 0), stack(outs_p, 1), stack(outs_s, 1),
            stack(outs_p, 2), stack(outs_s, 2), stack(outs_p, 3), stack(outs_s, 3),
            jnp.stack(mem_out, axis=0),
            stack(outs_p, 4), stack(outs_s, 4), stack(outs_p, 5), stack(outs_s, 5))
```

```python
import functools

import jax
import jax.numpy as jnp
import numpy as np
from jax import lax
from jax.experimental import pallas as pl
from jax.experimental.pallas import tpu as pltpu

F32 = jnp.float32
BF16 = jnp.bfloat16

HEAD_DIM = 64
N_HEADS = 4
BRANCH_W = N_HEADS * HEAD_DIM
ROT_HALF = 8
ROPE_THETA = 500000.0
CMP_BLOCK = 32
CMP_STRIDE = 16
SLC_BLOCK = 64
N_SELECT = 16
WINDOW = 512
MOBA_BLOCK = 256
MOBA_TOPK = 3
RWKV_IN = 1056
RWKV_PAD = 1152
NSA_PAD = 768
RWKV_GN_EPS = 64e-5
RWKV_CHUNK = 64
NEG = -1e30
SCALE = HEAD_DIM ** -0.5
LANES = 128
ATTN_KEY_TILE = 128
VMEM_LIMIT = 48 * 1024 * 1024


def _iota(shape, dim):
    return lax.broadcasted_iota(jnp.int32, shape, dim)


def _dot(a, b):
    return lax.dot_general(a, b, (((1,), (0,)), ((), ())), preferred_element_type=F32)


def _dot_nt(a, b):
    return lax.dot_general(a, b, (((1,), (1,)), ((), ())), preferred_element_type=F32)


def _dot_tn(a, b):
    return lax.dot_general(a, b, (((0,), (0,)), ((), ())), preferred_element_type=F32)


def _split(x):
    hi = x.astype(BF16)
    lo = (x - hi.astype(F32)).astype(BF16)
    return hi, lo


def _dot3(a, b, dot=_dot):
    ah, al = _split(a)
    bh, bl = _split(b)
    return dot(ah, bh) + dot(al, bh) + dot(ah, bl)


def _dot2_exact_rhs(a, b_bf16):
    ah, al = _split(a)
    return _dot(ah, b_bf16) + _dot(al, b_bf16)


def _sigmoid(x):
    return 1.0 / (1.0 + jnp.exp(-x))


def _rms(x, g, eps=1e-6):
    return x * lax.rsqrt(jnp.mean(x * x, axis=-1, keepdims=True) + eps) * g


def _msoftmax(s, mask):
    s = jnp.where(mask, s, NEG)
    p = jnp.where(mask, jnp.exp(s - jnp.max(s, axis=-1, keepdims=True)), 0.0)
    return p / jnp.maximum(jnp.sum(p, axis=-1, keepdims=True), 1e-30)


def _flash_step(q, k, v, mask, m, l, acc):
    s = jnp.where(mask, _dot_nt(q, k), NEG)
    m_new = jnp.maximum(m, jnp.max(s, axis=-1, keepdims=True))
    p = jnp.where(mask, jnp.exp(s - m_new), 0.0)
    alpha = jnp.exp(m - m_new)
    l = alpha * l + jnp.sum(p, axis=-1, keepdims=True)
    acc = alpha * acc + _dot(p.astype(BF16), v)
    return m_new, l, acc


def _flash_heads_t(q_ts, ks, v_ts, masks, carry):
    ss = [jnp.where(mask, _dot(k, q_t), NEG) for q_t, k, mask in zip(q_ts, ks, masks)]
    m_news = [jnp.maximum(c[0], jnp.max(s, axis=0, keepdims=True)) for c, s in zip(carry, ss)]
    ps = [jnp.where(mask, jnp.exp(s - m_new), 0.0) for s, m_new, mask in zip(ss, m_news, masks)]
    pvs = [_dot(v_t, p.astype(BF16)) for v_t, p in zip(v_ts, ps)]
    out = []
    for c, m_new, p, pv in zip(carry, m_news, ps, pvs):
        alpha = jnp.exp(c[0] - m_new)
        out.append((m_new, alpha * c[1] + jnp.sum(p, axis=0, keepdims=True), alpha * c[2] + pv))
    return tuple(out)


def _rope(x, tab, half_only=False):
    c, sa, sb = tab[0], tab[1], tab[2]
    if half_only:
        first = _iota(x.shape, 1) < HEAD_DIM
        c = jnp.where(first, c, 1.0)
        sa = jnp.where(first, sa, 0.0)
        sb = jnp.where(first, sb, 0.0)
    return x * c + pltpu.roll(x, LANES - ROT_HALF, 1) * sa + pltpu.roll(x, ROT_HALF, 1) * sb


def _rope_heads(x, tab):
    return jnp.concatenate([_rope(x[:, :LANES], tab), _rope(x[:, LANES:], tab)], axis=1)


def _rank_lt(score, n_real, k):
    lane = _iota(score.shape, 1)
    cnt = jnp.zeros(score.shape, jnp.int32)
    for i in range(n_real):
        ci = score[:, i:i + 1]
        beats = (ci > score) | ((ci == score) & (lane > i))
        cnt = cnt + beats.astype(jnp.int32)
    return cnt < k


def _pick_tile(n, target, mult=8):
    if n <= target:
        return n
    for t in range(target, 0, -1):
        if n % t == 0 and t % mult == 0:
            return t
    return n


def _params(*sem):
    return pltpu.CompilerParams(dimension_semantics=sem, vmem_limit_bytes=VMEM_LIMIT)


def _rms_cast_kernel(x_ref, g_ref, o_ref):
    o_ref[...] = _rms(x_ref[...], g_ref[...]).astype(BF16)


def rms_cast(x, g):
    m, d = x.shape
    tm = _pick_tile(m, 1024, 16)
    return pl.pallas_call(
        _rms_cast_kernel, grid=(m // tm,),
        in_specs=[pl.BlockSpec((tm, d), lambda i: (i, 0)), pl.BlockSpec((1, d), lambda i: (0, 0))],
        out_specs=pl.BlockSpec((tm, d), lambda i: (i, 0)),
        out_shape=jax.ShapeDtypeStruct((m, d), BF16),
        compiler_params=_params("parallel"), name="rms_cast",
    )(x, g.reshape(1, d))


def _mm_kernel(a_ref, b_ref, o_ref, *, gate):
    y = _dot(a_ref[...], b_ref[...])
    o_ref[...] = (_sigmoid(y) if gate else y).astype(o_ref.dtype)


def mm(a, w, layer, col0, n, out_dtype=F32, gate=False):
    m, k = a.shape
    tm = _pick_tile(m, 1024, 16)
    tn = n if n <= 1536 else _pick_tile(n, 1024, LANES)
    assert col0 % tn == 0 and n % tn == 0
    c0 = col0 // tn
    return pl.pallas_call(
        functools.partial(_mm_kernel, gate=gate), grid=(m // tm, n // tn),
        in_specs=[pl.BlockSpec((tm, k), lambda i, j: (i, 0)),
                  pl.BlockSpec((None, k, tn), lambda i, j: (layer, 0, c0 + j))],
        out_specs=pl.BlockSpec((tm, tn), lambda i, j: (i, j)),
        out_shape=jax.ShapeDtypeStruct((m, n), out_dtype),
        compiler_params=_params("parallel", "parallel"), name="mm",
    )(a, w)


def _ffn_kernel(x_ref, g1_ref, wg_ref, wu_ref, wd_ref, g2_ref, o_ref, h_ref, acc_ref):
    f = pl.program_id(1)

    @pl.when(f == 0)
    def _():
        h_ref[...] = _rms(x_ref[...], g1_ref[...]).astype(BF16)
        acc_ref[...] = jnp.zeros_like(acc_ref)

    h = h_ref[...]
    gate = _dot(h, wg_ref[...])
    up = _dot(h, wu_ref[...])
    act = (gate * _sigmoid(gate) * up).astype(BF16)
    acc_ref[...] += _dot(act, wd_ref[...])

    @pl.when(f == pl.num_programs(1) - 1)
    def _():
        o_ref[...] = x_ref[...] + _rms(acc_ref[...], g2_ref[...])


def ffn(x, g_pre, wg, wu, wd, g_post, layer):
    m, d = x.shape
    dff = wg.shape[2]
    tm = _pick_tile(m, 512, 16)
    tf = _pick_tile(dff, 256, LANES)
    return pl.pallas_call(
        _ffn_kernel, grid=(m // tm, dff // tf),
        in_specs=[pl.BlockSpec((tm, d), lambda i, f: (i, 0)),
                  pl.BlockSpec((1, d), lambda i, f: (0, 0)),
                  pl.BlockSpec((None, d, tf), lambda i, f: (layer, 0, f)),
                  pl.BlockSpec((None, d, tf), lambda i, f: (layer, 0, f)),
                  pl.BlockSpec((None, tf, d), lambda i, f: (layer, f, 0)),
                  pl.BlockSpec((1, d), lambda i, f: (0, 0))],
        out_specs=pl.BlockSpec((tm, d), lambda i, f: (i, 0)),
        out_shape=jax.ShapeDtypeStruct((m, d), F32),
        scratch_shapes=[pltpu.VMEM((tm, d), BF16), pltpu.VMEM((tm, d), F32)],
        compiler_params=_params("parallel", "arbitrary"), name="ffn",
    )(x, g_pre.reshape(1, d), wg, wu, wd, g_post.reshape(1, d))


def _merge_kernel(x_ref, o0_ref, o1_ref, o2_ref, o3_ref, gate_ref, wb_ref, wo_ref, g_ref, out_ref):
    d = x_ref.shape[1]
    merged = None
    for b, o_ref in enumerate((o0_ref, o1_ref, o2_ref, o3_ref)):
        term = gate_ref[:, b * d:(b + 1) * d].astype(F32) * _dot(o_ref[...], wb_ref[b])
        merged = term if merged is None else merged + term
    y = _dot(merged.astype(BF16), wo_ref[...])
    out_ref[...] = x_ref[...] + _rms(y, g_ref[...])


def merge(x, outs, gates, w_branch, w_out, g_post, layer):
    m, d = x.shape
    bw = outs[0].shape[1]
    tm = _pick_tile(m, 512, 16)
    row = lambda i: (i, 0)
    return pl.pallas_call(
        _merge_kernel, grid=(m // tm,),
        in_specs=[pl.BlockSpec((tm, d), row)] + [pl.BlockSpec((tm, bw), row)] * 4 + [
            pl.BlockSpec((tm, 4 * d), row),
            pl.BlockSpec((None, 4, bw, d), lambda i: (layer, 0, 0, 0)),
            pl.BlockSpec((None, d, d), lambda i: (layer, 0, 0)),
            pl.BlockSpec((1, d), lambda i: (0, 0))],
        out_specs=pl.BlockSpec((tm, d), row),
        out_shape=jax.ShapeDtypeStruct((m, d), F32),
        compiler_params=_params("parallel"), name="merge",
    )(x, *outs, gates, w_branch, w_out, g_post.reshape(1, d))


def _nsa_rope_kernel(z_ref, tab_ref, slc_ref, win_ref):
    z = z_ref[0]
    tab = tab_ref[...]
    slc_ref[0] = _rope(z[:, 384:512], tab, half_only=True)
    win_ref[0] = _rope(z[:, 512:640], tab, half_only=True)


def nsa_rope(z_nsa, tab):
    b, t, w = z_nsa.shape
    tt = _pick_tile(t, 512)
    out = jax.ShapeDtypeStruct((b, t, LANES), F32)
    return pl.pallas_call(
        _nsa_rope_kernel, grid=(b, t // tt),
        in_specs=[pl.BlockSpec((1, tt, w), lambda i, j: (i, j, 0)),
                  pl.BlockSpec((3, tt, LANES), lambda i, j: (0, j, 0))],
        out_specs=[pl.BlockSpec((1, tt, LANES), lambda i, j: (i, j, 0))] * 2,
        out_shape=[out, out],
        compiler_params=_params("parallel", "parallel"), name="nsa_rope",
    )(z_nsa, tab)


def _moba_rope_kernel(z_ref, tab_ref, kv_ref, mean_ref):
    z = z_ref[0]
    kr = _rope_heads(z[:, BRANCH_W:2 * BRANCH_W], tab_ref[...])
    kv_ref[0] = jnp.concatenate([kr, z[:, 2 * BRANCH_W:3 * BRANCH_W]], axis=1)
    mean_ref[0, 0] = jnp.sum(kr, axis=0, keepdims=True) * (1.0 / MOBA_BLOCK)


def moba_rope(z_moba, tab):
    b, t, w = z_moba.shape
    nb = t // MOBA_BLOCK
    return pl.pallas_call(
        _moba_rope_kernel, grid=(b, nb),
        in_specs=[pl.BlockSpec((1, MOBA_BLOCK, w), lambda i, j: (i, j, 0)),
                  pl.BlockSpec((3, MOBA_BLOCK, LANES), lambda i, j: (0, j, 0))],
        out_specs=[pl.BlockSpec((1, MOBA_BLOCK, 2 * BRANCH_W), lambda i, j: (i, j, 0)),
                   pl.BlockSpec((1, 1, 1, BRANCH_W), lambda i, j: (i, j, 0, 0))],
        out_shape=[jax.ShapeDtypeStruct((b, t, 2 * BRANCH_W), F32),
                   jax.ShapeDtypeStruct((b, nb, 1, BRANCH_W), F32)],
        compiler_params=_params("parallel", "parallel"), name="moba_rope",
    )(z_moba, tab)


def _compress(g, w1_ref, pe_ref, w2_ref):
    r = g.shape[0]
    ab = _dot(g.astype(BF16), w1_ref[...])
    pe = _dot(pe_ref[...].astype(BF16), w1_ref[...])
    half = ab.shape[1] // 2
    nxt = pltpu.roll(ab[:, half:], r - 1, 0)
    pre = ab[:, :half] + nxt + pe[0:1, :half] + pe[1:2, half:]
    hid = pre * _sigmoid(pre)
    return _dot(hid.astype(BF16), w2_ref[...])


def _compress_kernel(g_ref, w1_ref, pe_ref, w2_ref, o_ref):
    o_ref[0] = _compress(g_ref[0], w1_ref, pe_ref, w2_ref)


def compress(g, w1c, pe2, w2c):
    b, r, w = g.shape
    const = lambda i: (0, 0)
    return pl.pallas_call(
        _compress_kernel, grid=(b,),
        in_specs=[pl.BlockSpec((1, r, w), lambda i: (i, 0, 0)),
                  pl.BlockSpec(w1c.shape, const), pl.BlockSpec(pe2.shape, const),
                  pl.BlockSpec(w2c.shape, const)],
        out_specs=pl.BlockSpec((1, r, LANES), lambda i: (i, 0, 0)),
        out_shape=jax.ShapeDtypeStruct((b, r, LANES), F32),
        compiler_params=_params("parallel"), name="nsa_compress",
    )(g, w1c, pe2, w2c)


def _rank_lt_rows(score, n_real, k):
    row = _iota(score.shape, 0)
    cnt = jnp.zeros(score.shape, jnp.int32)
    for i in range(n_real):
        ri = score[i:i + 1, :]
        beats = (ri > score) | ((ri == score) & (row > i))
        cnt = cnt + beats.astype(jnp.int32)
    return cnt < k


def _select_blocks(imp, t_pos, n_slc, k_sel):
    j = _iota(imp.shape, 1)
    own = lax.shift_right_logical(t_pos, 6)
    causal = (j * SLC_BLOCK <= t_pos) & (j < n_slc)
    forced = (j == 0) | (j == own) | (j == own - 1)
    score = jnp.where(forced, -NEG, jnp.where(causal, imp, NEG))
    return _rank_lt(score, n_slc, k_sel) & causal


def _nsa_attn_kernel(z_ref, tab_ref, cmp_ref, slc_ref, win_ref, cover_ref, o_ref, sel_s,
                     *, tq, tk, n_cmp, n_slc, k_sel):
    qi = pl.program_id(1)
    z = z_ref[0]
    q = z[:, :BRANCH_W] * SCALE
    qr = _rope_heads(q, tab_ref[...])
    heads = [slice(h * HEAD_DIM, (h + 1) * HEAD_DIM) for h in range(N_HEADS)]

    cmpkv = cmp_ref[0]
    ck = cmpkv[:, :HEAD_DIM].astype(BF16)
    cv = cmpkv[:, HEAD_DIM:].astype(BF16)
    t_row = qi * tq + _iota((1, tq), 1)
    n = _iota((cmpkv.shape[0], 1), 0)
    cmask = (n * CMP_STRIDE + (CMP_BLOCK - 1) <= t_row) & (n < n_cmp)
    o_cmp, psum = [], None
    for hs in heads:
        s = jnp.where(cmask, _dot_nt(ck, q[:, hs].astype(BF16)), NEG)
        p = jnp.where(cmask, jnp.exp(s - jnp.max(s, axis=0, keepdims=True)), 0.0)
        p = p / jnp.maximum(jnp.sum(p, axis=0, keepdims=True), 1e-30)
        o_cmp.append(_dot_tn(cv, p.astype(BF16)))
        psum = p if psum is None else psum + p
    p_hi, p_lo = _split(psum)
    imp = _dot(cover_ref[...], p_hi) + _dot(cover_ref[...], p_lo)
    j = _iota(imp.shape, 0)
    own = lax.shift_right_logical(t_row, 6)
    causal = (j * SLC_BLOCK <= t_row) & (j < n_slc)
    forced = (j == 0) | (j == own) | (j == own - 1)
    score = jnp.where(forced, -NEG, jnp.where(causal, imp, NEG))
    sel_s[...] = (_rank_lt_rows(score, n_slc, k_sel) & causal).astype(F32)

    qr_t = qr.T.astype(BF16)
    qrb = [qr_t[hs, :] for hs in heads]
    init = tuple((jnp.full((1, tq), NEG, F32), jnp.zeros((1, tq), F32), jnp.zeros((HEAD_DIM, tq), F32))
                 for _ in heads)
    blocks_per_tile = tk // SLC_BLOCK
    tiles_per_q = tq // tk

    def slc_body(kt, carry):
        rows = slc_ref[0, pl.ds(pl.multiple_of(kt * tk, tk), tk), :]
        k = rows[:, :HEAD_DIM].astype(BF16)
        v_t = rows.T[HEAD_DIM:, :].astype(BF16)
        kpos = kt * tk + _iota((tk, 1), 0)
        picked = jnp.concatenate(
            [jnp.broadcast_to(sel_s[pl.ds(kt * blocks_per_tile + u, 1), :], (SLC_BLOCK, tq))
             for u in range(blocks_per_tile)], axis=0)
        mask = (picked > 0.5) & (kpos <= t_row)
        return _flash_heads_t(qrb, [k] * N_HEADS, [v_t] * N_HEADS, [mask] * N_HEADS, carry)

    def win_body(kt, carry):
        rows = win_ref[0, pl.ds(pl.multiple_of(kt * tk, tk), tk), :]
        k = rows[:, :HEAD_DIM].astype(BF16)
        v_t = rows.T[HEAD_DIM:, :].astype(BF16)
        d = t_row - (kt * tk + _iota((tk, 1), 0))
        mask = (d >= 0) & (d <= WINDOW)
        return _flash_heads_t(qrb, [k] * N_HEADS, [v_t] * N_HEADS, [mask] * N_HEADS, carry)

    hi = (qi + 1) * tiles_per_q
    slc = lax.fori_loop(0, hi, slc_body, init)
    win = lax.fori_loop(jnp.maximum(qi * tiles_per_q - (-(-WINDOW // tk)), 0), hi, win_body, init)

    g = _sigmoid(z[:, 640:640 + LANES]).T
    outs = []
    for h in range(N_HEADS):
        o_slc = slc[h][2] / jnp.maximum(slc[h][1], 1e-30)
        o_win = win[h][2] / jnp.maximum(win[h][1], 1e-30)
        outs.append(g[3 * h:3 * h + 1, :] * o_cmp[h] + g[3 * h + 1:3 * h + 2, :] * o_slc
                    + g[3 * h + 2:3 * h + 3, :] * o_win)
    o_ref[0] = jnp.concatenate(outs, axis=0).T.astype(o_ref.dtype)


def nsa_attn(z_nsa, tab, cmp_kv, new_slc, new_win, cover, n_cmp, n_slc):
    b, t, w = z_nsa.shape
    tq = _pick_tile(t, 256)
    tk = ATTN_KEY_TILE
    assert tq % tk == 0 and tk % SLC_BLOCK == 0 and WINDOW % tk == 0
    kern = functools.partial(_nsa_attn_kernel, tq=tq, tk=tk, n_cmp=n_cmp, n_slc=n_slc,
                             k_sel=min(N_SELECT, n_slc))
    full = lambda i, j: (i, 0, 0)
    const = lambda i, j: (0, 0)
    return pl.pallas_call(
        kern, grid=(b, t // tq),
        in_specs=[pl.BlockSpec((1, tq, w), lambda i, j: (i, j, 0)),
                  pl.BlockSpec((3, tq, LANES), lambda i, j: (0, j, 0)),
                  pl.BlockSpec((1,) + cmp_kv.shape[1:], full),
                  pl.BlockSpec((1, t, LANES), full), pl.BlockSpec((1, t, LANES), full),
                  pl.BlockSpec(cover.shape, const)],
        out_specs=pl.BlockSpec((1, tq, BRANCH_W), lambda i, j: (i, j, 0)),
        out_shape=jax.ShapeDtypeStruct((b, t, BRANCH_W), BF16),
        scratch_shapes=[pltpu.VMEM((cover.shape[0], tq), F32)],
        compiler_params=_params("parallel", "parallel"), name="nsa_attn",
    )(z_nsa, tab, cmp_kv, new_slc, new_win, cover)


def _moba_attn_kernel(z_ref, tab_ref, kv_ref, mean_ref, o_ref, pick_s, *, tq, tk, nb, topk):
    qi = pl.program_id(1)
    q = z_ref[0][:, :BRANCH_W] * SCALE
    qr = _rope_heads(q, tab_ref[...])
    means = mean_ref[0]
    heads = [slice(h * HEAD_DIM, (h + 1) * HEAD_DIM) for h in range(N_HEADS)]

    t_row = qi * tq + _iota((1, tq), 1)
    own = lax.shift_right_logical(t_row, 8)
    past = _iota((means.shape[0], 1), 0) < own
    for h, hs in enumerate(heads):
        gate = _dot3(means[:, hs], qr[:, hs], _dot_nt)
        pick_s[h] = (_rank_lt_rows(jnp.where(past, gate, NEG), nb, topk) & past).astype(F32)

    qr_t = qr.T.astype(BF16)
    init = tuple((jnp.full((1, tq), NEG, F32), jnp.zeros((1, tq), F32), jnp.zeros((HEAD_DIM, tq), F32))
                 for _ in heads)
    tiles_per_block = MOBA_BLOCK // tk

    def body(kt, carry):
        rows = kv_ref[0, pl.ds(pl.multiple_of(kt * tk, tk), tk), :]
        v_t = rows[:, BRANCH_W:].T.astype(BF16)
        blk = kt // tiles_per_block
        in_own = (own == blk) & (kt * tk + _iota((tk, 1), 0) <= t_row)
        not_own = own != blk
        masks = [in_own | (not_own & (pick_s[h, pl.ds(blk, 1), :] > 0.5)) for h in range(N_HEADS)]
        return _flash_heads_t([qr_t[hs, :] for hs in heads], [rows[:, hs].astype(BF16) for hs in heads],
                              [v_t[hs, :] for hs in heads], masks, carry)

    res = lax.fori_loop(0, (qi + 1) * (tq // tk), body, init)
    outs = [res[h][2] / jnp.maximum(res[h][1], 1e-30) for h in range(N_HEADS)]
    o_ref[0] = jnp.concatenate(outs, axis=0).T.astype(o_ref.dtype)


def moba_attn(z_moba, tab, new_moba, means):
    b, t, w = z_moba.shape
    assert t % MOBA_BLOCK == 0
    tq = MOBA_BLOCK
    tk = ATTN_KEY_TILE
    assert MOBA_BLOCK % tk == 0
    nb = t // MOBA_BLOCK
    kern = functools.partial(_moba_attn_kernel, tq=tq, tk=tk, nb=nb, topk=min(MOBA_TOPK, nb))
    full = lambda i, j: (i, 0, 0)
    return pl.pallas_call(
        kern, grid=(b, t // tq),
        in_specs=[pl.BlockSpec((1, tq, w), lambda i, j: (i, j, 0)),
                  pl.BlockSpec((3, tq, LANES), lambda i, j: (0, j, 0)),
                  pl.BlockSpec((1, t, 2 * BRANCH_W), full),
                  pl.BlockSpec((1, nb, BRANCH_W), full)],
        out_specs=pl.BlockSpec((1, tq, BRANCH_W), lambda i, j: (i, j, 0)),
        out_shape=jax.ShapeDtypeStruct((b, t, BRANCH_W), BF16),
        scratch_shapes=[pltpu.VMEM((N_HEADS, nb, tq), F32)],
        compiler_params=_params("parallel", "parallel"), name="moba_attn",
    )(z_moba, tab, new_moba, means)


def _cross_attn_kernel(q_ref, mem_ref, o_ref):
    q = q_ref[0] * SCALE
    mem = mem_ref[0]
    outs = []
    for h in range(N_HEADS):
        hs = slice(h * HEAD_DIM, (h + 1) * HEAD_DIM)
        s = _dot_nt(q[:, hs].astype(BF16), mem[:, hs].astype(BF16))
        p = jnp.exp(s - jnp.max(s, axis=-1, keepdims=True))
        p = p / jnp.sum(p, axis=-1, keepdims=True)
        v = mem[:, BRANCH_W + h * HEAD_DIM:BRANCH_W + (h + 1) * HEAD_DIM].astype(BF16)
        outs.append(_dot(p.astype(BF16), v))
    o_ref[0] = jnp.concatenate(outs, axis=1).astype(o_ref.dtype)


def cross_attn(q, mem_kv):
    b, t, w = q.shape
    n_mem = mem_kv.shape[1]
    tq = _pick_tile(t, 512)
    return pl.pallas_call(
        _cross_attn_kernel, grid=(b, t // tq),
        in_specs=[pl.BlockSpec((1, tq, w), lambda i, j: (i, j, 0)),
                  pl.BlockSpec((1, n_mem, 2 * BRANCH_W), lambda i, j: (i, 0, 0))],
        out_specs=pl.BlockSpec((1, tq, BRANCH_W), lambda i, j: (i, j, 0)),
        out_shape=jax.ShapeDtypeStruct((b, t, BRANCH_W), BF16),
        compiler_params=_params("parallel", "parallel"), name="cross_attn",
    )(q, mem_kv)


def _head_sum(x, bd_ref):
    return _dot2_exact_rhs(x, bd_ref[...])


def _rwkv_prep(z, zprev, vec_ref, wa_ref, g2_ref, bd_ref):
    mu = vec_ref[0:1, :]
    zm = z + (zprev - z) * mu
    r = zm[:, 0:256]
    k = zm[:, 256:512]
    v = zm[:, 512:768]
    lora = zm[:, 768:896]
    lora = jnp.where(_iota(lora.shape, 1) < 64, jnp.tanh(lora), lora)
    wa = _dot(lora.astype(BF16), wa_ref[...])
    wl = vec_ref[1:2, 0:256] + wa[:, :256]
    x = -wl
    softplus = jnp.maximum(x, 0.0) + jnp.log(1.0 + jnp.exp(-jnp.abs(x)))
    logdecay = -jnp.exp(-softplus - 0.5)
    a = _sigmoid(vec_ref[1:2, 256:512] + wa[:, 256:])
    g = _dot(_sigmoid(zm[:, 896:1152]).astype(BF16), g2_ref[...])
    kkr = k * vec_ref[2:3, 0:256]
    kk = kkr * lax.rsqrt(jnp.maximum(_head_sum(kkr * kkr, bd_ref), 1e-24))
    k2 = k * (1.0 + (a - 1.0) * vec_ref[3:4, 0:256])
    bonus = _head_sum(r * k2 * vec_ref[4:5, 0:256], bd_ref) * v
    return r, k2, v, kk, kk * a, logdecay, g, bonus


def _rwkv_finish(y, bonus, g, vec_ref, bd_ref):
    mu = _head_sum(y, bd_ref) * (1.0 / HEAD_DIM)
    d = y - mu
    var = _head_sum(d * d, bd_ref) * (1.0 / HEAD_DIM)
    yn = d * lax.rsqrt(var + RWKV_GN_EPS) * vec_ref[5:6, 0:256] + vec_ref[6:7, 0:256]
    return (yn + bonus) * g


def _dotc(a, b, dot=_dot):
    return dot(a.astype(BF16), b.astype(BF16))


def _tri_inv_all(lows):
    c = lows[0].shape[0]
    eye = (_iota((c, c), 0) == _iota((c, c), 1)).astype(F32)
    ts = [eye - low for low in lows]
    lps = [_dotc(low, low) for low in lows]
    n = 2
    while n < c:
        ts = [t + _dotc(t, lp) for t, lp in zip(ts, lps)]
        n *= 2
        if n < c:
            lps = [_dotc(lp, lp) for lp in lps]
    return ts


RWKV_GROUP = 8


def _rwkv_seq_kernel(z_ref, shift_ref, s0_ref, vec_ref, wa_ref, g2_ref, bd_ref, tri_ref,
                     o_ref, s_ref, last_ref, r_s, k_s, v_s, kk_s, b_s, ld_s, y_s, pe_s, *, tt):
    ti = pl.program_id(1)

    @pl.when(ti == 0)
    def _():
        last_ref[...] = shift_ref[0]
        s_ref[0] = s0_ref[0]

    z = z_ref[0]
    rolled = pltpu.roll(z, 1, 0)
    zprev = jnp.where(_iota(z.shape, 0) == 0, last_ref[...], rolled)
    last_ref[...] = z[tt - 1:tt, :]
    r, k2, v, kk, b, ld, g, bonus = _rwkv_prep(z, zprev, vec_ref, wa_ref, g2_ref, bd_ref)
    r_s[...] = r
    k_s[...] = k2
    v_s[...] = v
    kk_s[...] = kk
    b_s[...] = b
    ld_s[...] = ld

    c = RWKV_CHUNK
    ri = _iota((c, c), 0)
    ci = _iota((c, c), 1)
    strict = ri > ci
    incl = ri >= ci

    eye = (ri == ci).astype(F32)
    heads = [slice(h * HEAD_DIM, (h + 1) * HEAD_DIM) for h in range(N_HEADS)]
    n_chunks = tt // c
    group = RWKV_GROUP if n_chunks % RWKV_GROUP == 0 else 1

    def prepare(gi, carry):
        probs = []
        for j in range(group):
            ck = gi * group + j
            rows = pl.ds(pl.multiple_of(ck * c, c), c)
            ldc = ld_s[rows, :]
            cum = _dot3(tri_ref[...], ldc)
            p_in = jnp.exp(cum)
            inv_p = jnp.exp(-cum)
            kkt = kk_s[rows, :] * jnp.exp(cum - ldc)
            bt = b_s[rows, :] * inv_p
            kt = k_s[rows, :] * inv_p
            rt = r_s[rows, :] * p_in
            vv = v_s[rows, :]
            p_end = p_in[c - 1:c, :]
            pe_s[pl.ds(ck, 1), :] = p_end
            for hs in heads:
                probs.append(dict(kkt=kkt[:, hs], bt=bt[:, hs], kt=kt[:, hs], rt=rt[:, hs], v=vv[:, hs],
                                  p_end=p_end[:, hs]))
        blocks = [_dotc(jnp.concatenate([p['kkt'], p['rt']], axis=0),
                        jnp.concatenate([p['bt'], p['kt']], axis=0), _dot_nt) for p in probs]
        lows = [jnp.where(strict, bl[:c, :c], 0.0) for bl in blocks]
        gvs = [_dotc(jnp.where(strict, bl[:c, c:], 0.0), p['v']) for bl, p in zip(blocks, probs)]
        tinvs = _tri_inv_all(lows)
        ke_ue = [_dotc(t, jnp.concatenate([p['kkt'], gv], axis=1)) for t, p, gv in zip(tinvs, probs, gvs)]
        corr = [_dotc(jnp.where(incl, bl[c:, :c], 0.0), ku) for bl, ku in zip(blocks, ke_ue)]
        avs = [_dotc(jnp.where(incl, bl[c:, c:], 0.0), p['v']) for bl, p in zip(blocks, probs)]
        xps = [_dotc(ku[:, :HEAD_DIM], p['bt'], _dot_tn) * p['p_end'] for ku, p in zip(ke_ue, probs)]
        bcs = [_dotc(jnp.concatenate([p['v'], ku[:, HEAD_DIM:]], axis=0),
                     jnp.concatenate([p['kt'], -p['bt']], axis=0), _dot_tn) * p['p_end']
               for ku, p in zip(ke_ue, probs)]
        res = [p['rt'] - cr[:, :HEAD_DIM] for p, cr in zip(probs, corr)]
        y0s = [av - cr[:, HEAD_DIM:] for av, cr in zip(avs, corr)]
        for j in range(group):
            rows = pl.ds(pl.multiple_of((gi * group + j) * c, c), c)
            sl = slice(j * N_HEADS, (j + 1) * N_HEADS)
            r_s[rows, :] = jnp.concatenate(res[sl], axis=1)
            v_s[rows, :] = jnp.concatenate(y0s[sl], axis=1)
            kk_s[rows, :] = jnp.concatenate(xps[sl], axis=1)
            b_s[rows, :] = jnp.concatenate(bcs[sl], axis=1)
        return carry

    lax.fori_loop(0, n_chunks // group, prepare, 0)

    def advance(ck, carry):
        rows = pl.ds(pl.multiple_of(ck * c, c), c)
        re, y0, xp, bc = r_s[rows, :], v_s[rows, :], kk_s[rows, :], b_s[rows, :]
        p_end = pe_s[pl.ds(ck, 1), :]
        states = [s_ref[0, h] for h in range(N_HEADS)]
        ys = [_dotc(re[:, hs], s_h, _dot_nt) + y0[:, hs] for hs, s_h in zip(heads, states)]
        for h, (hs, s_h) in enumerate(zip(heads, states)):
            s_ref[0, h] = s_h * p_end[:, hs] - _dotc(s_h, xp[:, hs]) + bc[:, hs]
        y_s[rows, :] = jnp.concatenate(ys, axis=1)
        return carry

    lax.fori_loop(0, n_chunks, advance, 0, unroll=True)
    o_ref[0] = _rwkv_finish(y_s[...], bonus, g, vec_ref, bd_ref).astype(o_ref.dtype)


def rwkv_seq(z, shift, s0, vec, wa, g2, bd, tri):
    b, t, w = z.shape
    tt = _pick_tile(t, 512, RWKV_CHUNK)
    assert tt % RWKV_CHUNK == 0
    const = lambda i, j: (0, 0)
    scr = pltpu.VMEM((tt, BRANCH_W), F32)
    return pl.pallas_call(
        functools.partial(_rwkv_seq_kernel, tt=tt), grid=(b, t // tt),
        in_specs=[pl.BlockSpec((1, tt, w), lambda i, j: (i, j, 0)),
                  pl.BlockSpec((1, 1, w), lambda i, j: (i, 0, 0)),
                  pl.BlockSpec((1, N_HEADS, HEAD_DIM, HEAD_DIM), lambda i, j: (i, 0, 0, 0)),
                  pl.BlockSpec(vec.shape, const), pl.BlockSpec(wa.shape, const),
                  pl.BlockSpec(g2.shape, const), pl.BlockSpec(bd.shape, const),
                  pl.BlockSpec(tri.shape, const)],
        out_specs=[pl.BlockSpec((1, tt, BRANCH_W), lambda i, j: (i, j, 0)),
                   pl.BlockSpec((1, N_HEADS, HEAD_DIM, HEAD_DIM), lambda i, j: (i, 0, 0, 0))],
        out_shape=[jax.ShapeDtypeStruct((b, t, BRANCH_W), BF16),
                   jax.ShapeDtypeStruct((b, N_HEADS, HEAD_DIM, HEAD_DIM), F32)],
        scratch_shapes=[pltpu.VMEM((1, w), F32)] + [scr] * 7 + [pltpu.VMEM((tt // RWKV_CHUNK, BRANCH_W), F32)],
        compiler_params=_params("parallel", "arbitrary"), name="rwkv_seq",
    )(z, shift, s0, vec, wa, g2, bd, tri)


def _rwkv_step_kernel(z_ref, shift_ref, s0_ref, vec_ref, wa_ref, g2_ref, bd_ref, o_ref, s_ref):
    rows = 8
    z = jnp.broadcast_to(z_ref[0], (rows, z_ref.shape[2]))
    zprev = jnp.broadcast_to(shift_ref[0], z.shape)
    r, k2, v, kk, b, ld, g, bonus = _rwkv_prep(z, zprev, vec_ref, wa_ref, g2_ref, bd_ref)
    decay = jnp.exp(ld)
    eye = _iota((HEAD_DIM, HEAD_DIM), 0) == _iota((HEAD_DIM, HEAD_DIM), 1)
    ys = []
    for h in range(N_HEADS):
        hs = slice(h * HEAD_DIM, (h + 1) * HEAD_DIM)
        s_h = s0_ref[0, h]
        sk = jnp.sum(s_h * kk[0:1, hs], axis=1, keepdims=True)
        v_col = jnp.sum(jnp.where(eye, v[0:1, hs], 0.0), axis=1, keepdims=True)
        s_new = s_h * decay[0:1, hs] - sk * b[0:1, hs] + v_col * k2[0:1, hs]
        s_ref[0, h] = s_new
        y_col = jnp.sum(s_new * r[0:1, hs], axis=1, keepdims=True)
        ys.append(jnp.sum(jnp.where(eye, y_col, 0.0), axis=0, keepdims=True))
    y = jnp.broadcast_to(jnp.concatenate(ys, axis=1), (rows, BRANCH_W))
    o_ref[0] = _rwkv_finish(y, bonus, g, vec_ref, bd_ref)[0:1].astype(o_ref.dtype)


def rwkv_step(z, shift, s0, vec, wa, g2, bd):
    b, _, w = z.shape
    const = lambda i: (0, 0)
    return pl.pallas_call(
        _rwkv_step_kernel, grid=(b,),
        in_specs=[pl.BlockSpec((1, 1, w), lambda i: (i, 0, 0)),
                  pl.BlockSpec((1, 1, w), lambda i: (i, 0, 0)),
                  pl.BlockSpec((1, N_HEADS, HEAD_DIM, HEAD_DIM), lambda i: (i, 0, 0, 0)),
                  pl.BlockSpec(vec.shape, const), pl.BlockSpec(wa.shape, const),
                  pl.BlockSpec(g2.shape, const), pl.BlockSpec(bd.shape, const)],
        out_specs=[pl.BlockSpec((1, 1, BRANCH_W), lambda i: (i, 0, 0)),
                   pl.BlockSpec((1, N_HEADS, HEAD_DIM, HEAD_DIM), lambda i: (i, 0, 0, 0))],
        out_shape=[jax.ShapeDtypeStruct((b, 1, BRANCH_W), BF16),
                   jax.ShapeDtypeStruct((b, N_HEADS, HEAD_DIM, HEAD_DIM), F32)],
        compiler_params=_params("parallel"), name="rwkv_step",
    )(z, shift, s0, vec, wa, g2, bd)


def _stack_heads(x):
    rows = [x[:, h * HEAD_DIM:(h + 1) * HEAD_DIM] for h in range(N_HEADS)]
    return jnp.concatenate(rows + [jnp.zeros((8 - N_HEADS, HEAD_DIM), x.dtype)], axis=0)


def _unstack_heads(x):
    return jnp.concatenate([x[h:h + 1, :] for h in range(N_HEADS)], axis=1)


def _column(row):
    n = row.shape[1]
    eye = _iota((n, n), 0) == _iota((n, n), 1)
    return jnp.sum(jnp.where(eye, row, 0.0), axis=1, keepdims=True)


def _to_row(col):
    n = col.shape[0]
    eye = _iota((n, n), 0) == _iota((n, n), 1)
    return jnp.sum(jnp.where(eye, col, 0.0), axis=0, keepdims=True)


def _page_scores(k_page, q_cols, s_ref, n_pages):
    def body(p, c):
        for h, qc in enumerate(q_cols):
            s_ref[h, pl.ds(p, 1), :] = jnp.sum(k_page(p, h) * qc, axis=0, keepdims=True)
        return c

    lax.fori_loop(0, n_pages, body, 0, unroll=4)


def _page_values(v_page, p_ref, n_pages):
    def body(p, accs):
        return tuple(acc + v_page(p, h) * p_ref[h, pl.ds(p, 1), :] for h, acc in enumerate(accs))

    init = tuple(jnp.zeros((HEAD_DIM, p_ref.shape[2]), F32) for _ in range(N_HEADS))
    return [jnp.sum(a, axis=1, keepdims=True) for a in lax.fori_loop(0, n_pages, body, init, unroll=4)]


def _softmax_with_new(s, mask, s_new):
    s = jnp.where(mask, s, NEG)
    m = jnp.maximum(jnp.max(jnp.max(s, axis=1, keepdims=True), axis=0, keepdims=True), s_new)
    p = jnp.where(mask, jnp.exp(s - m), 0.0)
    p_new = jnp.exp(s_new - m)
    den = jnp.maximum(jnp.sum(jnp.sum(p, axis=1, keepdims=True), axis=0, keepdims=True) + p_new, 1e-30)
    return p / den, p_new / den


def _nsa_sample_kernel(pt_ref, z_ref, tab_ref, win_ref, w1_ref, pe_ref, w2_ref, cover_ref, pj_ref, hr_ref,
                       cmp_hbm, slc_hbm, o_ref, slc_row_ref, win_row_ref,
                       cbuf, sbuf, rows_s, lines_s, s_s, p_s, sem,
                       *, base, n_pages, page, past, n_cmp, n_slc, k_sel):
    bi = pl.program_id(0)
    n_lines = past // CMP_STRIDE
    slot = lax.rem(bi, 2)

    def copies(seq, sl, p):
        pg = pt_ref[seq * n_pages + p] + base
        return (pltpu.make_async_copy(cmp_hbm.at[pg], cbuf.at[sl, p], sem.at[sl, 0]),
                pltpu.make_async_copy(slc_hbm.at[pg], sbuf.at[sl, p], sem.at[sl, 1]))

    def start_fetch(seq, sl):
        for p in range(n_pages):
            for cp in copies(seq, sl, p):
                cp.start()

    @pl.when(bi == 0)
    def _():
        start_fetch(0, 0)

    @pl.when(bi + 1 < pl.num_programs(0))
    def _():
        start_fetch(bi + 1, 1 - slot)

    z = z_ref[0]
    tab = tab_ref[...]
    q = z[:, :BRANCH_W] * SCALE
    q4 = _stack_heads(q)
    q4r_f32 = _stack_heads(_rope_heads(q, tab))
    q4r = q4r_f32.astype(BF16)
    slc_row = _rope(z[:, 384:512], tab, half_only=True)
    win_row = _rope(z[:, 512:640], tab, half_only=True)
    slc_row_ref[0] = slc_row
    win_row_ref[0] = win_row

    for p in range(n_pages):
        for cp in copies(bi, slot, p):
            cp.wait()

    def xpose(p, c):
        rows_s[pl.ds(pl.multiple_of(p * page, page), page), :] = cbuf[slot, p].T
        return c

    lax.fori_loop(0, n_pages, xpose, 0, unroll=8)
    for l in range(CMP_STRIDE):
        lines_s[:, l * LANES:(l + 1) * LANES] = rows_s[pl.ds(l, n_lines, stride=CMP_STRIDE), :]

    cmpkv = _compress(lines_s[...], w1_ref, pe_ref, w2_ref)
    ck = cmpkv[:, :HEAD_DIM].astype(BF16)
    cv = cmpkv[:, HEAD_DIM:].astype(BF16)
    n = _iota((1, cmpkv.shape[0]), 1)
    cmask = (n * CMP_STRIDE + (CMP_BLOCK - 1) <= past) & (n < n_cmp)
    p_cmp = _msoftmax(_dot_nt(q4.astype(BF16), ck), cmask)
    o_cmp = _dot(p_cmp.astype(BF16), cv)
    psum = jnp.sum(jnp.where(_iota(p_cmp.shape, 0) < N_HEADS, p_cmp, 0.0), axis=0, keepdims=True)
    imp = _dot2_exact_rhs(jnp.broadcast_to(psum, p_cmp.shape), cover_ref[...])[0:1]
    t_pos = jnp.full((1, 1), past, jnp.int32)
    sel = _select_blocks(imp, t_pos, n_slc, k_sel)

    def with_new_key(s_past, mask, new_row, v_past):
        s_new = jnp.sum(q4r_f32 * new_row[:, :HEAD_DIM], axis=1, keepdims=True)
        s_past = jnp.where(mask, s_past, NEG)
        m = jnp.maximum(jnp.max(s_past, axis=1, keepdims=True), s_new)
        p_past = jnp.where(mask, jnp.exp(s_past - m), 0.0)
        p_new = jnp.exp(s_new - m)
        den = jnp.maximum(jnp.sum(p_past, axis=1, keepdims=True) + p_new, 1e-30)
        return _dot((p_past / den).astype(BF16), v_past) + (p_new / den) * new_row[:, HEAD_DIM:]

    sel_pages = _dot((pj_ref[...] * sel.astype(F32)).astype(BF16), hr_ref[...]) > 0.5
    _page_scores(lambda p, h: sbuf[slot, p, 0], [_column(q4r_f32[h:h + 1, :]) for h in range(N_HEADS)], s_s,
                 n_pages)
    p_new = []
    for h in range(N_HEADS):
        s_new = jnp.sum(q4r_f32[h:h + 1, :] * slc_row[:, :HEAD_DIM], axis=1, keepdims=True)
        p_s[h], pn = _softmax_with_new(s_s[h], sel_pages, s_new)
        p_new.append(pn)
    o_slc = [_to_row(col) + p_new[h] * slc_row[:, HEAD_DIM:]
             for h, col in enumerate(_page_values(lambda p, h: sbuf[slot, p, 1], p_s, n_pages))]
    wrows = win_ref[0]
    wk = wrows[:, :HEAD_DIM].astype(BF16)
    wv = wrows[:, HEAD_DIM:].astype(BF16)
    o_win = with_new_key(_dot_nt(q4r, wk), jnp.full((1, wrows.shape[0]), True), win_row, wv)

    g = _sigmoid(z[:, 640:640 + 3 * N_HEADS])
    outs = []
    for h in range(N_HEADS):
        outs.append(g[:, 3 * h:3 * h + 1] * o_cmp[h:h + 1] + g[:, 3 * h + 1:3 * h + 2] * o_slc[h]
                    + g[:, 3 * h + 2:3 * h + 3] * o_win[h:h + 1])
    o_ref[0] = jnp.concatenate(outs, axis=1).astype(o_ref.dtype)


def nsa_sample(page_table, z_nsa, tab, win, w1c, pe2, w2c, cover, pj, hr, cache_cmp, cache_slc,
               layer, n_pool, page, n_cmp, n_slc):
    b = z_nsa.shape[0]
    n_pages = page_table.shape[1]
    past = n_pages * page
    kern = functools.partial(_nsa_sample_kernel, base=layer * n_pool, n_pages=n_pages, page=page, past=past,
                             n_cmp=n_cmp, n_slc=n_slc, k_sel=min(N_SELECT, n_slc))
    const = lambda i, pt: (0, 0)
    row = lambda i, pt: (i, 0, 0)
    grid_spec = pltpu.PrefetchScalarGridSpec(
        num_scalar_prefetch=1, grid=(b,),
        in_specs=[pl.BlockSpec((1, 1, z_nsa.shape[2]), row),
                  pl.BlockSpec(tab.shape, lambda i, pt: (0, 0, 0)),
                  pl.BlockSpec((1,) + win.shape[1:], row),
                  pl.BlockSpec(w1c.shape, const), pl.BlockSpec(pe2.shape, const),
                  pl.BlockSpec(w2c.shape, const), pl.BlockSpec(cover.shape, const),
                  pl.BlockSpec(pj.shape, const), pl.BlockSpec(hr.shape, const),
                  pl.BlockSpec(memory_space=pl.ANY), pl.BlockSpec(memory_space=pl.ANY)],
        out_specs=[pl.BlockSpec((1, 1, BRANCH_W), row), pl.BlockSpec((1, 1, LANES), row),
                   pl.BlockSpec((1, 1, LANES), row)],
        scratch_shapes=[pltpu.VMEM((2, n_pages, 2 * HEAD_DIM, page), F32),
                        pltpu.VMEM((2, n_pages, 2, HEAD_DIM, page), F32),
                        pltpu.VMEM((past, 2 * HEAD_DIM), F32),
                        pltpu.VMEM((past // CMP_STRIDE, CMP_STRIDE * LANES), F32),
                        pltpu.VMEM((N_HEADS, n_pages, page), F32),
                        pltpu.VMEM((N_HEADS, n_pages, page), F32),
                        pltpu.SemaphoreType.DMA((2, 2))])
    return pl.pallas_call(
        kern, grid_spec=grid_spec,
        out_shape=[jax.ShapeDtypeStruct((b, 1, BRANCH_W), BF16),
                   jax.ShapeDtypeStruct((b, 1, LANES), F32), jax.ShapeDtypeStruct((b, 1, LANES), F32)],
        compiler_params=_params("arbitrary"), name="nsa_sample",
    )(page_table.reshape(-1), z_nsa, tab, win, w1c, pe2, w2c, cover, pj, hr, cache_cmp, cache_slc)


def _moba_sample_kernel(pt_ref, z_ref, tab_ref, kv_hbm, o_ref, row_ref, buf, s_s, p_s, sem,
                        *, base, n_pages, pages_per_block, topk):
    bi = pl.program_id(0)
    slot = lax.rem(bi, 2)

    def copy(seq, sl, p):
        pg = pt_ref[seq * n_pages + p] + base
        return pltpu.make_async_copy(kv_hbm.at[pg], buf.at[sl, p], sem.at[sl])

    @pl.when(bi == 0)
    def _():
        for p in range(n_pages):
            copy(0, 0, p).start()

    @pl.when(bi + 1 < pl.num_programs(0))
    def _():
        for p in range(n_pages):
            copy(bi + 1, 1 - slot, p).start()

    z = z_ref[0]
    tab = tab_ref[...]
    q = z[:, :BRANCH_W] * SCALE
    qr = _rope_heads(q, tab)
    k_new = _rope_heads(z[:, BRANCH_W:2 * BRANCH_W], tab)
    v_new = z[:, 2 * BRANCH_W:3 * BRANCH_W]
    row_ref[0] = jnp.concatenate([k_new, v_new], axis=1)

    for p in range(n_pages):
        copy(bi, slot, p).wait()

    heads = [slice(h * HEAD_DIM, (h + 1) * HEAD_DIM) for h in range(N_HEADS)]
    _page_scores(lambda p, h: buf[slot, p, h], [_column(qr[:, hs]) for hs in heads], s_s, n_pages)
    pi = _iota((n_pages, n_pages), 0)
    pj = _iota((n_pages, n_pages), 1)
    shift = pages_per_block.bit_length() - 1
    same_block = lax.shift_right_logical(pi, shift) == lax.shift_right_logical(pj, shift)
    p_new = []
    for h, hs in enumerate(heads):
        s = s_s[h]
        page_sum = jnp.sum(s, axis=1, keepdims=True)
        page_sum_row = jnp.sum(jnp.where(pi == pj, page_sum, 0.0), axis=0, keepdims=True)
        gate_col = jnp.sum(jnp.where(same_block, page_sum_row, 0.0), axis=1, keepdims=True)
        gate_row = jnp.sum(jnp.where(same_block, page_sum, 0.0), axis=0, keepdims=True)
        beats = ((gate_row > gate_col) | ((gate_row == gate_col) & (pj < pi))) & jnp.logical_not(same_block)
        picked = jnp.sum(beats.astype(F32), axis=1, keepdims=True) < pages_per_block * topk - 0.5
        s_new = jnp.sum(qr[:, hs] * k_new[:, hs], axis=1, keepdims=True)
        p_s[h], pn = _softmax_with_new(s, picked, s_new)
        p_new.append(pn)
    cols = _page_values(lambda p, h: buf[slot, p, N_HEADS + h], p_s, n_pages)
    outs = [_to_row(cols[h]) + p_new[h] * v_new[:, hs] for h, hs in enumerate(heads)]
    o_ref[0] = jnp.concatenate(outs, axis=1).astype(o_ref.dtype)


def moba_sample(page_table, z_moba, tab, cache, layer, n_pool, page):
    b = z_moba.shape[0]
    n_pages = page_table.shape[1]
    past = n_pages * page
    assert past % MOBA_BLOCK == 0 and MOBA_BLOCK % page == 0
    ppb = MOBA_BLOCK // page
    assert ppb & (ppb - 1) == 0
    kern = functools.partial(_moba_sample_kernel, base=layer * n_pool, n_pages=n_pages, pages_per_block=ppb,
                             topk=min(MOBA_TOPK, past // MOBA_BLOCK + 1))
    row = lambda i, pt: (i, 0, 0)
    grid_spec = pltpu.PrefetchScalarGridSpec(
        num_scalar_prefetch=1, grid=(b,),
        in_specs=[pl.BlockSpec((1, 1, z_moba.shape[2]), row),
                  pl.BlockSpec(tab.shape, lambda i, pt: (0, 0, 0)),
                  pl.BlockSpec(memory_space=pl.ANY)],
        out_specs=[pl.BlockSpec((1, 1, BRANCH_W), row), pl.BlockSpec((1, 1, 2 * BRANCH_W), row)],
        scratch_shapes=[pltpu.VMEM((2, n_pages, 2 * N_HEADS, HEAD_DIM, page), F32),
                        pltpu.VMEM((N_HEADS, n_pages, page), F32),
                        pltpu.VMEM((N_HEADS, n_pages, page), F32),
                        pltpu.SemaphoreType.DMA((2,))])
    return pl.pallas_call(
        kern, grid_spec=grid_spec,
        out_shape=[jax.ShapeDtypeStruct((b, 1, BRANCH_W), BF16),
                   jax.ShapeDtypeStruct((b, 1, 2 * BRANCH_W), F32)],
        compiler_params=_params("arbitrary"), name="moba_sample",
    )(page_table.reshape(-1), z_moba, tab, cache)


def _rope_table(pos):
    inv = ROPE_THETA ** (-jnp.arange(ROT_HALF, dtype=F32) / ROT_HALF)
    ang = pos.astype(F32)[:, None] * inv[None, :]
    cos, sin = jnp.cos(ang), jnp.sin(ang)
    t = pos.shape[0]
    ones = jnp.ones((t, HEAD_DIM - 2 * ROT_HALF), F32)
    zeros = jnp.zeros((t, HEAD_DIM - 2 * ROT_HALF), F32)
    z8 = jnp.zeros((t, ROT_HALF), F32)
    c = jnp.concatenate([cos, cos, ones], axis=1)
    sa = jnp.concatenate([-sin, z8, zeros], axis=1)
    sb = jnp.concatenate([z8, sin, zeros], axis=1)
    return jnp.stack([jnp.tile(c, (1, 2)), jnp.tile(sa, (1, 2)), jnp.tile(sb, (1, 2))], axis=0)


def _cover_matrix(n_rows, n_cmp, n_slc, width):
    starts = np.arange(n_rows) * CMP_STRIDE
    blk = np.arange(width) * SLC_BLOCK
    cover = (starts[:, None] < blk[None, :] + SLC_BLOCK) & (starts[:, None] + CMP_BLOCK > blk[None, :])
    cover &= (np.arange(n_rows)[:, None] < n_cmp) & (np.arange(width)[None, :] < n_slc)
    return jnp.asarray(cover, BF16)


def _expand_matrix(n_blocks, block, n_keys):
    e = np.arange(n_blocks)[:, None] == (np.arange(n_keys)[None, :] // block)
    return jnp.asarray(e, BF16)


def _round_up(n, m):
    return -(-n // m) * m


COL_NSA, COL_MOBA, COL_CROSS, COL_RWKV, COL_MERGE = 0, 768, 1536, 2304, 4096


def _pack_w_in(w_in):
    def zeros(n):
        return jnp.zeros(w_in.shape[:2] + (n,), w_in.dtype)

    parts = [w_in[..., 0:652], zeros(NSA_PAD - 652), w_in[..., 1708:2476], w_in[..., 2476:2732],
             zeros(COL_RWKV - COL_CROSS - BRANCH_W), w_in[..., 652:1708], zeros(RWKV_PAD - RWKV_IN),
             zeros(COL_MERGE - COL_RWKV - RWKV_PAD), w_in[..., 2732:]]
    return jnp.concatenate(parts, axis=-1).astype(BF16)


def _layer_weights(l, nsa_pe, nsa_phi_w1, nsa_phi_w2, rwkv_mu, rwkv_w0, rwkv_w2, rwkv_a0, rwkv_a2,
                   rwkv_g2, rwkv_k_k, rwkv_k_a, rwkv_r_k, rwkv_ln_w, rwkv_ln_b):
    w = {}
    w1 = nsa_phi_w1[l]
    hidden = w1.shape[2]
    w1r = w1.reshape(2, 2, CMP_STRIDE, HEAD_DIM, hidden)
    w1c = jnp.zeros((2, CMP_STRIDE, 2, HEAD_DIM, 2, hidden), F32)
    w1c = w1c.at[:, :, 0, :, 0, :].set(w1r[0]).at[:, :, 1, :, 1, :].set(w1r[1])
    w1c = w1c.reshape(2, CMP_STRIDE * 2 * HEAD_DIM, 2 * hidden)
    w['w1c'] = jnp.concatenate([w1c[0], w1c[1]], axis=1).astype(BF16)
    pe = nsa_pe[l].reshape(2, CMP_STRIDE * 2 * HEAD_DIM)
    w['pe2'] = jnp.pad(pe, ((0, 6), (0, 0)))
    w2 = nsa_phi_w2[l]
    w2c = jnp.zeros((2, hidden, 2, HEAD_DIM), F32).at[0, :, 0, :].set(w2[0]).at[1, :, 1, :].set(w2[1])
    w['w2c'] = w2c.reshape(2 * hidden, 2 * HEAD_DIM).astype(BF16)
    vec = jnp.zeros((8, RWKV_PAD), F32)
    vec = vec.at[0, :RWKV_IN].set(rwkv_mu[l])
    vec = vec.at[1, 0:256].set(rwkv_w0[l]).at[1, 256:512].set(rwkv_a0[l])
    for i, p in enumerate((rwkv_k_k, rwkv_k_a, rwkv_r_k, rwkv_ln_w, rwkv_ln_b)):
        vec = vec.at[2 + i, 0:256].set(p[l])
    w['rwkv_vec'] = vec
    wa = jnp.zeros((128, 512), F32).at[0:64, 0:256].set(rwkv_w2[l]).at[64:128, 256:512].set(rwkv_a2[l])
    w['rwkv_wa'] = wa.astype(BF16)
    w['rwkv_g2'] = jnp.pad(rwkv_g2[l], ((0, 256 - rwkv_g2.shape[1]), (0, 0))).astype(BF16)
    return w


def _project(h, w_all, l, d):
    z_nsa = mm(h, w_all, l, COL_NSA, NSA_PAD)
    z_rwkv = mm(h, w_all, l, COL_RWKV, RWKV_PAD)
    z_moba = mm(h, w_all, l, COL_MOBA, 3 * BRANCH_W)
    z_cross = mm(h, w_all, l, COL_CROSS, BRANCH_W)
    gates = mm(h, w_all, l, COL_MERGE, 4 * d, out_dtype=BF16, gate=True)
    return z_nsa, z_rwkv, z_moba, z_cross, gates


def _mixers_prompt(h, w_all, l, w, tab, mem_kv, consts, b, t):
    z_nsa, z_rwkv, z_moba, z_cross, gates = _project(h, w_all, l, h.shape[1])
    z_nsa = z_nsa.reshape(b, t, NSA_PAD)
    z_rwkv = z_rwkv.reshape(b, t, RWKV_PAD)
    z_moba = z_moba.reshape(b, t, 3 * BRANCH_W)
    z_cross = z_cross.reshape(b, t, BRANCH_W)

    new_cmp = z_nsa[:, :, 256:384]
    new_slc, new_win = nsa_rope(z_nsa, tab)
    cmp_kv = compress(new_cmp.reshape(b, t // CMP_STRIDE, CMP_STRIDE * LANES), w['w1c'], w['pe2'], w['w2c'])
    o_nsa = nsa_attn(z_nsa, tab, cmp_kv, new_slc, new_win, consts['cover_p'], consts['n_cmp_p'],
                     consts['n_slc_p'])

    zero_shift = jnp.zeros((b, 1, RWKV_PAD), F32)
    zero_state = jnp.zeros((b, N_HEADS, HEAD_DIM, HEAD_DIM), F32)
    o_rwkv, new_state = rwkv_seq(z_rwkv, zero_shift, zero_state, w['rwkv_vec'], w['rwkv_wa'], w['rwkv_g2'],
                                 consts['bd'], consts['tri'])
    new_shift = z_rwkv[:, t - 1, :RWKV_IN]

    new_moba, means = moba_rope(z_moba, tab)
    o_moba = moba_attn(z_moba, tab, new_moba, means.reshape(b, -1, BRANCH_W))
    o_cross = cross_attn(z_cross, mem_kv)

    outs = [o.reshape(b * t, BRANCH_W) for o in (o_nsa, o_rwkv, o_moba, o_cross)]
    win_keep = min(WINDOW, t)
    caches = (new_cmp.reshape(b, t, 2, HEAD_DIM), new_slc.reshape(b, t, 2, HEAD_DIM),
              new_win[:, t - win_keep:].reshape(b, win_keep, 2, HEAD_DIM),
              new_moba.reshape(b, t, 2, N_HEADS, HEAD_DIM), new_state, new_shift)
    return outs, gates, caches


def _mixers_sample(h, w_all, l, w, tab, page_table, caches_in, consts, b):
    cmp_pages, slc_pages, cache_win, moba_pages, cache_mem, state_rwkv, state_shift, n_pool, page = caches_in
    z_nsa, z_rwkv, z_moba, z_cross, gates = _project(h, w_all, l, h.shape[1])
    z_nsa = z_nsa.reshape(b, 1, NSA_PAD)
    z_rwkv = z_rwkv.reshape(b, 1, RWKV_PAD)
    z_moba = z_moba.reshape(b, 1, 3 * BRANCH_W)
    z_cross = z_cross.reshape(b, 1, BRANCH_W)

    win = cache_win[l].reshape(b, -1, LANES)
    o_nsa, slc_row, win_row = nsa_sample(
        page_table, z_nsa, tab, win, w['w1c'], w['pe2'], w['w2c'], consts['cover_s'], consts['pj_s'],
        consts['hr_s'], cmp_pages, slc_pages, l, n_pool, page, consts['n_cmp_s'], consts['n_slc_s'])
    new_cmp = z_nsa[:, :, 256:384]
    full_win = jnp.concatenate([win, win_row], axis=1)
    keep = min(WINDOW, full_win.shape[1])
    new_buf = full_win[:, full_win.shape[1] - keep:]

    shift = jnp.pad(state_shift[l], ((0, 0), (0, RWKV_PAD - RWKV_IN))).reshape(b, 1, RWKV_PAD)
    o_rwkv, new_state = rwkv_step(z_rwkv, shift, state_rwkv[l], w['rwkv_vec'], w['rwkv_wa'], w['rwkv_g2'],
                                  consts['bd'])
    new_shift = z_rwkv[:, 0, :RWKV_IN]

    o_moba, moba_row = moba_sample(page_table, z_moba, tab, moba_pages, l, n_pool, page)
    o_cross = cross_attn(z_cross, cache_mem[l].reshape(b, -1, 2 * BRANCH_W))

    outs = [o.reshape(b, BRANCH_W) for o in (o_nsa, o_rwkv, o_moba, o_cross)]
    caches = (new_cmp.reshape(b, 1, 2, HEAD_DIM), slc_row.reshape(b, 1, 2, HEAD_DIM),
              new_buf.reshape(b, keep, 2, HEAD_DIM), moba_row.reshape(b, 1, 2, N_HEADS, HEAD_DIM),
              new_state, new_shift)
    return outs, gates, caches


def kernel(x_prompt, x_sample, cache_nsa_cmp, cache_nsa_slc, cache_nsa_win, cache_moba, cache_mem, state_rwkv, state_shift, page_table, mem_prompt, g_pre_mix, g_post_mix, g_pre_ffn, g_post_ffn, g_mem, w_in, nsa_pe, nsa_phi_w1, nsa_phi_w2, rwkv_mu, rwkv_w0, rwkv_w2, rwkv_a0, rwkv_a2, rwkv_g2, rwkv_k_k, rwkv_k_a, rwkv_r_k, rwkv_ln_w, rwkv_ln_b, w_mem_kv, w_branch, w_out, w_ffn_gate, w_ffn_up, w_ffn_down):
    bp, t, d = x_prompt.shape
    bs = x_sample.shape[0]
    assert x_sample.shape[1] == 1
    depth = w_in.shape[0]
    page = cache_nsa_cmp.shape[2]
    past = page_table.shape[1] * page
    n_mem = mem_prompt.shape[1]
    assert t % MOBA_BLOCK == 0 and past % MOBA_BLOCK == 0 and page % SLC_BLOCK == 0

    n_cmp_p = (t - CMP_BLOCK) // CMP_STRIDE + 1
    n_slc_p = -(-t // SLC_BLOCK)
    n_cmp_s = (past + 1 - CMP_BLOCK) // CMP_STRIDE + 1
    n_slc_s = -(-(past + 1) // SLC_BLOCK)
    consts = {
        'n_cmp_p': n_cmp_p, 'n_slc_p': n_slc_p, 'n_cmp_s': n_cmp_s, 'n_slc_s': n_slc_s,
        'cover_p': _cover_matrix(t // CMP_STRIDE, n_cmp_p, n_slc_p, _round_up(n_slc_p, 8)).T,
        'cover_s': _cover_matrix(past // CMP_STRIDE, n_cmp_s, n_slc_s, _round_up(n_slc_s, LANES)),
        'pj_s': jnp.asarray(np.arange(_round_up(n_slc_s, LANES))[None, :] // (page // SLC_BLOCK)
                            == np.arange(past // page)[:, None], F32),
        'hr_s': jnp.asarray(np.arange(_round_up(n_slc_s, LANES))[:, None] % (page // SLC_BLOCK)
                            == np.arange(page)[None, :] // SLC_BLOCK, BF16),
        'bd': jnp.asarray(np.arange(BRANCH_W)[:, None] // HEAD_DIM == np.arange(BRANCH_W)[None, :] // HEAD_DIM, BF16),
        'tri': jnp.asarray(np.tril(np.ones((RWKV_CHUNK, RWKV_CHUNK), np.float32))),
    }
    tab_p = _rope_table(jnp.arange(t, dtype=jnp.int32))
    tab_s = _rope_table(jnp.full((1,), past, jnp.int32))

    xp = x_prompt.reshape(bp * t, d)
    xs = x_sample.reshape(bs, d)
    mem_flat = mem_prompt.reshape(bp * n_mem, d)
    n_pool = cache_nsa_cmp.shape[1]
    cmp_pages = jnp.transpose(cache_nsa_cmp, (0, 1, 3, 4, 2)).reshape(depth * n_pool, 2 * HEAD_DIM, page)
    slc_pages = jnp.transpose(cache_nsa_slc, (0, 1, 3, 4, 2)).reshape(depth * n_pool, 2, HEAD_DIM, page)
    moba_pages = jnp.transpose(cache_moba, (0, 1, 3, 4, 5, 2)).reshape(depth * n_pool, 2 * N_HEADS, HEAD_DIM, page)
    sample_caches = (cmp_pages, slc_pages, cache_nsa_win, moba_pages, cache_mem, state_rwkv, state_shift,
                     n_pool, page)
    w_all = _pack_w_in(w_in)
    w_mem, w_br, w_o = w_mem_kv.astype(BF16), w_branch.astype(BF16), w_out.astype(BF16)
    w_fg, w_fu, w_fd = w_ffn_gate.astype(BF16), w_ffn_up.astype(BF16), w_ffn_down.astype(BF16)
    outs_p, outs_s, mem_out = [], [], []
    for l in range(depth):
        w = _layer_weights(l, nsa_pe, nsa_phi_w1, nsa_phi_w2, rwkv_mu, rwkv_w0, rwkv_w2, rwkv_a0,
                           rwkv_a2, rwkv_g2, rwkv_k_k, rwkv_k_a, rwkv_r_k, rwkv_ln_w, rwkv_ln_b)
        mem_kv = mm(rms_cast(mem_flat, g_mem[l]), w_mem, l, 0, 2 * BRANCH_W).reshape(bp, n_mem, 2 * BRANCH_W)
        mem_out.append(mem_kv.reshape(bp, n_mem, 2, N_HEADS, HEAD_DIM))

        o_p, gates_p, new_p = _mixers_prompt(rms_cast(xp, g_pre_mix[l]), w_all, l, w, tab_p, mem_kv, consts, bp, t)
        xp = merge(xp, o_p, gates_p, w_br, w_o, g_post_mix[l], l)
        xp = ffn(xp, g_pre_ffn[l], w_fg, w_fu, w_fd, g_post_ffn[l], l)

        o_s, gates_s, new_s = _mixers_sample(rms_cast(xs, g_pre_mix[l]), w_all, l, w, tab_s, page_table,
                                             sample_caches, consts, bs)
        xs = merge(xs, o_s, gates_s, w_br, w_o, g_post_mix[l], l)
        xs = ffn(xs, g_pre_ffn[l], w_fg, w_fu, w_fd, g_post_ffn[l], l)
        outs_p.append(new_p)
        outs_s.append(new_s)

    def stack(outs, i):
        return jnp.stack([o[i] for o in outs], axis=0)

    return (xp.reshape(bp, t, d), xs.reshape(bs, 1, d),
            stack(outs_p, 0), stack(outs_s, 0), stack(outs_p, 1), stack(outs_s, 1),
            stack(outs_p, 2), stack(outs_s, 2), stack(outs_p, 3), stack(outs_s, 3),
            jnp.stack(mem_out, axis=0),
            stack(outs_p, 4), stack(outs_s, 4), stack(outs_p, 5), stack(outs_s, 5))
```

```python
import functools

import jax
import jax.numpy as jnp
import numpy as np
from jax import lax
from jax.experimental import pallas as pl
from jax.experimental.pallas import tpu as pltpu

F32 = jnp.float32
BF16 = jnp.bfloat16

HEAD_DIM = 64
N_HEADS = 4
BRANCH_W = N_HEADS * HEAD_DIM
ROT_HALF = 8
ROPE_THETA = 500000.0
CMP_BLOCK = 32
CMP_STRIDE = 16
SLC_BLOCK = 64
N_SELECT = 16
WINDOW = 512
MOBA_BLOCK = 256
MOBA_TOPK = 3
RWKV_IN = 1056
RWKV_PAD = 1152
NSA_PAD = 768
RWKV_GN_EPS = 64e-5
RWKV_CHUNK = 64
NEG = -1e30
SCALE = HEAD_DIM ** -0.5
LANES = 128
ATTN_KEY_TILE = 256
FFN_TILE = 1408
VMEM_LIMIT = 48 * 1024 * 1024


def _iota(shape, dim):
    return lax.broadcasted_iota(jnp.int32, shape, dim)


def _dot(a, b):
    return lax.dot_general(a, b, (((1,), (0,)), ((), ())), preferred_element_type=F32)


def _dot_nt(a, b):
    return lax.dot_general(a, b, (((1,), (1,)), ((), ())), preferred_element_type=F32)


def _dot_tn(a, b):
    return lax.dot_general(a, b, (((0,), (0,)), ((), ())), preferred_element_type=F32)


def _split(x):
    hi = x.astype(BF16)
    lo = (x - hi.astype(F32)).astype(BF16)
    return hi, lo


def _dot3(a, b, dot=_dot):
    ah, al = _split(a)
    bh, bl = _split(b)
    return dot(ah, bh) + dot(al, bh) + dot(ah, bl)


def _dot2_exact_rhs(a, b_bf16):
    ah, al = _split(a)
    return _dot(ah, b_bf16) + _dot(al, b_bf16)


def _sigmoid(x):
    return 1.0 / (1.0 + jnp.exp(-x))


def _rms(x, g, eps=1e-6):
    return x * lax.rsqrt(jnp.mean(x * x, axis=-1, keepdims=True) + eps) * g


def _msoftmax(s, mask):
    s = jnp.where(mask, s, NEG)
    p = jnp.where(mask, jnp.exp(s - jnp.max(s, axis=-1, keepdims=True)), 0.0)
    return p / jnp.maximum(jnp.sum(p, axis=-1, keepdims=True), 1e-30)


def _flash_step(q, k, v, mask, m, l, acc):
    s = jnp.where(mask, _dot_nt(q, k), NEG)
    m_new = jnp.maximum(m, jnp.max(s, axis=-1, keepdims=True))
    p = jnp.where(mask, jnp.exp(s - m_new), 0.0)
    alpha = jnp.exp(m - m_new)
    l = alpha * l + jnp.sum(p, axis=-1, keepdims=True)
    acc = alpha * acc + _dot(p.astype(BF16), v)
    return m_new, l, acc


def _with_ones(v_t):
    return jnp.concatenate([v_t, jnp.ones((8, v_t.shape[1]), v_t.dtype)], axis=0)


def _flash_heads_t(q_ts, ks, v_ts, biases, valid, carry):
    ss = []
    for q_t, k, bias in zip(q_ts, ks, biases):
        s = _dot(k, q_t)
        s = s if bias is None else s + bias
        ss.append(s if valid is None else jnp.where(valid, s, NEG))
    m_news = [jnp.maximum(c[0], jnp.max(s, axis=0, keepdims=True)) for c, s in zip(carry, ss)]
    pvs = [_dot(v_t, jnp.exp(s - m_new).astype(BF16)) for v_t, s, m_new in zip(v_ts, ss, m_news)]
    return tuple((m_new, jnp.exp(c[0] - m_new) * c[1] + pv) for c, m_new, pv in zip(carry, m_news, pvs))


def _flash_finish(carry):
    acc = carry[1]
    d = acc.shape[0] - 8
    return acc[:d] / jnp.maximum(acc[d:d + 1], 1e-30)


def _rope(x, tab, half_only=False):
    c, sa, sb = tab[0], tab[1], tab[2]
    if half_only:
        first = _iota(x.shape, 1) < HEAD_DIM
        c = jnp.where(first, c, 1.0)
        sa = jnp.where(first, sa, 0.0)
        sb = jnp.where(first, sb, 0.0)
    return x * c + pltpu.roll(x, LANES - ROT_HALF, 1) * sa + pltpu.roll(x, ROT_HALF, 1) * sb


def _rope_heads(x, tab):
    return jnp.concatenate([_rope(x[:, :LANES], tab), _rope(x[:, LANES:], tab)], axis=1)


def _rank_lt(score, n_real, k):
    lane = _iota(score.shape, 1)
    cnt = jnp.zeros(score.shape, jnp.int32)
    for i in range(n_real):
        ci = score[:, i:i + 1]
        beats = (ci > score) | ((ci == score) & (lane > i))
        cnt = cnt + beats.astype(jnp.int32)
    return cnt < k


def _pick_tile(n, target, mult=8):
    if n <= target:
        return n
    for t in range(target, 0, -1):
        if n % t == 0 and t % mult == 0:
            return t
    return n


def _params(*sem):
    return pltpu.CompilerParams(dimension_semantics=sem, vmem_limit_bytes=VMEM_LIMIT)


def _rms_cast_kernel(x_ref, g_ref, o_ref):
    o_ref[...] = _rms(x_ref[...], g_ref[...]).astype(BF16)


def rms_cast(x, g):
    m, d = x.shape
    tm = _pick_tile(m, 1024, 16)
    return pl.pallas_call(
        _rms_cast_kernel, grid=(m // tm,),
        in_specs=[pl.BlockSpec((tm, d), lambda i: (i, 0)), pl.BlockSpec((1, d), lambda i: (0, 0))],
        out_specs=pl.BlockSpec((tm, d), lambda i: (i, 0)),
        out_shape=jax.ShapeDtypeStruct((m, d), BF16),
        compiler_params=_params("parallel"), name="rms_cast",
    )(x, g.reshape(1, d))


def _mm_kernel(a_ref, b_ref, o_ref, *, gate):
    y = _dot(a_ref[...], b_ref[...])
    o_ref[...] = (_sigmoid(y) if gate else y).astype(o_ref.dtype)


def mm(a, w, layer, col0, n, out_dtype=F32, gate=False):
    m, k = a.shape
    tm = _pick_tile(m, 1024, 16)
    tn = n if n <= 1536 else _pick_tile(n, 1024, LANES)
    assert col0 % tn == 0 and n % tn == 0
    c0 = col0 // tn
    return pl.pallas_call(
        functools.partial(_mm_kernel, gate=gate), grid=(m // tm, n // tn),
        in_specs=[pl.BlockSpec((tm, k), lambda i, j: (i, 0)),
                  pl.BlockSpec((None, k, tn), lambda i, j: (layer, 0, c0 + j))],
        out_specs=pl.BlockSpec((tm, tn), lambda i, j: (i, j)),
        out_shape=jax.ShapeDtypeStruct((m, n), out_dtype),
        compiler_params=_params("parallel", "parallel"), name="mm",
    )(a, w)


def _ffn_kernel(x_ref, g1_ref, wg_ref, wu_ref, wd_ref, g2_ref, o_ref, h_ref, acc_ref):
    f = pl.program_id(1)

    @pl.when(f == 0)
    def _():
        h_ref[...] = _rms(x_ref[...], g1_ref[...]).astype(BF16)
        acc_ref[...] = jnp.zeros_like(acc_ref)

    h = h_ref[...]
    gate = _dot(h, wg_ref[...])
    up = _dot(h, wu_ref[...])
    act = (gate * _sigmoid(gate) * up).astype(BF16)
    acc_ref[...] += _dot(act, wd_ref[...])

    @pl.when(f == pl.num_programs(1) - 1)
    def _():
        o_ref[...] = x_ref[...] + _rms(acc_ref[...], g2_ref[...])


def ffn(x, g_pre, wg, wu, wd, g_post, layer):
    m, d = x.shape
    dff = wg.shape[2]
    tm = _pick_tile(m, 512, 16)
    tf = _pick_tile(dff, FFN_TILE, LANES)
    return pl.pallas_call(
        _ffn_kernel, grid=(m // tm, dff // tf),
        in_specs=[pl.BlockSpec((tm, d), lambda i, f: (i, 0)),
                  pl.BlockSpec((1, d), lambda i, f: (0, 0)),
                  pl.BlockSpec((None, d, tf), lambda i, f: (layer, 0, f)),
                  pl.BlockSpec((None, d, tf), lambda i, f: (layer, 0, f)),
                  pl.BlockSpec((None, tf, d), lambda i, f: (layer, f, 0)),
                  pl.BlockSpec((1, d), lambda i, f: (0, 0))],
        out_specs=pl.BlockSpec((tm, d), lambda i, f: (i, 0)),
        out_shape=jax.ShapeDtypeStruct((m, d), F32),
        scratch_shapes=[pltpu.VMEM((tm, d), BF16), pltpu.VMEM((tm, d), F32)],
        compiler_params=_params("parallel", "arbitrary"), name="ffn",
    )(x, g_pre.reshape(1, d), wg, wu, wd, g_post.reshape(1, d))


def _merge_kernel(x_ref, o0_ref, o1_ref, o2_ref, o3_ref, gate_ref, wb_ref, wo_ref, g_ref, out_ref):
    d = x_ref.shape[1]
    merged = None
    for b, o_ref in enumerate((o0_ref, o1_ref, o2_ref, o3_ref)):
        term = gate_ref[:, b * d:(b + 1) * d].astype(F32) * _dot(o_ref[...], wb_ref[b])
        merged = term if merged is None else merged + term
    y = _dot(merged.astype(BF16), wo_ref[...])
    out_ref[...] = x_ref[...] + _rms(y, g_ref[...])


def merge(x, outs, gates, w_branch, w_out, g_post, layer):
    m, d = x.shape
    bw = outs[0].shape[1]
    tm = _pick_tile(m, 512, 16)
    row = lambda i: (i, 0)
    return pl.pallas_call(
        _merge_kernel, grid=(m // tm,),
        in_specs=[pl.BlockSpec((tm, d), row)] + [pl.BlockSpec((tm, bw), row)] * 4 + [
            pl.BlockSpec((tm, 4 * d), row),
            pl.BlockSpec((None, 4, bw, d), lambda i: (layer, 0, 0, 0)),
            pl.BlockSpec((None, d, d), lambda i: (layer, 0, 0)),
            pl.BlockSpec((1, d), lambda i: (0, 0))],
        out_specs=pl.BlockSpec((tm, d), row),
        out_shape=jax.ShapeDtypeStruct((m, d), F32),
        compiler_params=_params("parallel"), name="merge",
    )(x, *outs, gates, w_branch, w_out, g_post.reshape(1, d))


def _nsa_rope_kernel(z_ref, tab_ref, slc_ref, win_ref):
    z = z_ref[0]
    tab = tab_ref[...]
    slc_ref[0] = _rope(z[:, 384:512], tab, half_only=True)
    win_ref[0] = _rope(z[:, 512:640], tab, half_only=True)


def nsa_rope(z_nsa, tab):
    b, t, w = z_nsa.shape
    tt = _pick_tile(t, 512)
    out = jax.ShapeDtypeStruct((b, t, LANES), F32)
    return pl.pallas_call(
        _nsa_rope_kernel, grid=(b, t // tt),
        in_specs=[pl.BlockSpec((1, tt, w), lambda i, j: (i, j, 0)),
                  pl.BlockSpec((3, tt, LANES), lambda i, j: (0, j, 0))],
        out_specs=[pl.BlockSpec((1, tt, LANES), lambda i, j: (i, j, 0))] * 2,
        out_shape=[out, out],
        compiler_params=_params("parallel", "parallel"), name="nsa_rope",
    )(z_nsa, tab)


def _moba_rope_kernel(z_ref, tab_ref, kv_ref, mean_ref):
    z = z_ref[0]
    kr = _rope_heads(z[:, BRANCH_W:2 * BRANCH_W], tab_ref[...])
    kv_ref[0] = jnp.concatenate([kr, z[:, 2 * BRANCH_W:3 * BRANCH_W]], axis=1)
    mean_ref[0, 0] = jnp.sum(kr, axis=0, keepdims=True) * (1.0 / MOBA_BLOCK)


def moba_rope(z_moba, tab):
    b, t, w = z_moba.shape
    nb = t // MOBA_BLOCK
    return pl.pallas_call(
        _moba_rope_kernel, grid=(b, nb),
        in_specs=[pl.BlockSpec((1, MOBA_BLOCK, w), lambda i, j: (i, j, 0)),
                  pl.BlockSpec((3, MOBA_BLOCK, LANES), lambda i, j: (0, j, 0))],
        out_specs=[pl.BlockSpec((1, MOBA_BLOCK, 2 * BRANCH_W), lambda i, j: (i, j, 0)),
                   pl.BlockSpec((1, 1, 1, BRANCH_W), lambda i, j: (i, j, 0, 0))],
        out_shape=[jax.ShapeDtypeStruct((b, t, 2 * BRANCH_W), F32),
                   jax.ShapeDtypeStruct((b, nb, 1, BRANCH_W), F32)],
        compiler_params=_params("parallel", "parallel"), name="moba_rope",
    )(z_moba, tab)


def _compress(g, w1_ref, pe_ref, w2_ref):
    r = g.shape[0]
    ab = _dot(g.astype(BF16), w1_ref[...])
    pe = _dot(pe_ref[...].astype(BF16), w1_ref[...])
    half = ab.shape[1] // 2
    nxt = pltpu.roll(ab[:, half:], r - 1, 0)
    pre = ab[:, :half] + nxt + pe[0:1, :half] + pe[1:2, half:]
    hid = pre * _sigmoid(pre)
    return _dot(hid.astype(BF16), w2_ref[...])


def _compress_kernel(g_ref, w1_ref, pe_ref, w2_ref, o_ref):
    o_ref[0] = _compress(g_ref[0], w1_ref, pe_ref, w2_ref)


def compress(g, w1c, pe2, w2c):
    b, r, w = g.shape
    const = lambda i: (0, 0)
    return pl.pallas_call(
        _compress_kernel, grid=(b,),
        in_specs=[pl.BlockSpec((1, r, w), lambda i: (i, 0, 0)),
                  pl.BlockSpec(w1c.shape, const), pl.BlockSpec(pe2.shape, const),
                  pl.BlockSpec(w2c.shape, const)],
        out_specs=pl.BlockSpec((1, r, LANES), lambda i: (i, 0, 0)),
        out_shape=jax.ShapeDtypeStruct((b, r, LANES), F32),
        compiler_params=_params("parallel"), name="nsa_compress",
    )(g, w1c, pe2, w2c)


def _rank_lt_rows(score, n_real, k):
    row = _iota(score.shape, 0)
    cnt = jnp.zeros(score.shape, jnp.int32)
    for i in range(n_real):
        ri = score[i:i + 1, :]
        beats = (ri > score) | ((ri == score) & (row > i))
        cnt = cnt + beats.astype(jnp.int32)
    return cnt < k


def _select_blocks(imp, t_pos, n_slc, k_sel):
    j = _iota(imp.shape, 1)
    own = lax.shift_right_logical(t_pos, 6)
    causal = (j * SLC_BLOCK <= t_pos) & (j < n_slc)
    forced = (j == 0) | (j == own) | (j == own - 1)
    score = jnp.where(forced, -NEG, jnp.where(causal, imp, NEG))
    return _rank_lt(score, n_slc, k_sel) & causal


def _nsa_attn_kernel(z_ref, tab_ref, cmp_ref, slc_ref, win_ref, cover_ref, o_ref, sel_s,
                     *, tq, tk, n_cmp, n_slc, k_sel):
    qi = pl.program_id(1)
    z = z_ref[0]
    q = z[:, :BRANCH_W] * SCALE
    qr = _rope_heads(q, tab_ref[...])
    heads = [slice(h * HEAD_DIM, (h + 1) * HEAD_DIM) for h in range(N_HEADS)]

    cmpkv = cmp_ref[0]
    ck = cmpkv[:, :HEAD_DIM].astype(BF16)
    cv = cmpkv[:, HEAD_DIM:].astype(BF16)
    t_row = qi * tq + _iota((1, tq), 1)
    n = _iota((cmpkv.shape[0], 1), 0)
    cmask = (n * CMP_STRIDE + (CMP_BLOCK - 1) <= t_row) & (n < n_cmp)
    o_cmp, psum = [], None
    for hs in heads:
        s = jnp.where(cmask, _dot_nt(ck, q[:, hs].astype(BF16)), NEG)
        p = jnp.where(cmask, jnp.exp(s - jnp.max(s, axis=0, keepdims=True)), 0.0)
        p = p / jnp.maximum(jnp.sum(p, axis=0, keepdims=True), 1e-30)
        o_cmp.append(_dot_tn(cv, p.astype(BF16)))
        psum = p if psum is None else psum + p
    p_hi, p_lo = _split(psum)
    imp = _dot(cover_ref[...], p_hi) + _dot(cover_ref[...], p_lo)
    j = _iota(imp.shape, 0)
    own = lax.shift_right_logical(t_row, 6)
    causal = (j * SLC_BLOCK <= t_row) & (j < n_slc)
    forced = (j == 0) | (j == own) | (j == own - 1)
    score = jnp.where(forced, -NEG, jnp.where(causal, imp, NEG))
    picked = _rank_lt_rows(score, n_slc, k_sel) & causal
    sel_s[...] = jnp.where(picked, 0.0, NEG)

    qr_t = qr.T.astype(BF16)
    qrb = [qr_t[hs, :] for hs in heads]
    init = tuple((jnp.full((1, tq), NEG, F32), jnp.zeros((HEAD_DIM + 8, tq), F32)) for _ in heads)
    blocks_per_tile = tk // SLC_BLOCK
    tiles_per_q = tq // tk

    def tile(ref, kt):
        rows = ref[0, pl.ds(pl.multiple_of(kt * tk, tk), tk), :]
        return [rows[:, :HEAD_DIM].astype(BF16)] * N_HEADS, [_with_ones(rows.T[HEAD_DIM:, :].astype(BF16))] * N_HEADS

    def slc_body(kt, carry, diagonal):
        ks, v_ts = tile(slc_ref, kt)
        bias = jnp.concatenate(
            [jnp.broadcast_to(sel_s[pl.ds(kt * blocks_per_tile + u, 1), :], (SLC_BLOCK, tq))
             for u in range(blocks_per_tile)], axis=0)
        valid = (kt * tk + _iota((tk, 1), 0) <= t_row) if diagonal else None
        return _flash_heads_t(qrb, ks, v_ts, [bias] * N_HEADS, valid, carry)

    def win_body(kt, carry, banded):
        ks, v_ts = tile(win_ref, kt)
        d = t_row - (kt * tk + _iota((tk, 1), 0))
        valid = ((d >= 0) & (d <= WINDOW)) if banded else None
        return _flash_heads_t(qrb, ks, v_ts, [None] * N_HEADS, valid, carry)

    first_diag = qi * tiles_per_q
    hi = first_diag + tiles_per_q
    slc = lax.fori_loop(0, first_diag, functools.partial(slc_body, diagonal=False), init)
    slc = lax.fori_loop(first_diag, hi, functools.partial(slc_body, diagonal=True), slc)
    win_lo = jnp.maximum(first_diag - WINDOW // tk, 0)
    win_free = jnp.maximum(first_diag - (WINDOW - tq) // tk, 0)
    win = lax.fori_loop(win_lo, win_free, functools.partial(win_body, banded=True), init)
    win = lax.fori_loop(win_free, first_diag, functools.partial(win_body, banded=False), win)
    win = lax.fori_loop(first_diag, hi, functools.partial(win_body, banded=True), win)

    g = _sigmoid(z[:, 640:640 + LANES]).T
    outs = []
    for h in range(N_HEADS):
        outs.append(g[3 * h:3 * h + 1, :] * o_cmp[h] + g[3 * h + 1:3 * h + 2, :] * _flash_finish(slc[h])
                    + g[3 * h + 2:3 * h + 3, :] * _flash_finish(win[h]))
    o_ref[0] = jnp.concatenate(outs, axis=0).T.astype(o_ref.dtype)


def nsa_attn(z_nsa, tab, cmp_kv, new_slc, new_win, cover, n_cmp, n_slc):
    b, t, w = z_nsa.shape
    tq = _pick_tile(t, 256)
    tk = ATTN_KEY_TILE
    assert tq % tk == 0 and tk % SLC_BLOCK == 0 and WINDOW % tk == 0 and WINDOW >= tq
    kern = functools.partial(_nsa_attn_kernel, tq=tq, tk=tk, n_cmp=n_cmp, n_slc=n_slc,
                             k_sel=min(N_SELECT, n_slc))
    full = lambda i, j: (i, 0, 0)
    const = lambda i, j: (0, 0)
    return pl.pallas_call(
        kern, grid=(b, t // tq),
        in_specs=[pl.BlockSpec((1, tq, w), lambda i, j: (i, j, 0)),
                  pl.BlockSpec((3, tq, LANES), lambda i, j: (0, j, 0)),
                  pl.BlockSpec((1,) + cmp_kv.shape[1:], full),
                  pl.BlockSpec((1, t, LANES), full), pl.BlockSpec((1, t, LANES), full),
                  pl.BlockSpec(cover.shape, const)],
        out_specs=pl.BlockSpec((1, tq, BRANCH_W), lambda i, j: (i, j, 0)),
        out_shape=jax.ShapeDtypeStruct((b, t, BRANCH_W), BF16),
        scratch_shapes=[pltpu.VMEM((cover.shape[0], tq), F32)],
        compiler_params=_params("parallel", "parallel"), name="nsa_attn",
    )(z_nsa, tab, cmp_kv, new_slc, new_win, cover)


def _moba_attn_kernel(z_ref, tab_ref, kv_ref, mean_ref, o_ref, pick_s, *, tq, tk, nb, topk):
    qi = pl.program_id(1)
    q = z_ref[0][:, :BRANCH_W] * SCALE
    qr = _rope_heads(q, tab_ref[...])
    means = mean_ref[0]
    heads = [slice(h * HEAD_DIM, (h + 1) * HEAD_DIM) for h in range(N_HEADS)]

    t_row = qi * tq + _iota((1, tq), 1)
    own = lax.shift_right_logical(t_row, 8)
    past = _iota((means.shape[0], 1), 0) < own
    for h, hs in enumerate(heads):
        gate = _dot3(means[:, hs], qr[:, hs], _dot_nt)
        picked = _rank_lt_rows(jnp.where(past, gate, NEG), nb, topk) & past
        pick_s[h] = jnp.where(picked, 0.0, NEG)

    qr_t = qr.T.astype(BF16)
    qrb = [qr_t[hs, :] for hs in heads]
    init = tuple((jnp.full((1, tq), NEG, F32), jnp.zeros((HEAD_DIM + 8, tq), F32)) for _ in heads)
    tiles_per_block = MOBA_BLOCK // tk

    def body(kt, carry, own_block):
        rows = kv_ref[0, pl.ds(pl.multiple_of(kt * tk, tk), tk), :]
        v_t = rows[:, BRANCH_W:].T.astype(BF16)
        ks = [rows[:, hs].astype(BF16) for hs in heads]
        v_ts = [_with_ones(v_t[hs, :]) for hs in heads]
        if own_block:
            return _flash_heads_t(qrb, ks, v_ts, [None] * N_HEADS, kt * tk + _iota((tk, 1), 0) <= t_row, carry)
        blk = kt // tiles_per_block
        return _flash_heads_t(qrb, ks, v_ts, [pick_s[h, pl.ds(blk, 1), :] for h in range(N_HEADS)], None, carry)

    first_own = qi * (tq // tk)
    res = lax.fori_loop(0, first_own, functools.partial(body, own_block=False), init)
    res = lax.fori_loop(first_own, first_own + tq // tk, functools.partial(body, own_block=True), res)
    o_ref[0] = jnp.concatenate([_flash_finish(r) for r in res], axis=0).T.astype(o_ref.dtype)


def moba_attn(z_moba, tab, new_moba, means):
    b, t, w = z_moba.shape
    assert t % MOBA_BLOCK == 0
    tq = MOBA_BLOCK
    tk = ATTN_KEY_TILE
    assert MOBA_BLOCK % tk == 0
    nb = t // MOBA_BLOCK
    kern = functools.partial(_moba_attn_kernel, tq=tq, tk=tk, nb=nb, topk=min(MOBA_TOPK, nb))
    full = lambda i, j: (i, 0, 0)
    return pl.pallas_call(
        kern, grid=(b, t // tq),
        in_specs=[pl.BlockSpec((1, tq, w), lambda i, j: (i, j, 0)),
                  pl.BlockSpec((3, tq, LANES), lambda i, j: (0, j, 0)),
                  pl.BlockSpec((1, t, 2 * BRANCH_W), full),
                  pl.BlockSpec((1, nb, BRANCH_W), full)],
        out_specs=pl.BlockSpec((1, tq, BRANCH_W), lambda i, j: (i, j, 0)),
        out_shape=jax.ShapeDtypeStruct((b, t, BRANCH_W), BF16),
        scratch_shapes=[pltpu.VMEM((N_HEADS, nb, tq), F32)],
        compiler_params=_params("parallel", "parallel"), name="moba_attn",
    )(z_moba, tab, new_moba, means)


def _cross_attn_kernel(q_ref, mem_ref, o_ref):
    q = q_ref[0] * SCALE
    mem = mem_ref[0]
    outs = []
    for h in range(N_HEADS):
        hs = slice(h * HEAD_DIM, (h + 1) * HEAD_DIM)
        s = _dot_nt(q[:, hs].astype(BF16), mem[:, hs].astype(BF16))
        p = jnp.exp(s - jnp.max(s, axis=-1, keepdims=True))
        p = p / jnp.sum(p, axis=-1, keepdims=True)
        v = mem[:, BRANCH_W + h * HEAD_DIM:BRANCH_W + (h + 1) * HEAD_DIM].astype(BF16)
        outs.append(_dot(p.astype(BF16), v))
    o_ref[0] = jnp.concatenate(outs, axis=1).astype(o_ref.dtype)


def cross_attn(q, mem_kv):
    b, t, w = q.shape
    n_mem = mem_kv.shape[1]
    tq = _pick_tile(t, 512)
    return pl.pallas_call(
        _cross_attn_kernel, grid=(b, t // tq),
        in_specs=[pl.BlockSpec((1, tq, w), lambda i, j: (i, j, 0)),
                  pl.BlockSpec((1, n_mem, 2 * BRANCH_W), lambda i, j: (i, 0, 0))],
        out_specs=pl.BlockSpec((1, tq, BRANCH_W), lambda i, j: (i, j, 0)),
        out_shape=jax.ShapeDtypeStruct((b, t, BRANCH_W), BF16),
        compiler_params=_params("parallel", "parallel"), name="cross_attn",
    )(q, mem_kv)


def _head_sum(x, bd_ref):
    return _dot2_exact_rhs(x, bd_ref[...])


def _rwkv_prep(z, zprev, vec_ref, wa_ref, g2_ref, bd_ref):
    mu = vec_ref[0:1, :]
    zm = z + (zprev - z) * mu
    r = zm[:, 0:256]
    k = zm[:, 256:512]
    v = zm[:, 512:768]
    lora = zm[:, 768:896]
    lora = jnp.where(_iota(lora.shape, 1) < 64, jnp.tanh(lora), lora)
    wa = _dot(lora.astype(BF16), wa_ref[...])
    wl = vec_ref[1:2, 0:256] + wa[:, :256]
    x = -wl
    softplus = jnp.maximum(x, 0.0) + jnp.log(1.0 + jnp.exp(-jnp.abs(x)))
    logdecay = -jnp.exp(-softplus - 0.5)
    a = _sigmoid(vec_ref[1:2, 256:512] + wa[:, 256:])
    g = _dot(_sigmoid(zm[:, 896:1152]).astype(BF16), g2_ref[...])
    kkr = k * vec_ref[2:3, 0:256]
    kk = kkr * lax.rsqrt(jnp.maximum(_head_sum(kkr * kkr, bd_ref), 1e-24))
    k2 = k * (1.0 + (a - 1.0) * vec_ref[3:4, 0:256])
    bonus = _head_sum(r * k2 * vec_ref[4:5, 0:256], bd_ref) * v
    return r, k2, v, kk, kk * a, logdecay, g, bonus


def _rwkv_finish(y, bonus, g, vec_ref, bd_ref):
    mu = _head_sum(y, bd_ref) * (1.0 / HEAD_DIM)
    d = y - mu
    var = _head_sum(d * d, bd_ref) * (1.0 / HEAD_DIM)
    yn = d * lax.rsqrt(var + RWKV_GN_EPS) * vec_ref[5:6, 0:256] + vec_ref[6:7, 0:256]
    return (yn + bonus) * g


def _dotc(a, b, dot=_dot):
    return dot(a.astype(BF16), b.astype(BF16))


def _tri_inv_all(lows):
    c = lows[0].shape[0]
    eye = (_iota((c, c), 0) == _iota((c, c), 1)).astype(F32)
    ts = [eye - low for low in lows]
    lps = [_dotc(low, low) for low in lows]
    n = 2
    while n < c:
        ts = [t + _dotc(t, lp) for t, lp in zip(ts, lps)]
        n *= 2
        if n < c:
            lps = [_dotc(lp, lp) for lp in lps]
    return ts


RWKV_GROUP = 8


def _rwkv_seq_kernel(z_ref, shift_ref, s0_ref, vec_ref, wa_ref, g2_ref, bd_ref, tri_ref,
                     o_ref, s_ref, last_ref, r_s, k_s, v_s, kk_s, b_s, ld_s, y_s, pe_s, *, tt):
    ti = pl.program_id(1)

    @pl.when(ti == 0)
    def _():
        last_ref[...] = shift_ref[0]
        s_ref[0] = s0_ref[0]

    z = z_ref[0]
    rolled = pltpu.roll(z, 1, 0)
    zprev = jnp.where(_iota(z.shape, 0) == 0, last_ref[...], rolled)
    last_ref[...] = z[tt - 1:tt, :]
    r, k2, v, kk, b, ld, g, bonus = _rwkv_prep(z, zprev, vec_ref, wa_ref, g2_ref, bd_ref)
    r_s[...] = r
    k_s[...] = k2
    v_s[...] = v
    kk_s[...] = kk
    b_s[...] = b
    ld_s[...] = ld

    c = RWKV_CHUNK
    ri = _iota((c, c), 0)
    ci = _iota((c, c), 1)
    strict = ri > ci
    incl = ri >= ci

    eye = (ri == ci).astype(F32)
    heads = [slice(h * HEAD_DIM, (h + 1) * HEAD_DIM) for h in range(N_HEADS)]
    n_chunks = tt // c
    group = RWKV_GROUP if n_chunks % RWKV_GROUP == 0 else 1

    def prepare(gi, carry):
        probs = []
        for j in range(group):
            ck = gi * group + j
            rows = pl.ds(pl.multiple_of(ck * c, c), c)
            ldc = ld_s[rows, :]
            cum = _dot3(tri_ref[...], ldc)
            p_in = jnp.exp(cum)
            inv_p = jnp.exp(-cum)
            kkt = kk_s[rows, :] * jnp.exp(cum - ldc)
            bt = b_s[rows, :] * inv_p
            kt = k_s[rows, :] * inv_p
            rt = r_s[rows, :] * p_in
            vv = v_s[rows, :]
            p_end = p_in[c - 1:c, :]
            pe_s[pl.ds(ck, 1), :] = p_end
            for hs in heads:
                probs.append(dict(kkt=kkt[:, hs], bt=bt[:, hs], kt=kt[:, hs], rt=rt[:, hs], v=vv[:, hs],
                                  p_end=p_end[:, hs]))
        blocks = [_dotc(jnp.concatenate([p['kkt'], p['rt']], axis=0),
                        jnp.concatenate([p['bt'], p['kt']], axis=0), _dot_nt) for p in probs]
        lows = [jnp.where(strict, bl[:c, :c], 0.0) for bl in blocks]
        gvs = [_dotc(jnp.where(strict, bl[:c, c:], 0.0), p['v']) for bl, p in zip(blocks, probs)]
        tinvs = _tri_inv_all(lows)
        ke_ue = [_dotc(t, jnp.concatenate([p['kkt'], gv], axis=1)) for t, p, gv in zip(tinvs, probs, gvs)]
        corr = [_dotc(jnp.where(incl, bl[c:, :c], 0.0), ku) for bl, ku in zip(blocks, ke_ue)]
        avs = [_dotc(jnp.where(incl, bl[c:, c:], 0.0), p['v']) for bl, p in zip(blocks, probs)]
        xps = [_dotc(ku[:, :HEAD_DIM], p['bt'], _dot_tn) * p['p_end'] for ku, p in zip(ke_ue, probs)]
        bcs = [_dotc(jnp.concatenate([p['v'], ku[:, HEAD_DIM:]], axis=0),
                     jnp.concatenate([p['kt'], -p['bt']], axis=0), _dot_tn) * p['p_end']
               for ku, p in zip(ke_ue, probs)]
        res = [p['rt'] - cr[:, :HEAD_DIM] for p, cr in zip(probs, corr)]
        y0s = [av - cr[:, HEAD_DIM:] for av, cr in zip(avs, corr)]
        for j in range(group):
            rows = pl.ds(pl.multiple_of((gi * group + j) * c, c), c)
            sl = slice(j * N_HEADS, (j + 1) * N_HEADS)
            r_s[rows, :] = jnp.concatenate(res[sl], axis=1)
            v_s[rows, :] = jnp.concatenate(y0s[sl], axis=1)
            kk_s[rows, :] = jnp.concatenate(xps[sl], axis=1)
            b_s[rows, :] = jnp.concatenate(bcs[sl], axis=1)
        return carry

    lax.fori_loop(0, n_chunks // group, prepare, 0)

    def advance(ck, carry):
        rows = pl.ds(pl.multiple_of(ck * c, c), c)
        re, y0, xp, bc = r_s[rows, :], v_s[rows, :], kk_s[rows, :], b_s[rows, :]
        p_end = pe_s[pl.ds(ck, 1), :]
        states = [s_ref[0, h] for h in range(N_HEADS)]
        ys = [_dotc(re[:, hs], s_h, _dot_nt) + y0[:, hs] for hs, s_h in zip(heads, states)]
        for h, (hs, s_h) in enumerate(zip(heads, states)):
            s_ref[0, h] = s_h * p_end[:, hs] - _dotc(s_h, xp[:, hs]) + bc[:, hs]
        y_s[rows, :] = jnp.concatenate(ys, axis=1)
        return carry

    lax.fori_loop(0, n_chunks, advance, 0, unroll=True)
    o_ref[0] = _rwkv_finish(y_s[...], bonus, g, vec_ref, bd_ref).astype(o_ref.dtype)


def rwkv_seq(z, shift, s0, vec, wa, g2, bd, tri):
    b, t, w = z.shape
    tt = _pick_tile(t, 512, RWKV_CHUNK)
    assert tt % RWKV_CHUNK == 0
    const = lambda i, j: (0, 0)
    scr = pltpu.VMEM((tt, BRANCH_W), F32)
    return pl.pallas_call(
        functools.partial(_rwkv_seq_kernel, tt=tt), grid=(b, t // tt),
        in_specs=[pl.BlockSpec((1, tt, w), lambda i, j: (i, j, 0)),
                  pl.BlockSpec((1, 1, w), lambda i, j: (i, 0, 0)),
                  pl.BlockSpec((1, N_HEADS, HEAD_DIM, HEAD_DIM), lambda i, j: (i, 0, 0, 0)),
                  pl.BlockSpec(vec.shape, const), pl.BlockSpec(wa.shape, const),
                  pl.BlockSpec(g2.shape, const), pl.BlockSpec(bd.shape, const),
                  pl.BlockSpec(tri.shape, const)],
        out_specs=[pl.BlockSpec((1, tt, BRANCH_W), lambda i, j: (i, j, 0)),
                   pl.BlockSpec((1, N_HEADS, HEAD_DIM, HEAD_DIM), lambda i, j: (i, 0, 0, 0))],
        out_shape=[jax.ShapeDtypeStruct((b, t, BRANCH_W), BF16),
                   jax.ShapeDtypeStruct((b, N_HEADS, HEAD_DIM, HEAD_DIM), F32)],
        scratch_shapes=[pltpu.VMEM((1, w), F32)] + [scr] * 7 + [pltpu.VMEM((tt // RWKV_CHUNK, BRANCH_W), F32)],
        compiler_params=_params("parallel", "arbitrary"), name="rwkv_seq",
    )(z, shift, s0, vec, wa, g2, bd, tri)


def _rwkv_step_kernel(z_ref, shift_ref, s0_ref, vec_ref, wa_ref, g2_ref, bd_ref, o_ref, s_ref):
    rows = 8
    z = jnp.broadcast_to(z_ref[0], (rows, z_ref.shape[2]))
    zprev = jnp.broadcast_to(shift_ref[0], z.shape)
    r, k2, v, kk, b, ld, g, bonus = _rwkv_prep(z, zprev, vec_ref, wa_ref, g2_ref, bd_ref)
    decay = jnp.exp(ld)
    eye = _iota((HEAD_DIM, HEAD_DIM), 0) == _iota((HEAD_DIM, HEAD_DIM), 1)
    ys = []
    for h in range(N_HEADS):
        hs = slice(h * HEAD_DIM, (h + 1) * HEAD_DIM)
        s_h = s0_ref[0, h]
        sk = jnp.sum(s_h * kk[0:1, hs], axis=1, keepdims=True)
        v_col = jnp.sum(jnp.where(eye, v[0:1, hs], 0.0), axis=1, keepdims=True)
        s_new = s_h * decay[0:1, hs] - sk * b[0:1, hs] + v_col * k2[0:1, hs]
        s_ref[0, h] = s_new
        y_col = jnp.sum(s_new * r[0:1, hs], axis=1, keepdims=True)
        ys.append(jnp.sum(jnp.where(eye, y_col, 0.0), axis=0, keepdims=True))
    y = jnp.broadcast_to(jnp.concatenate(ys, axis=1), (rows, BRANCH_W))
    o_ref[0] = _rwkv_finish(y, bonus, g, vec_ref, bd_ref)[0:1].astype(o_ref.dtype)


def rwkv_step(z, shift, s0, vec, wa, g2, bd):
    b, _, w = z.shape
    const = lambda i: (0, 0)
    return pl.pallas_call(
        _rwkv_step_kernel, grid=(b,),
        in_specs=[pl.BlockSpec((1, 1, w), lambda i: (i, 0, 0)),
                  pl.BlockSpec((1, 1, w), lambda i: (i, 0, 0)),
                  pl.BlockSpec((1, N_HEADS, HEAD_DIM, HEAD_DIM), lambda i: (i, 0, 0, 0)),
                  pl.BlockSpec(vec.shape, const), pl.BlockSpec(wa.shape, const),
                  pl.BlockSpec(g2.shape, const), pl.BlockSpec(bd.shape, const)],
        out_specs=[pl.BlockSpec((1, 1, BRANCH_W), lambda i: (i, 0, 0)),
                   pl.BlockSpec((1, N_HEADS, HEAD_DIM, HEAD_DIM), lambda i: (i, 0, 0, 0))],
        out_shape=[jax.ShapeDtypeStruct((b, 1, BRANCH_W), BF16),
                   jax.ShapeDtypeStruct((b, N_HEADS, HEAD_DIM, HEAD_DIM), F32)],
        compiler_params=_params("parallel"), name="rwkv_step",
    )(z, shift, s0, vec, wa, g2, bd)


def _stack_heads(x):
    rows = [x[:, h * HEAD_DIM:(h + 1) * HEAD_DIM] for h in range(N_HEADS)]
    return jnp.concatenate(rows + [jnp.zeros((8 - N_HEADS, HEAD_DIM), x.dtype)], axis=0)


def _unstack_heads(x):
    return jnp.concatenate([x[h:h + 1, :] for h in range(N_HEADS)], axis=1)


def _column(row):
    n = row.shape[1]
    eye = _iota((n, n), 0) == _iota((n, n), 1)
    return jnp.sum(jnp.where(eye, row, 0.0), axis=1, keepdims=True)


def _to_row(col):
    n = col.shape[0]
    eye = _iota((n, n), 0) == _iota((n, n), 1)
    return jnp.sum(jnp.where(eye, col, 0.0), axis=0, keepdims=True)


def _page_scores(k_page, q_cols, s_ref, n_pages):
    def body(p, c):
        for h, qc in enumerate(q_cols):
            s_ref[h, pl.ds(p, 1), :] = jnp.sum(k_page(p, h) * qc, axis=0, keepdims=True)
        return c

    lax.fori_loop(0, n_pages, body, 0, unroll=4)


def _page_values(v_page, p_ref, n_pages):
    def body(p, accs):
        return tuple(acc + v_page(p, h) * p_ref[h, pl.ds(p, 1), :] for h, acc in enumerate(accs))

    init = tuple(jnp.zeros((HEAD_DIM, p_ref.shape[2]), F32) for _ in range(N_HEADS))
    return [jnp.sum(a, axis=1, keepdims=True) for a in lax.fori_loop(0, n_pages, body, init, unroll=4)]


def _softmax_with_new(s, mask, s_new):
    s = jnp.where(mask, s, NEG)
    m = jnp.maximum(jnp.max(jnp.max(s, axis=1, keepdims=True), axis=0, keepdims=True), s_new)
    p = jnp.where(mask, jnp.exp(s - m), 0.0)
    p_new = jnp.exp(s_new - m)
    den = jnp.maximum(jnp.sum(jnp.sum(p, axis=1, keepdims=True), axis=0, keepdims=True) + p_new, 1e-30)
    return p / den, p_new / den


def _nsa_sample_kernel(pt_ref, z_ref, tab_ref, win_ref, w1_ref, pe_ref, w2_ref, cover_ref, pj_ref, hr_ref,
                       cmp_hbm, slc_hbm, o_ref, slc_row_ref, win_row_ref,
                       cbuf, sbuf, rows_s, lines_s, s_s, p_s, sem,
                       *, base, n_pages, page, past, n_cmp, n_slc, k_sel):
    bi = pl.program_id(0)
    n_lines = past // CMP_STRIDE
    slot = lax.rem(bi, 2)

    def copies(seq, sl, p):
        pg = pt_ref[seq * n_pages + p] + base
        return (pltpu.make_async_copy(cmp_hbm.at[pg], cbuf.at[sl, p], sem.at[sl, 0]),
                pltpu.make_async_copy(slc_hbm.at[pg], sbuf.at[sl, p], sem.at[sl, 1]))

    def start_fetch(seq, sl):
        for p in range(n_pages):
            for cp in copies(seq, sl, p):
                cp.start()

    @pl.when(bi == 0)
    def _():
        start_fetch(0, 0)

    @pl.when(bi + 1 < pl.num_programs(0))
    def _():
        start_fetch(bi + 1, 1 - slot)

    z = z_ref[0]
    tab = tab_ref[...]
    q = z[:, :BRANCH_W] * SCALE
    q4 = _stack_heads(q)
    q4r_f32 = _stack_heads(_rope_heads(q, tab))
    q4r = q4r_f32.astype(BF16)
    slc_row = _rope(z[:, 384:512], tab, half_only=True)
    win_row = _rope(z[:, 512:640], tab, half_only=True)
    slc_row_ref[0] = slc_row
    win_row_ref[0] = win_row

    for p in range(n_pages):
        for cp in copies(bi, slot, p):
            cp.wait()

    def xpose(p, c):
        rows_s[pl.ds(pl.multiple_of(p * page, page), page), :] = cbuf[slot, p].T
        return c

    lax.fori_loop(0, n_pages, xpose, 0, unroll=8)
    for l in range(CMP_STRIDE):
        lines_s[:, l * LANES:(l + 1) * LANES] = rows_s[pl.ds(l, n_lines, stride=CMP_STRIDE), :]

    cmpkv = _compress(lines_s[...], w1_ref, pe_ref, w2_ref)
    ck = cmpkv[:, :HEAD_DIM].astype(BF16)
    cv = cmpkv[:, HEAD_DIM:].astype(BF16)
    n = _iota((1, cmpkv.shape[0]), 1)
    cmask = (n * CMP_STRIDE + (CMP_BLOCK - 1) <= past) & (n < n_cmp)
    p_cmp = _msoftmax(_dot_nt(q4.astype(BF16), ck), cmask)
    o_cmp = _dot(p_cmp.astype(BF16), cv)
    psum = jnp.sum(jnp.where(_iota(p_cmp.shape, 0) < N_HEADS, p_cmp, 0.0), axis=0, keepdims=True)
    imp = _dot2_exact_rhs(jnp.broadcast_to(psum, p_cmp.shape), cover_ref[...])[0:1]
    t_pos = jnp.full((1, 1), past, jnp.int32)
    sel = _select_blocks(imp, t_pos, n_slc, k_sel)

    def with_new_key(s_past, mask, new_row, v_past):
        s_new = jnp.sum(q4r_f32 * new_row[:, :HEAD_DIM], axis=1, keepdims=True)
        s_past = jnp.where(mask, s_past, NEG)
        m = jnp.maximum(jnp.max(s_past, axis=1, keepdims=True), s_new)
        p_past = jnp.where(mask, jnp.exp(s_past - m), 0.0)
        p_new = jnp.exp(s_new - m)
        den = jnp.maximum(jnp.sum(p_past, axis=1, keepdims=True) + p_new, 1e-30)
        return _dot((p_past / den).astype(BF16), v_past) + (p_new / den) * new_row[:, HEAD_DIM:]

    sel_pages = _dot((pj_ref[...] * sel.astype(F32)).astype(BF16), hr_ref[...]) > 0.5
    _page_scores(lambda p, h: sbuf[slot, p, 0], [_column(q4r_f32[h:h + 1, :]) for h in range(N_HEADS)], s_s,
                 n_pages)
    p_new = []
    for h in range(N_HEADS):
        s_new = jnp.sum(q4r_f32[h:h + 1, :] * slc_row[:, :HEAD_DIM], axis=1, keepdims=True)
        p_s[h], pn = _softmax_with_new(s_s[h], sel_pages, s_new)
        p_new.append(pn)
    o_slc = [_to_row(col) + p_new[h] * slc_row[:, HEAD_DIM:]
             for h, col in enumerate(_page_values(lambda p, h: sbuf[slot, p, 1], p_s, n_pages))]
    wrows = win_ref[0]
    wk = wrows[:, :HEAD_DIM].astype(BF16)
    wv = wrows[:, HEAD_DIM:].astype(BF16)
    o_win = with_new_key(_dot_nt(q4r, wk), jnp.full((1, wrows.shape[0]), True), win_row, wv)

    g = _sigmoid(z[:, 640:640 + 3 * N_HEADS])
    outs = []
    for h in range(N_HEADS):
        outs.append(g[:, 3 * h:3 * h + 1] * o_cmp[h:h + 1] + g[:, 3 * h + 1:3 * h + 2] * o_slc[h]
                    + g[:, 3 * h + 2:3 * h + 3] * o_win[h:h + 1])
    o_ref[0] = jnp.concatenate(outs, axis=1).astype(o_ref.dtype)


def nsa_sample(page_table, z_nsa, tab, win, w1c, pe2, w2c, cover, pj, hr, cache_cmp, cache_slc,
               layer, n_pool, page, n_cmp, n_slc):
    b = z_nsa.shape[0]
    n_pages = page_table.shape[1]
    past = n_pages * page
    kern = functools.partial(_nsa_sample_kernel, base=layer * n_pool, n_pages=n_pages, page=page, past=past,
                             n_cmp=n_cmp, n_slc=n_slc, k_sel=min(N_SELECT, n_slc))
    const = lambda i, pt: (0, 0)
    row = lambda i, pt: (i, 0, 0)
    grid_spec = pltpu.PrefetchScalarGridSpec(
        num_scalar_prefetch=1, grid=(b,),
        in_specs=[pl.BlockSpec((1, 1, z_nsa.shape[2]), row),
                  pl.BlockSpec(tab.shape, lambda i, pt: (0, 0, 0)),
                  pl.BlockSpec((1,) + win.shape[1:], row),
                  pl.BlockSpec(w1c.shape, const), pl.BlockSpec(pe2.shape, const),
                  pl.BlockSpec(w2c.shape, const), pl.BlockSpec(cover.shape, const),
                  pl.BlockSpec(pj.shape, const), pl.BlockSpec(hr.shape, const),
                  pl.BlockSpec(memory_space=pl.ANY), pl.BlockSpec(memory_space=pl.ANY)],
        out_specs=[pl.BlockSpec((1, 1, BRANCH_W), row), pl.BlockSpec((1, 1, LANES), row),
                   pl.BlockSpec((1, 1, LANES), row)],
        scratch_shapes=[pltpu.VMEM((2, n_pages, 2 * HEAD_DIM, page), F32),
                        pltpu.VMEM((2, n_pages, 2, HEAD_DIM, page), F32),
                        pltpu.VMEM((past, 2 * HEAD_DIM), F32),
                        pltpu.VMEM((past // CMP_STRIDE, CMP_STRIDE * LANES), F32),
                        pltpu.VMEM((N_HEADS, n_pages, page), F32),
                        pltpu.VMEM((N_HEADS, n_pages, page), F32),
                        pltpu.SemaphoreType.DMA((2, 2))])
    return pl.pallas_call(
        kern, grid_spec=grid_spec,
        out_shape=[jax.ShapeDtypeStruct((b, 1, BRANCH_W), BF16),
                   jax.ShapeDtypeStruct((b, 1, LANES), F32), jax.ShapeDtypeStruct((b, 1, LANES), F32)],
        compiler_params=_params("arbitrary"), name="nsa_sample",
    )(page_table.reshape(-1), z_nsa, tab, win, w1c, pe2, w2c, cover, pj, hr, cache_cmp, cache_slc)


def _moba_sample_kernel(pt_ref, z_ref, tab_ref, kv_hbm, o_ref, row_ref, buf, s_s, p_s, sem,
                        *, base, n_pages, pages_per_block, topk):
    bi = pl.program_id(0)
    slot = lax.rem(bi, 2)

    def copy(seq, sl, p):
        pg = pt_ref[seq * n_pages + p] + base
        return pltpu.make_async_copy(kv_hbm.at[pg], buf.at[sl, p], sem.at[sl])

    @pl.when(bi == 0)
    def _():
        for p in range(n_pages):
            copy(0, 0, p).start()

    @pl.when(bi + 1 < pl.num_programs(0))
    def _():
        for p in range(n_pages):
            copy(bi + 1, 1 - slot, p).start()

    z = z_ref[0]
    tab = tab_ref[...]
    q = z[:, :BRANCH_W] * SCALE
    qr = _rope_heads(q, tab)
    k_new = _rope_heads(z[:, BRANCH_W:2 * BRANCH_W], tab)
    v_new = z[:, 2 * BRANCH_W:3 * BRANCH_W]
    row_ref[0] = jnp.concatenate([k_new, v_new], axis=1)

    for p in range(n_pages):
        copy(bi, slot, p).wait()

    heads = [slice(h * HEAD_DIM, (h + 1) * HEAD_DIM) for h in range(N_HEADS)]
    _page_scores(lambda p, h: buf[slot, p, h], [_column(qr[:, hs]) for hs in heads], s_s, n_pages)
    pi = _iota((n_pages, n_pages), 0)
    pj = _iota((n_pages, n_pages), 1)
    shift = pages_per_block.bit_length() - 1
    same_block = lax.shift_right_logical(pi, shift) == lax.shift_right_logical(pj, shift)
    p_new = []
    for h, hs in enumerate(heads):
        s = s_s[h]
        page_sum = jnp.sum(s, axis=1, keepdims=True)
        page_sum_row = jnp.sum(jnp.where(pi == pj, page_sum, 0.0), axis=0, keepdims=True)
        gate_col = jnp.sum(jnp.where(same_block, page_sum_row, 0.0), axis=1, keepdims=True)
        gate_row = jnp.sum(jnp.where(same_block, page_sum, 0.0), axis=0, keepdims=True)
        beats = ((gate_row > gate_col) | ((gate_row == gate_col) & (pj < pi))) & jnp.logical_not(same_block)
        picked = jnp.sum(beats.astype(F32), axis=1, keepdims=True) < pages_per_block * topk - 0.5
        s_new = jnp.sum(qr[:, hs] * k_new[:, hs], axis=1, keepdims=True)
        p_s[h], pn = _softmax_with_new(s, picked, s_new)
        p_new.append(pn)
    cols = _page_values(lambda p, h: buf[slot, p, N_HEADS + h], p_s, n_pages)
    outs = [_to_row(cols[h]) + p_new[h] * v_new[:, hs] for h, hs in enumerate(heads)]
    o_ref[0] = jnp.concatenate(outs, axis=1).astype(o_ref.dtype)


def moba_sample(page_table, z_moba, tab, cache, layer, n_pool, page):
    b = z_moba.shape[0]
    n_pages = page_table.shape[1]
    past = n_pages * page
    assert past % MOBA_BLOCK == 0 and MOBA_BLOCK % page == 0
    ppb = MOBA_BLOCK // page
    assert ppb & (ppb - 1) == 0
    kern = functools.partial(_moba_sample_kernel, base=layer * n_pool, n_pages=n_pages, pages_per_block=ppb,
                             topk=min(MOBA_TOPK, past // MOBA_BLOCK + 1))
    row = lambda i, pt: (i, 0, 0)
    grid_spec = pltpu.PrefetchScalarGridSpec(
        num_scalar_prefetch=1, grid=(b,),
        in_specs=[pl.BlockSpec((1, 1, z_moba.shape[2]), row),
                  pl.BlockSpec(tab.shape, lambda i, pt: (0, 0, 0)),
                  pl.BlockSpec(memory_space=pl.ANY)],
        out_specs=[pl.BlockSpec((1, 1, BRANCH_W), row), pl.BlockSpec((1, 1, 2 * BRANCH_W), row)],
        scratch_shapes=[pltpu.VMEM((2, n_pages, 2 * N_HEADS, HEAD_DIM, page), F32),
                        pltpu.VMEM((N_HEADS, n_pages, page), F32),
                        pltpu.VMEM((N_HEADS, n_pages, page), F32),
                        pltpu.SemaphoreType.DMA((2,))])
    return pl.pallas_call(
        kern, grid_spec=grid_spec,
        out_shape=[jax.ShapeDtypeStruct((b, 1, BRANCH_W), BF16),
                   jax.ShapeDtypeStruct((b, 1, 2 * BRANCH_W), F32)],
        compiler_params=_params("arbitrary"), name="moba_sample",
    )(page_table.reshape(-1), z_moba, tab, cache)


def _rope_table(pos):
    inv = ROPE_THETA ** (-jnp.arange(ROT_HALF, dtype=F32) / ROT_HALF)
    ang = pos.astype(F32)[:, None] * inv[None, :]
    cos, sin = jnp.cos(ang), jnp.sin(ang)
    t = pos.shape[0]
    ones = jnp.ones((t, HEAD_DIM - 2 * ROT_HALF), F32)
    zeros = jnp.zeros((t, HEAD_DIM - 2 * ROT_HALF), F32)
    z8 = jnp.zeros((t, ROT_HALF), F32)
    c = jnp.concatenate([cos, cos, ones], axis=1)
    sa = jnp.concatenate([-sin, z8, zeros], axis=1)
    sb = jnp.concatenate([z8, sin, zeros], axis=1)
    return jnp.stack([jnp.tile(c, (1, 2)), jnp.tile(sa, (1, 2)), jnp.tile(sb, (1, 2))], axis=0)


def _cover_matrix(n_rows, n_cmp, n_slc, width):
    starts = np.arange(n_rows) * CMP_STRIDE
    blk = np.arange(width) * SLC_BLOCK
    cover = (starts[:, None] < blk[None, :] + SLC_BLOCK) & (starts[:, None] + CMP_BLOCK > blk[None, :])
    cover &= (np.arange(n_rows)[:, None] < n_cmp) & (np.arange(width)[None, :] < n_slc)
    return jnp.asarray(cover, BF16)


def _expand_matrix(n_blocks, block, n_keys):
    e = np.arange(n_blocks)[:, None] == (np.arange(n_keys)[None, :] // block)
    return jnp.asarray(e, BF16)


def _round_up(n, m):
    return -(-n // m) * m


COL_NSA, COL_MOBA, COL_CROSS, COL_RWKV, COL_MERGE = 0, 768, 1536, 2304, 4096


def _pack_w_in(w_in):
    def zeros(n):
        return jnp.zeros(w_in.shape[:2] + (n,), w_in.dtype)

    parts = [w_in[..., 0:652], zeros(NSA_PAD - 652), w_in[..., 1708:2476], w_in[..., 2476:2732],
             zeros(COL_RWKV - COL_CROSS - BRANCH_W), w_in[..., 652:1708], zeros(RWKV_PAD - RWKV_IN),
             zeros(COL_MERGE - COL_RWKV - RWKV_PAD), w_in[..., 2732:]]
    return jnp.concatenate(parts, axis=-1).astype(BF16)


def _layer_weights(l, nsa_pe, nsa_phi_w1, nsa_phi_w2, rwkv_mu, rwkv_w0, rwkv_w2, rwkv_a0, rwkv_a2,
                   rwkv_g2, rwkv_k_k, rwkv_k_a, rwkv_r_k, rwkv_ln_w, rwkv_ln_b):
    w = {}
    w1 = nsa_phi_w1[l]
    hidden = w1.shape[2]
    w1r = w1.reshape(2, 2, CMP_STRIDE, HEAD_DIM, hidden)
    w1c = jnp.zeros((2, CMP_STRIDE, 2, HEAD_DIM, 2, hidden), F32)
    w1c = w1c.at[:, :, 0, :, 0, :].set(w1r[0]).at[:, :, 1, :, 1, :].set(w1r[1])
    w1c = w1c.reshape(2, CMP_STRIDE * 2 * HEAD_DIM, 2 * hidden)
    w['w1c'] = jnp.concatenate([w1c[0], w1c[1]], axis=1).astype(BF16)
    pe = nsa_pe[l].reshape(2, CMP_STRIDE * 2 * HEAD_DIM)
    w['pe2'] = jnp.pad(pe, ((0, 6), (0, 0)))
    w2 = nsa_phi_w2[l]
    w2c = jnp.zeros((2, hidden, 2, HEAD_DIM), F32).at[0, :, 0, :].set(w2[0]).at[1, :, 1, :].set(w2[1])
    w['w2c'] = w2c.reshape(2 * hidden, 2 * HEAD_DIM).astype(BF16)
    vec = jnp.zeros((8, RWKV_PAD), F32)
    vec = vec.at[0, :RWKV_IN].set(rwkv_mu[l])
    vec = vec.at[1, 0:256].set(rwkv_w0[l]).at[1, 256:512].set(rwkv_a0[l])
    for i, p in enumerate((rwkv_k_k, rwkv_k_a, rwkv_r_k, rwkv_ln_w, rwkv_ln_b)):
        vec = vec.at[2 + i, 0:256].set(p[l])
    w['rwkv_vec'] = vec
    wa = jnp.zeros((128, 512), F32).at[0:64, 0:256].set(rwkv_w2[l]).at[64:128, 256:512].set(rwkv_a2[l])
    w['rwkv_wa'] = wa.astype(BF16)
    w['rwkv_g2'] = jnp.pad(rwkv_g2[l], ((0, 256 - rwkv_g2.shape[1]), (0, 0))).astype(BF16)
    return w


def _project(h, w_all, l, d):
    z_nsa = mm(h, w_all, l, COL_NSA, NSA_PAD)
    z_rwkv = mm(h, w_all, l, COL_RWKV, RWKV_PAD)
    z_moba = mm(h, w_all, l, COL_MOBA, 3 * BRANCH_W)
    z_cross = mm(h, w_all, l, COL_CROSS, BRANCH_W)
    gates = mm(h, w_all, l, COL_MERGE, 4 * d, out_dtype=BF16, gate=True)
    return z_nsa, z_rwkv, z_moba, z_cross, gates


def _mixers_prompt(h, w_all, l, w, tab, mem_kv, consts, b, t):
    z_nsa, z_rwkv, z_moba, z_cross, gates = _project(h, w_all, l, h.shape[1])
    z_nsa = z_nsa.reshape(b, t, NSA_PAD)
    z_rwkv = z_rwkv.reshape(b, t, RWKV_PAD)
    z_moba = z_moba.reshape(b, t, 3 * BRANCH_W)
    z_cross = z_cross.reshape(b, t, BRANCH_W)

    new_cmp = z_nsa[:, :, 256:384]
    new_slc, new_win = nsa_rope(z_nsa, tab)
    cmp_kv = compress(new_cmp.reshape(b, t // CMP_STRIDE, CMP_STRIDE * LANES), w['w1c'], w['pe2'], w['w2c'])
    o_nsa = nsa_attn(z_nsa, tab, cmp_kv, new_slc, new_win, consts['cover_p'], consts['n_cmp_p'],
                     consts['n_slc_p'])

    zero_shift = jnp.zeros((b, 1, RWKV_PAD), F32)
    zero_state = jnp.zeros((b, N_HEADS, HEAD_DIM, HEAD_DIM), F32)
    o_rwkv, new_state = rwkv_seq(z_rwkv, zero_shift, zero_state, w['rwkv_vec'], w['rwkv_wa'], w['rwkv_g2'],
                                 consts['bd'], consts['tri'])
    new_shift = z_rwkv[:, t - 1, :RWKV_IN]

    new_moba, means = moba_rope(z_moba, tab)
    o_moba = moba_attn(z_moba, tab, new_moba, means.reshape(b, -1, BRANCH_W))
    o_cross = cross_attn(z_cross, mem_kv)

    outs = [o.reshape(b * t, BRANCH_W) for o in (o_nsa, o_rwkv, o_moba, o_cross)]
    win_keep = min(WINDOW, t)
    caches = (new_cmp.reshape(b, t, 2, HEAD_DIM), new_slc.reshape(b, t, 2, HEAD_DIM),
              new_win[:, t - win_keep:].reshape(b, win_keep, 2, HEAD_DIM),
              new_moba.reshape(b, t, 2, N_HEADS, HEAD_DIM), new_state, new_shift)
    return outs, gates, caches


def _mixers_sample(h, w_all, l, w, tab, page_table, caches_in, consts, b):
    cmp_pages, slc_pages, cache_win, moba_pages, cache_mem, state_rwkv, state_shift, n_pool, page = caches_in
    z_nsa, z_rwkv, z_moba, z_cross, gates = _project(h, w_all, l, h.shape[1])
    z_nsa = z_nsa.reshape(b, 1, NSA_PAD)
    z_rwkv = z_rwkv.reshape(b, 1, RWKV_PAD)
    z_moba = z_moba.reshape(b, 1, 3 * BRANCH_W)
    z_cross = z_cross.reshape(b, 1, BRANCH_W)

    win = cache_win[l].reshape(b, -1, LANES)
    o_nsa, slc_row, win_row = nsa_sample(
        page_table, z_nsa, tab, win, w['w1c'], w['pe2'], w['w2c'], consts['cover_s'], consts['pj_s'],
        consts['hr_s'], cmp_pages, slc_pages, l, n_pool, page, consts['n_cmp_s'], consts['n_slc_s'])
    new_cmp = z_nsa[:, :, 256:384]
    full_win = jnp.concatenate([win, win_row], axis=1)
    keep = min(WINDOW, full_win.shape[1])
    new_buf = full_win[:, full_win.shape[1] - keep:]

    shift = jnp.pad(state_shift[l], ((0, 0), (0, RWKV_PAD - RWKV_IN))).reshape(b, 1, RWKV_PAD)
    o_rwkv, new_state = rwkv_step(z_rwkv, shift, state_rwkv[l], w['rwkv_vec'], w['rwkv_wa'], w['rwkv_g2'],
                                  consts['bd'])
    new_shift = z_rwkv[:, 0, :RWKV_IN]

    o_moba, moba_row = moba_sample(page_table, z_moba, tab, moba_pages, l, n_pool, page)
    o_cross = cross_attn(z_cross, cache_mem[l].reshape(b, -1, 2 * BRANCH_W))

    outs = [o.reshape(b, BRANCH_W) for o in (o_nsa, o_rwkv, o_moba, o_cross)]
    caches = (new_cmp.reshape(b, 1, 2, HEAD_DIM), slc_row.reshape(b, 1, 2, HEAD_DIM),
              new_buf.reshape(b, keep, 2, HEAD_DIM), moba_row.reshape(b, 1, 2, N_HEADS, HEAD_DIM),
              new_state, new_shift)
    return outs, gates, caches


def kernel(x_prompt, x_sample, cache_nsa_cmp, cache_nsa_slc, cache_nsa_win, cache_moba, cache_mem, state_rwkv, state_shift, page_table, mem_prompt, g_pre_mix, g_post_mix, g_pre_ffn, g_post_ffn, g_mem, w_in, nsa_pe, nsa_phi_w1, nsa_phi_w2, rwkv_mu, rwkv_w0, rwkv_w2, rwkv_a0, rwkv_a2, rwkv_g2, rwkv_k_k, rwkv_k_a, rwkv_r_k, rwkv_ln_w, rwkv_ln_b, w_mem_kv, w_branch, w_out, w_ffn_gate, w_ffn_up, w_ffn_down):
    bp, t, d = x_prompt.shape
    bs = x_sample.shape[0]
    assert x_sample.shape[1] == 1
    depth = w_in.shape[0]
    page = cache_nsa_cmp.shape[2]
    past = page_table.shape[1] * page
    n_mem = mem_prompt.shape[1]
    assert t % MOBA_BLOCK == 0 and past % MOBA_BLOCK == 0 and page % SLC_BLOCK == 0

    n_cmp_p = (t - CMP_BLOCK) // CMP_STRIDE + 1
    n_slc_p = -(-t // SLC_BLOCK)
    n_cmp_s = (past + 1 - CMP_BLOCK) // CMP_STRIDE + 1
    n_slc_s = -(-(past + 1) // SLC_BLOCK)
    consts = {
        'n_cmp_p': n_cmp_p, 'n_slc_p': n_slc_p, 'n_cmp_s': n_cmp_s, 'n_slc_s': n_slc_s,
        'cover_p': _cover_matrix(t // CMP_STRIDE, n_cmp_p, n_slc_p, _round_up(n_slc_p, 8)).T,
        'cover_s': _cover_matrix(past // CMP_STRIDE, n_cmp_s, n_slc_s, _round_up(n_slc_s, LANES)),
        'pj_s': jnp.asarray(np.arange(_round_up(n_slc_s, LANES))[None, :] // (page // SLC_BLOCK)
                            == np.arange(past // page)[:, None], F32),
        'hr_s': jnp.asarray(np.arange(_round_up(n_slc_s, LANES))[:, None] % (page // SLC_BLOCK)
                            == np.arange(page)[None, :] // SLC_BLOCK, BF16),
        'bd': jnp.asarray(np.arange(BRANCH_W)[:, None] // HEAD_DIM == np.arange(BRANCH_W)[None, :] // HEAD_DIM, BF16),
        'tri': jnp.asarray(np.tril(np.ones((RWKV_CHUNK, RWKV_CHUNK), np.float32))),
    }
    tab_p = _rope_table(jnp.arange(t, dtype=jnp.int32))
    tab_s = _rope_table(jnp.full((1,), past, jnp.int32))

    xp = x_prompt.reshape(bp * t, d)
    xs = x_sample.reshape(bs, d)
    mem_flat = mem_prompt.reshape(bp * n_mem, d)
    n_pool = cache_nsa_cmp.shape[1]
    cmp_pages = jnp.transpose(cache_nsa_cmp, (0, 1, 3, 4, 2)).reshape(depth * n_pool, 2 * HEAD_DIM, page)
    slc_pages = jnp.transpose(cache_nsa_slc, (0, 1, 3, 4, 2)).reshape(depth * n_pool, 2, HEAD_DIM, page)
    moba_pages = jnp.transpose(cache_moba, (0, 1, 3, 4, 5, 2)).reshape(depth * n_pool, 2 * N_HEADS, HEAD_DIM, page)
    sample_caches = (cmp_pages, slc_pages, cache_nsa_win, moba_pages, cache_mem, state_rwkv, state_shift,
                     n_pool, page)
    w_all = _pack_w_in(w_in)
    w_mem, w_br, w_o = w_mem_kv.astype(BF16), w_branch.astype(BF16), w_out.astype(BF16)
    w_fg, w_fu, w_fd = w_ffn_gate.astype(BF16), w_ffn_up.astype(BF16), w_ffn_down.astype(BF16)
    outs_p, outs_s, mem_out = [], [], []
    for l in range(depth):
        w = _layer_weights(l, nsa_pe, nsa_phi_w1, nsa_phi_w2, rwkv_mu, rwkv_w0, rwkv_w2, rwkv_a0,
                           rwkv_a2, rwkv_g2, rwkv_k_k, rwkv_k_a, rwkv_r_k, rwkv_ln_w, rwkv_ln_b)
        mem_kv = mm(rms_cast(mem_flat, g_mem[l]), w_mem, l, 0, 2 * BRANCH_W).reshape(bp, n_mem, 2 * BRANCH_W)
        mem_out.append(mem_kv.reshape(bp, n_mem, 2, N_HEADS, HEAD_DIM))

        o_p, gates_p, new_p = _mixers_prompt(rms_cast(xp, g_pre_mix[l]), w_all, l, w, tab_p, mem_kv, consts, bp, t)
        xp = merge(xp, o_p, gates_p, w_br, w_o, g_post_mix[l], l)
        xp = ffn(xp, g_pre_ffn[l], w_fg, w_fu, w_fd, g_post_ffn[l], l)

        o_s, gates_s, new_s = _mixers_sample(rms_cast(xs, g_pre_mix[l]), w_all, l, w, tab_s, page_table,
                                             sample_caches, consts, bs)
        xs = merge(xs, o_s, gates_s, w_br, w_o, g_post_mix[l], l)
        xs = ffn(xs, g_pre_ffn[l], w_fg, w_fu, w_fd, g_post_ffn[l], l)
        outs_p.append(new_p)
        outs_s.append(new_s)

    def stack(outs, i):
        return jnp.stack([o[i] for o in outs], axis=0)

    return (xp.reshape(bp, t, d), xs.reshape(bs, 1, d),
            stack(outs_p, 0), stack(outs_s, 0), stack(outs_p, 1), stack(outs_s, 1),
            stack(outs_p, 2), stack(outs_s, 2), stack(outs_p, 3), stack(outs_s, 3),
            jnp.stack(mem_out, axis=0),
            stack(outs_p, 4), stack(outs_s, 4), stack(outs_p, 5), stack(outs_s, 5))
```

```python
import functools

import jax
import jax.numpy as jnp
import numpy as np
from jax import lax
from jax.experimental import pallas as pl
from jax.experimental.pallas import tpu as pltpu

F32 = jnp.float32
BF16 = jnp.bfloat16

HEAD_DIM = 64
N_HEADS = 4
BRANCH_W = N_HEADS * HEAD_DIM
ROT_HALF = 8
ROPE_THETA = 500000.0
CMP_BLOCK = 32
CMP_STRIDE = 16
SLC_BLOCK = 64
N_SELECT = 16
WINDOW = 512
MOBA_BLOCK = 256
MOBA_TOPK = 3
RWKV_IN = 1056
RWKV_PAD = 1152
NSA_PAD = 768
RWKV_GN_EPS = 64e-5
RWKV_CHUNK = 64
NEG = -1e30
SCALE = HEAD_DIM ** -0.5
LANES = 128
ATTN_KEY_TILE = 256
FFN_TILE = 1408
VMEM_LIMIT = 48 * 1024 * 1024


def _iota(shape, dim):
    return lax.broadcasted_iota(jnp.int32, shape, dim)


def _dot(a, b):
    return lax.dot_general(a, b, (((1,), (0,)), ((), ())), preferred_element_type=F32)


def _dot_nt(a, b):
    return lax.dot_general(a, b, (((1,), (1,)), ((), ())), preferred_element_type=F32)


def _dot_tn(a, b):
    return lax.dot_general(a, b, (((0,), (0,)), ((), ())), preferred_element_type=F32)


def _split(x):
    hi = x.astype(BF16)
    lo = (x - hi.astype(F32)).astype(BF16)
    return hi, lo


def _dot3(a, b, dot=_dot):
    ah, al = _split(a)
    bh, bl = _split(b)
    return dot(ah, bh) + dot(al, bh) + dot(ah, bl)


def _dot2_exact_rhs(a, b_bf16):
    ah, al = _split(a)
    return _dot(ah, b_bf16) + _dot(al, b_bf16)


def _sigmoid(x):
    return 1.0 / (1.0 + jnp.exp(-x))


def _rms(x, g, eps=1e-6):
    return x * lax.rsqrt(jnp.mean(x * x, axis=-1, keepdims=True) + eps) * g


def _msoftmax(s, mask):
    s = jnp.where(mask, s, NEG)
    p = jnp.where(mask, jnp.exp(s - jnp.max(s, axis=-1, keepdims=True)), 0.0)
    return p / jnp.maximum(jnp.sum(p, axis=-1, keepdims=True), 1e-30)


def _flash_step(q, k, v, mask, m, l, acc):
    s = jnp.where(mask, _dot_nt(q, k), NEG)
    m_new = jnp.maximum(m, jnp.max(s, axis=-1, keepdims=True))
    p = jnp.where(mask, jnp.exp(s - m_new), 0.0)
    alpha = jnp.exp(m - m_new)
    l = alpha * l + jnp.sum(p, axis=-1, keepdims=True)
    acc = alpha * acc + _dot(p.astype(BF16), v)
    return m_new, l, acc


def _with_ones(v_t):
    return jnp.concatenate([v_t, jnp.ones((8, v_t.shape[1]), v_t.dtype)], axis=0)


def _flash_heads_t(q_ts, ks, v_ts, biases, valid, carry):
    ss = []
    for q_t, k, bias in zip(q_ts, ks, biases):
        s = _dot(k, q_t)
        s = s if bias is None else s + bias
        ss.append(s if valid is None else jnp.where(valid, s, NEG))
    m_news = [jnp.maximum(c[0], jnp.max(s, axis=0, keepdims=True)) for c, s in zip(carry, ss)]
    pvs = [_dot(v_t, jnp.exp(s - m_new).astype(BF16)) for v_t, s, m_new in zip(v_ts, ss, m_news)]
    return tuple((m_new, jnp.exp(c[0] - m_new) * c[1] + pv) for c, m_new, pv in zip(carry, m_news, pvs))


def _flash_finish(carry):
    acc = carry[1]
    d = acc.shape[0] - 8
    return acc[:d] / jnp.maximum(acc[d:d + 1], 1e-30)


def _rope(x, tab, half_only=False):
    c, sa, sb = tab[0], tab[1], tab[2]
    if half_only:
        first = _iota(x.shape, 1) < HEAD_DIM
        c = jnp.where(first, c, 1.0)
        sa = jnp.where(first, sa, 0.0)
        sb = jnp.where(first, sb, 0.0)
    return x * c + pltpu.roll(x, LANES - ROT_HALF, 1) * sa + pltpu.roll(x, ROT_HALF, 1) * sb


def _rope_heads(x, tab):
    return jnp.concatenate([_rope(x[:, :LANES], tab), _rope(x[:, LANES:], tab)], axis=1)


def _rank_lt(score, n_real, k):
    lane = _iota(score.shape, 1)
    cnt = jnp.zeros(score.shape, jnp.int32)
    for i in range(n_real):
        ci = score[:, i:i + 1]
        beats = (ci > score) | ((ci == score) & (lane > i))
        cnt = cnt + beats.astype(jnp.int32)
    return cnt < k


def _pick_tile(n, target, mult=8):
    if n <= target:
        return n
    for t in range(target, 0, -1):
        if n % t == 0 and t % mult == 0:
            return t
    return n


def _params(*sem):
    return pltpu.CompilerParams(dimension_semantics=sem, vmem_limit_bytes=VMEM_LIMIT)


def _rms_cast_kernel(x_ref, g_ref, o_ref):
    o_ref[...] = _rms(x_ref[...], g_ref[...]).astype(BF16)


def rms_cast(x, g):
    m, d = x.shape
    tm = _pick_tile(m, 1024, 16)
    return pl.pallas_call(
        _rms_cast_kernel, grid=(m // tm,),
        in_specs=[pl.BlockSpec((tm, d), lambda i: (i, 0)), pl.BlockSpec((1, d), lambda i: (0, 0))],
        out_specs=pl.BlockSpec((tm, d), lambda i: (i, 0)),
        out_shape=jax.ShapeDtypeStruct((m, d), BF16),
        compiler_params=_params("parallel"), name="rms_cast",
    )(x, g.reshape(1, d))


def _mm_kernel(a_ref, b_ref, o_ref, *, gate):
    y = _dot(a_ref[...], b_ref[...])
    o_ref[...] = (_sigmoid(y) if gate else y).astype(o_ref.dtype)


def mm(a, w, layer, col0, n, out_dtype=F32, gate=False):
    m, k = a.shape
    tm = _pick_tile(m, 1024, 16)
    tn = n if n <= 1536 else _pick_tile(n, 1024, LANES)
    assert col0 % tn == 0 and n % tn == 0
    c0 = col0 // tn
    return pl.pallas_call(
        functools.partial(_mm_kernel, gate=gate), grid=(m // tm, n // tn),
        in_specs=[pl.BlockSpec((tm, k), lambda i, j: (i, 0)),
                  pl.BlockSpec((None, k, tn), lambda i, j: (layer, 0, c0 + j))],
        out_specs=pl.BlockSpec((tm, tn), lambda i, j: (i, j)),
        out_shape=jax.ShapeDtypeStruct((m, n), out_dtype),
        compiler_params=_params("parallel", "parallel"), name="mm",
    )(a, w)


def _ffn_kernel(x_ref, g1_ref, wg_ref, wu_ref, wd_ref, g2_ref, o_ref, h_ref, acc_ref):
    f = pl.program_id(1)

    @pl.when(f == 0)
    def _():
        h_ref[...] = _rms(x_ref[...], g1_ref[...]).astype(BF16)
        acc_ref[...] = jnp.zeros_like(acc_ref)

    h = h_ref[...]
    gate = _dot(h, wg_ref[...])
    up = _dot(h, wu_ref[...])
    act = (gate * _sigmoid(gate) * up).astype(BF16)
    acc_ref[...] += _dot(act, wd_ref[...])

    @pl.when(f == pl.num_programs(1) - 1)
    def _():
        o_ref[...] = x_ref[...] + _rms(acc_ref[...], g2_ref[...])


def ffn(x, g_pre, wg, wu, wd, g_post, layer):
    m, d = x.shape
    dff = wg.shape[2]
    tm = _pick_tile(m, 512, 16)
    tf = _pick_tile(dff, FFN_TILE, LANES)
    return pl.pallas_call(
        _ffn_kernel, grid=(m // tm, dff // tf),
        in_specs=[pl.BlockSpec((tm, d), lambda i, f: (i, 0)),
                  pl.BlockSpec((1, d), lambda i, f: (0, 0)),
                  pl.BlockSpec((None, d, tf), lambda i, f: (layer, 0, f)),
                  pl.BlockSpec((None, d, tf), lambda i, f: (layer, 0, f)),
                  pl.BlockSpec((None, tf, d), lambda i, f: (layer, f, 0)),
                  pl.BlockSpec((1, d), lambda i, f: (0, 0))],
        out_specs=pl.BlockSpec((tm, d), lambda i, f: (i, 0)),
        out_shape=jax.ShapeDtypeStruct((m, d), F32),
        scratch_shapes=[pltpu.VMEM((tm, d), BF16), pltpu.VMEM((tm, d), F32)],
        compiler_params=_params("parallel", "arbitrary"), name="ffn",
    )(x, g_pre.reshape(1, d), wg, wu, wd, g_post.reshape(1, d))


def _merge_kernel(x_ref, o0_ref, o1_ref, o2_ref, o3_ref, gate_ref, wb_ref, wo_ref, g_ref, out_ref):
    d = x_ref.shape[1]
    merged = None
    for b, o_ref in enumerate((o0_ref, o1_ref, o2_ref, o3_ref)):
        term = gate_ref[:, b * d:(b + 1) * d].astype(F32) * _dot(o_ref[...], wb_ref[b])
        merged = term if merged is None else merged + term
    y = _dot(merged.astype(BF16), wo_ref[...])
    out_ref[...] = x_ref[...] + _rms(y, g_ref[...])


def merge(x, outs, gates, w_branch, w_out, g_post, layer):
    m, d = x.shape
    bw = outs[0].shape[1]
    tm = _pick_tile(m, 512, 16)
    row = lambda i: (i, 0)
    return pl.pallas_call(
        _merge_kernel, grid=(m // tm,),
        in_specs=[pl.BlockSpec((tm, d), row)] + [pl.BlockSpec((tm, bw), row)] * 4 + [
            pl.BlockSpec((tm, 4 * d), row),
            pl.BlockSpec((None, 4, bw, d), lambda i: (layer, 0, 0, 0)),
            pl.BlockSpec((None, d, d), lambda i: (layer, 0, 0)),
            pl.BlockSpec((1, d), lambda i: (0, 0))],
        out_specs=pl.BlockSpec((tm, d), row),
        out_shape=jax.ShapeDtypeStruct((m, d), F32),
        compiler_params=_params("parallel"), name="merge",
    )(x, *outs, gates, w_branch, w_out, g_post.reshape(1, d))


def _nsa_rope_kernel(z_ref, tab_ref, slc_ref, win_ref):
    z = z_ref[0]
    tab = tab_ref[...]
    slc_ref[0] = _rope(z[:, 384:512], tab, half_only=True)
    win_ref[0] = _rope(z[:, 512:640], tab, half_only=True)


def nsa_rope(z_nsa, tab):
    b, t, w = z_nsa.shape
    tt = _pick_tile(t, 512)
    out = jax.ShapeDtypeStruct((b, t, LANES), F32)
    return pl.pallas_call(
        _nsa_rope_kernel, grid=(b, t // tt),
        in_specs=[pl.BlockSpec((1, tt, w), lambda i, j: (i, j, 0)),
                  pl.BlockSpec((3, tt, LANES), lambda i, j: (0, j, 0))],
        out_specs=[pl.BlockSpec((1, tt, LANES), lambda i, j: (i, j, 0))] * 2,
        out_shape=[out, out],
        compiler_params=_params("parallel", "parallel"), name="nsa_rope",
    )(z_nsa, tab)


def _moba_rope_kernel(z_ref, tab_ref, kv_ref, mean_ref):
    z = z_ref[0]
    kr = _rope_heads(z[:, BRANCH_W:2 * BRANCH_W], tab_ref[...])
    kv_ref[0] = jnp.concatenate([kr, z[:, 2 * BRANCH_W:3 * BRANCH_W]], axis=1)
    mean_ref[0, 0] = jnp.sum(kr, axis=0, keepdims=True) * (1.0 / MOBA_BLOCK)


def moba_rope(z_moba, tab):
    b, t, w = z_moba.shape
    nb = t // MOBA_BLOCK
    return pl.pallas_call(
        _moba_rope_kernel, grid=(b, nb),
        in_specs=[pl.BlockSpec((1, MOBA_BLOCK, w), lambda i, j: (i, j, 0)),
                  pl.BlockSpec((3, MOBA_BLOCK, LANES), lambda i, j: (0, j, 0))],
        out_specs=[pl.BlockSpec((1, MOBA_BLOCK, 2 * BRANCH_W), lambda i, j: (i, j, 0)),
                   pl.BlockSpec((1, 1, 1, BRANCH_W), lambda i, j: (i, j, 0, 0))],
        out_shape=[jax.ShapeDtypeStruct((b, t, 2 * BRANCH_W), F32),
                   jax.ShapeDtypeStruct((b, nb, 1, BRANCH_W), F32)],
        compiler_params=_params("parallel", "parallel"), name="moba_rope",
    )(z_moba, tab)


def _compress(lines, w1_ref, pe_ref, w2_ref):
    outs = []
    for j, g in enumerate(lines):
        r = g.shape[0]
        ab = _dot(g.astype(BF16), w1_ref[j])
        pe = _dot(pe_ref[j].astype(BF16), w1_ref[j])
        half = ab.shape[1] // 2
        nxt = pltpu.roll(ab[:, half:], r - 1, 0)
        pre = ab[:, :half] + nxt + pe[0:1, :half] + pe[1:2, half:]
        hid = pre * _sigmoid(pre)
        outs.append(_dot(hid.astype(BF16), w2_ref[j]))
    return jnp.concatenate(outs, axis=1)


def _compress_kernel(gk_ref, gv_ref, w1_ref, pe_ref, w2_ref, o_ref):
    o_ref[0] = _compress((gk_ref[0], gv_ref[0]), w1_ref, pe_ref, w2_ref)


def compress(gk, gv, w1, pe2, w2, layer):
    b, r, w = gk.shape
    lay = lambda i: (layer, 0, 0, 0)
    return pl.pallas_call(
        _compress_kernel, grid=(b,),
        in_specs=[pl.BlockSpec((1, r, w), lambda i: (i, 0, 0)), pl.BlockSpec((1, r, w), lambda i: (i, 0, 0)),
                  pl.BlockSpec((None,) + w1.shape[1:], lay), pl.BlockSpec((None,) + pe2.shape[1:], lay),
                  pl.BlockSpec((None,) + w2.shape[1:], lay)],
        out_specs=pl.BlockSpec((1, r, LANES), lambda i: (i, 0, 0)),
        out_shape=jax.ShapeDtypeStruct((b, r, LANES), F32),
        compiler_params=_params("parallel"), name="nsa_compress",
    )(gk, gv, w1, pe2, w2)


def _rank_lt_rows(score, n_real, k):
    row = _iota(score.shape, 0)
    cnt = jnp.zeros(score.shape, jnp.int32)
    for i in range(n_real):
        ri = score[i:i + 1, :]
        beats = (ri > score) | ((ri == score) & (row > i))
        cnt = cnt + beats.astype(jnp.int32)
    return cnt < k


def _select_blocks(imp, t_pos, n_slc, k_sel):
    j = _iota(imp.shape, 1)
    own = lax.shift_right_logical(t_pos, 6)
    causal = (j * SLC_BLOCK <= t_pos) & (j < n_slc)
    forced = (j == 0) | (j == own) | (j == own - 1)
    score = jnp.where(forced, -NEG, jnp.where(causal, imp, NEG))
    return _rank_lt(score, n_slc, k_sel) & causal


def _nsa_attn_kernel(z_ref, tab_ref, cmp_ref, slc_ref, win_ref, cover_ref, o_ref, sel_s,
                     *, tq, tk, n_cmp, n_slc, k_sel):
    qi = pl.program_id(1)
    z = z_ref[0]
    q = z[:, :BRANCH_W] * SCALE
    qr = _rope_heads(q, tab_ref[...])
    heads = [slice(h * HEAD_DIM, (h + 1) * HEAD_DIM) for h in range(N_HEADS)]

    cmpkv = cmp_ref[0]
    ck = cmpkv[:, :HEAD_DIM].astype(BF16)
    cv = cmpkv[:, HEAD_DIM:].astype(BF16)
    t_row = qi * tq + _iota((1, tq), 1)
    n = _iota((cmpkv.shape[0], 1), 0)
    cmask = (n * CMP_STRIDE + (CMP_BLOCK - 1) <= t_row) & (n < n_cmp)
    o_cmp, psum = [], None
    for hs in heads:
        s = jnp.where(cmask, _dot_nt(ck, q[:, hs].astype(BF16)), NEG)
        p = jnp.where(cmask, jnp.exp(s - jnp.max(s, axis=0, keepdims=True)), 0.0)
        p = p / jnp.maximum(jnp.sum(p, axis=0, keepdims=True), 1e-30)
        o_cmp.append(_dot_tn(cv, p.astype(BF16)))
        psum = p if psum is None else psum + p
    p_hi, p_lo = _split(psum)
    imp = _dot(cover_ref[...], p_hi) + _dot(cover_ref[...], p_lo)
    j = _iota(imp.shape, 0)
    own = lax.shift_right_logical(t_row, 6)
    causal = (j * SLC_BLOCK <= t_row) & (j < n_slc)
    forced = (j == 0) | (j == own) | (j == own - 1)
    score = jnp.where(forced, -NEG, jnp.where(causal, imp, NEG))
    picked = _rank_lt_rows(score, n_slc, k_sel) & causal
    sel_s[...] = jnp.where(picked, 0.0, NEG)

    qr_t = qr.T.astype(BF16)
    qrb = [qr_t[hs, :] for hs in heads]
    init = tuple((jnp.full((1, tq), NEG, F32), jnp.zeros((HEAD_DIM + 8, tq), F32)) for _ in heads)
    blocks_per_tile = tk // SLC_BLOCK
    tiles_per_q = tq // tk

    def tile(ref, kt):
        rows = ref[0, pl.ds(pl.multiple_of(kt * tk, tk), tk), :]
        return [rows[:, :HEAD_DIM].astype(BF16)] * N_HEADS, [_with_ones(rows.T[HEAD_DIM:, :].astype(BF16))] * N_HEADS

    def slc_body(kt, carry, diagonal):
        ks, v_ts = tile(slc_ref, kt)
        bias = jnp.concatenate(
            [jnp.broadcast_to(sel_s[pl.ds(kt * blocks_per_tile + u, 1), :], (SLC_BLOCK, tq))
             for u in range(blocks_per_tile)], axis=0)
        valid = (kt * tk + _iota((tk, 1), 0) <= t_row) if diagonal else None
        return _flash_heads_t(qrb, ks, v_ts, [bias] * N_HEADS, valid, carry)

    def win_body(kt, carry, banded):
        ks, v_ts = tile(win_ref, kt)
        d = t_row - (kt * tk + _iota((tk, 1), 0))
        valid = ((d >= 0) & (d <= WINDOW)) if banded else None
        return _flash_heads_t(qrb, ks, v_ts, [None] * N_HEADS, valid, carry)

    first_diag = qi * tiles_per_q
    hi = first_diag + tiles_per_q
    slc = lax.fori_loop(0, first_diag, functools.partial(slc_body, diagonal=False), init)
    slc = lax.fori_loop(first_diag, hi, functools.partial(slc_body, diagonal=True), slc)
    win_lo = jnp.maximum(first_diag - WINDOW // tk, 0)
    win_free = jnp.maximum(first_diag - (WINDOW - tq) // tk, 0)
    win = lax.fori_loop(win_lo, win_free, functools.partial(win_body, banded=True), init)
    win = lax.fori_loop(win_free, first_diag, functools.partial(win_body, banded=False), win)
    win = lax.fori_loop(first_diag, hi, functools.partial(win_body, banded=True), win)

    g = _sigmoid(z[:, 640:640 + LANES]).T
    outs = []
    for h in range(N_HEADS):
        outs.append(g[3 * h:3 * h + 1, :] * o_cmp[h] + g[3 * h + 1:3 * h + 2, :] * _flash_finish(slc[h])
                    + g[3 * h + 2:3 * h + 3, :] * _flash_finish(win[h]))
    o_ref[0] = jnp.concatenate(outs, axis=0).T.astype(o_ref.dtype)


def nsa_attn(z_nsa, tab, cmp_kv, new_slc, new_win, cover, n_cmp, n_slc):
    b, t, w = z_nsa.shape
    tq = _pick_tile(t, 256)
    tk = ATTN_KEY_TILE
    assert tq % tk == 0 and tk % SLC_BLOCK == 0 and WINDOW % tk == 0 and WINDOW >= tq
    kern = functools.partial(_nsa_attn_kernel, tq=tq, tk=tk, n_cmp=n_cmp, n_slc=n_slc,
                             k_sel=min(N_SELECT, n_slc))
    full = lambda i, j: (i, 0, 0)
    const = lambda i, j: (0, 0)
    return pl.pallas_call(
        kern, grid=(b, t // tq),
        in_specs=[pl.BlockSpec((1, tq, w), lambda i, j: (i, j, 0)),
                  pl.BlockSpec((3, tq, LANES), lambda i, j: (0, j, 0)),
                  pl.BlockSpec((1,) + cmp_kv.shape[1:], full),
                  pl.BlockSpec((1, t, LANES), full), pl.BlockSpec((1, t, LANES), full),
                  pl.BlockSpec(cover.shape, const)],
        out_specs=pl.BlockSpec((1, tq, BRANCH_W), lambda i, j: (i, j, 0)),
        out_shape=jax.ShapeDtypeStruct((b, t, BRANCH_W), BF16),
        scratch_shapes=[pltpu.VMEM((cover.shape[0], tq), F32)],
        compiler_params=_params("parallel", "parallel"), name="nsa_attn",
    )(z_nsa, tab, cmp_kv, new_slc, new_win, cover)


def _moba_attn_kernel(z_ref, tab_ref, kv_ref, mean_ref, o_ref, pick_s, *, tq, tk, nb, topk):
    qi = pl.program_id(1)
    q = z_ref[0][:, :BRANCH_W] * SCALE
    qr = _rope_heads(q, tab_ref[...])
    means = mean_ref[0]
    heads = [slice(h * HEAD_DIM, (h + 1) * HEAD_DIM) for h in range(N_HEADS)]

    t_row = qi * tq + _iota((1, tq), 1)
    own = lax.shift_right_logical(t_row, 8)
    past = _iota((means.shape[0], 1), 0) < own
    for h, hs in enumerate(heads):
        gate = _dot3(means[:, hs], qr[:, hs], _dot_nt)
        picked = _rank_lt_rows(jnp.where(past, gate, NEG), nb, topk) & past
        pick_s[h] = jnp.where(picked, 0.0, NEG)

    qr_t = qr.T.astype(BF16)
    qrb = [qr_t[hs, :] for hs in heads]
    init = tuple((jnp.full((1, tq), NEG, F32), jnp.zeros((HEAD_DIM + 8, tq), F32)) for _ in heads)
    tiles_per_block = MOBA_BLOCK // tk

    def body(kt, carry, own_block):
        rows = kv_ref[0, pl.ds(pl.multiple_of(kt * tk, tk), tk), :]
        v_t = rows[:, BRANCH_W:].T.astype(BF16)
        ks = [rows[:, hs].astype(BF16) for hs in heads]
        v_ts = [_with_ones(v_t[hs, :]) for hs in heads]
        if own_block:
            return _flash_heads_t(qrb, ks, v_ts, [None] * N_HEADS, kt * tk + _iota((tk, 1), 0) <= t_row, carry)
        blk = kt // tiles_per_block
        return _flash_heads_t(qrb, ks, v_ts, [pick_s[h, pl.ds(blk, 1), :] for h in range(N_HEADS)], None, carry)

    first_own = qi * (tq // tk)
    res = lax.fori_loop(0, first_own, functools.partial(body, own_block=False), init)
    res = lax.fori_loop(first_own, first_own + tq // tk, functools.partial(body, own_block=True), res)
    o_ref[0] = jnp.concatenate([_flash_finish(r) for r in res], axis=0).T.astype(o_ref.dtype)


def moba_attn(z_moba, tab, new_moba, means):
    b, t, w = z_moba.shape
    assert t % MOBA_BLOCK == 0
    tq = MOBA_BLOCK
    tk = ATTN_KEY_TILE
    assert MOBA_BLOCK % tk == 0
    nb = t // MOBA_BLOCK
    kern = functools.partial(_moba_attn_kernel, tq=tq, tk=tk, nb=nb, topk=min(MOBA_TOPK, nb))
    full = lambda i, j: (i, 0, 0)
    return pl.pallas_call(
        kern, grid=(b, t // tq),
        in_specs=[pl.BlockSpec((1, tq, w), lambda i, j: (i, j, 0)),
                  pl.BlockSpec((3, tq, LANES), lambda i, j: (0, j, 0)),
                  pl.BlockSpec((1, t, 2 * BRANCH_W), full),
                  pl.BlockSpec((1, nb, BRANCH_W), full)],
        out_specs=pl.BlockSpec((1, tq, BRANCH_W), lambda i, j: (i, j, 0)),
        out_shape=jax.ShapeDtypeStruct((b, t, BRANCH_W), BF16),
        scratch_shapes=[pltpu.VMEM((N_HEADS, nb, tq), F32)],
        compiler_params=_params("parallel", "parallel"), name="moba_attn",
    )(z_moba, tab, new_moba, means)


def _cross_attn_kernel(q_ref, mem_ref, o_ref):
    q = q_ref[0] * SCALE
    mem = mem_ref[0]
    outs = []
    for h in range(N_HEADS):
        hs = slice(h * HEAD_DIM, (h + 1) * HEAD_DIM)
        s = _dot_nt(q[:, hs].astype(BF16), mem[:, hs].astype(BF16))
        p = jnp.exp(s - jnp.max(s, axis=-1, keepdims=True))
        p = p / jnp.sum(p, axis=-1, keepdims=True)
        v = mem[:, BRANCH_W + h * HEAD_DIM:BRANCH_W + (h + 1) * HEAD_DIM].astype(BF16)
        outs.append(_dot(p.astype(BF16), v))
    o_ref[0] = jnp.concatenate(outs, axis=1).astype(o_ref.dtype)


def _cross_attn_t_kernel(q_ref, mem_ref, o_ref):
    q_t = (q_ref[0] * SCALE).T.astype(BF16)
    mem = mem_ref[0]
    v_t = mem[:, BRANCH_W:].T.astype(BF16)
    heads = [slice(h * HEAD_DIM, (h + 1) * HEAD_DIM) for h in range(N_HEADS)]
    ss = [_dot(mem[:, hs].astype(BF16), q_t[hs, :]) for hs in heads]
    ps = [jnp.exp(s - jnp.max(s, axis=0, keepdims=True)).astype(BF16) for s in ss]
    accs = [_dot(_with_ones(v_t[hs, :]), p) for hs, p in zip(heads, ps)]
    outs = [a[:HEAD_DIM] / a[HEAD_DIM:HEAD_DIM + 1] for a in accs]
    o_ref[0] = jnp.concatenate(outs, axis=0).T.astype(o_ref.dtype)


def cross_attn(q, mem_kv):
    b, t, w = q.shape
    n_mem = mem_kv.shape[1]
    lanes_ok = t % (2 * LANES) == 0 and n_mem % LANES == 0
    tq = 2 * LANES if lanes_ok else _pick_tile(t, 512)
    return pl.pallas_call(
        _cross_attn_t_kernel if lanes_ok else _cross_attn_kernel, grid=(b, t // tq),
        in_specs=[pl.BlockSpec((1, tq, w), lambda i, j: (i, j, 0)),
                  pl.BlockSpec((1, n_mem, 2 * BRANCH_W), lambda i, j: (i, 0, 0))],
        out_specs=pl.BlockSpec((1, tq, BRANCH_W), lambda i, j: (i, j, 0)),
        out_shape=jax.ShapeDtypeStruct((b, t, BRANCH_W), BF16),
        compiler_params=_params("parallel", "parallel"), name="cross_attn",
    )(q, mem_kv)


def _head_sum(x, bd_ref):
    return _dot(x.astype(BF16), bd_ref[...])


def _rwkv_prep(z, zprev, vec_ref, wa_ref, g2_ref, bd_ref):
    mu = vec_ref[0:1, :]
    zm = z + (zprev - z) * mu
    r = zm[:, 0:256]
    k = zm[:, 256:512]
    v = zm[:, 512:768]
    lora = zm[:, 768:896]
    lora = jnp.where(_iota(lora.shape, 1) < 64, jnp.tanh(lora), lora)
    wa = _dot(lora.astype(BF16), wa_ref[...])
    wl = vec_ref[1:2, 0:256] + wa[:, :256]
    x = -wl
    softplus = jnp.maximum(x, 0.0) + jnp.log(1.0 + jnp.exp(-jnp.abs(x)))
    logdecay = -jnp.exp(-softplus - 0.5)
    a = _sigmoid(vec_ref[1:2, 256:512] + wa[:, 256:])
    g = _dot(_sigmoid(zm[:, 896:1152]).astype(BF16), g2_ref[...])
    kkr = k * vec_ref[2:3, 0:256]
    kk = kkr * lax.rsqrt(jnp.maximum(_head_sum(kkr * kkr, bd_ref), 1e-24))
    k2 = k * (1.0 + (a - 1.0) * vec_ref[3:4, 0:256])
    bonus = _head_sum(r * k2 * vec_ref[4:5, 0:256], bd_ref) * v
    return r, k2, v, kk, kk * a, logdecay, g, bonus


def _rwkv_finish(y, bonus, g, vec_ref, bd_ref):
    mu = _head_sum(y, bd_ref) * (1.0 / HEAD_DIM)
    d = y - mu
    var = _head_sum(d * d, bd_ref) * (1.0 / HEAD_DIM)
    yn = d * lax.rsqrt(var + RWKV_GN_EPS) * vec_ref[5:6, 0:256] + vec_ref[6:7, 0:256]
    return (yn + bonus) * g


def _dotc(a, b, dot=_dot):
    return dot(a.astype(BF16), b.astype(BF16))


def _tri_inv_all(lows):
    c = lows[0].shape[0]
    eye = (_iota((c, c), 0) == _iota((c, c), 1)).astype(F32)
    ts = [eye - low for low in lows]
    lps = [_dotc(low, low) for low in lows]
    n = 2
    while n < c:
        ts = [t + _dotc(t, lp) for t, lp in zip(ts, lps)]
        n *= 2
        if n < c:
            lps = [_dotc(lp, lp) for lp in lps]
    return ts


RWKV_GROUP = 8


def _rwkv_seq_kernel(z_ref, shift_ref, s0_ref, vec_ref, wa_ref, g2_ref, bd_ref, tri_ref,
                     o_ref, s_ref, last_ref, r_s, k_s, v_s, kk_s, b_s, ld_s, y_s, pe_s, *, tt):
    ti = pl.program_id(1)

    @pl.when(ti == 0)
    def _():
        last_ref[...] = shift_ref[0]
        s_ref[0] = s0_ref[0]

    z = z_ref[0]
    rolled = pltpu.roll(z, 1, 0)
    zprev = jnp.where(_iota(z.shape, 0) == 0, last_ref[...], rolled)
    last_ref[...] = z[tt - 1:tt, :]
    r, k2, v, kk, b, ld, g, bonus = _rwkv_prep(z, zprev, vec_ref, wa_ref, g2_ref, bd_ref)
    r_s[...] = r
    k_s[...] = k2
    v_s[...] = v
    kk_s[...] = kk
    b_s[...] = b
    ld_s[...] = ld

    c = RWKV_CHUNK
    ri = _iota((c, c), 0)
    ci = _iota((c, c), 1)
    strict = ri > ci
    incl = ri >= ci

    eye = (ri == ci).astype(F32)
    heads = [slice(h * HEAD_DIM, (h + 1) * HEAD_DIM) for h in range(N_HEADS)]
    n_chunks = tt // c
    group = RWKV_GROUP if n_chunks % RWKV_GROUP == 0 else 1

    def prepare(gi, carry):
        probs = []
        for j in range(group):
            ck = gi * group + j
            rows = pl.ds(pl.multiple_of(ck * c, c), c)
            ldc = ld_s[rows, :]
            cum = _dot3(tri_ref[...], ldc)
            p_in = jnp.exp(cum)
            inv_p = jnp.exp(-cum)
            kkt = kk_s[rows, :] * jnp.exp(cum - ldc)
            bt = b_s[rows, :] * inv_p
            kt = k_s[rows, :] * inv_p
            rt = r_s[rows, :] * p_in
            vv = v_s[rows, :]
            p_end = p_in[c - 1:c, :]
            pe_s[pl.ds(ck, 1), :] = p_end
            for hs in heads:
                probs.append(dict(kkt=kkt[:, hs], bt=bt[:, hs], kt=kt[:, hs], rt=rt[:, hs], v=vv[:, hs],
                                  p_end=p_end[:, hs]))
        blocks = [_dotc(jnp.concatenate([p['kkt'], p['rt']], axis=0),
                        jnp.concatenate([p['bt'], p['kt']], axis=0), _dot_nt) for p in probs]
        lows = [jnp.where(strict, bl[:c, :c], 0.0) for bl in blocks]
        gvs = [_dotc(jnp.where(strict, bl[:c, c:], 0.0), p['v']) for bl, p in zip(blocks, probs)]
        tinvs = _tri_inv_all(lows)
        ke_ue = [_dotc(t, jnp.concatenate([p['kkt'], gv], axis=1)) for t, p, gv in zip(tinvs, probs, gvs)]
        corr = [_dotc(jnp.where(incl, bl[c:, :c], 0.0), ku) for bl, ku in zip(blocks, ke_ue)]
        avs = [_dotc(jnp.where(incl, bl[c:, c:], 0.0), p['v']) for bl, p in zip(blocks, probs)]
        xps = [_dotc(ku[:, :HEAD_DIM], p['bt'], _dot_tn) * p['p_end'] for ku, p in zip(ke_ue, probs)]
        bcs = [_dotc(jnp.concatenate([p['v'], ku[:, HEAD_DIM:]], axis=0),
                     jnp.concatenate([p['kt'], -p['bt']], axis=0), _dot_tn) * p['p_end']
               for ku, p in zip(ke_ue, probs)]
        res = [p['rt'] - cr[:, :HEAD_DIM] for p, cr in zip(probs, corr)]
        y0s = [av - cr[:, HEAD_DIM:] for av, cr in zip(avs, corr)]
        for j in range(group):
            rows = pl.ds(pl.multiple_of((gi * group + j) * c, c), c)
            sl = slice(j * N_HEADS, (j + 1) * N_HEADS)
            r_s[rows, :] = jnp.concatenate(res[sl], axis=1)
            v_s[rows, :] = jnp.concatenate(y0s[sl], axis=1)
            kk_s[rows, :] = jnp.concatenate(xps[sl], axis=1)
            b_s[rows, :] = jnp.concatenate(bcs[sl], axis=1)
        return carry

    lax.fori_loop(0, n_chunks // group, prepare, 0)

    def advance(ck, carry):
        rows = pl.ds(pl.multiple_of(ck * c, c), c)
        re, y0, xp, bc = r_s[rows, :], v_s[rows, :], kk_s[rows, :], b_s[rows, :]
        p_end = pe_s[pl.ds(ck, 1), :]
        states = [s_ref[0, h] for h in range(N_HEADS)]
        ys = [_dotc(re[:, hs], s_h, _dot_nt) + y0[:, hs] for hs, s_h in zip(heads, states)]
        for h, (hs, s_h) in enumerate(zip(heads, states)):
            s_ref[0, h] = s_h * p_end[:, hs] - _dotc(s_h, xp[:, hs]) + bc[:, hs]
        y_s[rows, :] = jnp.concatenate(ys, axis=1)
        return carry

    lax.fori_loop(0, n_chunks, advance, 0, unroll=True)
    o_ref[0] = _rwkv_finish(y_s[...], bonus, g, vec_ref, bd_ref).astype(o_ref.dtype)


def rwkv_seq(z, shift, s0, vec, wa, g2, bd, tri):
    b, t, w = z.shape
    tt = _pick_tile(t, 512, RWKV_CHUNK)
    assert tt % RWKV_CHUNK == 0
    const = lambda i, j: (0, 0)
    scr = pltpu.VMEM((tt, BRANCH_W), F32)
    return pl.pallas_call(
        functools.partial(_rwkv_seq_kernel, tt=tt), grid=(b, t // tt),
        in_specs=[pl.BlockSpec((1, tt, w), lambda i, j: (i, j, 0)),
                  pl.BlockSpec((1, 1, w), lambda i, j: (i, 0, 0)),
                  pl.BlockSpec((1, N_HEADS, HEAD_DIM, HEAD_DIM), lambda i, j: (i, 0, 0, 0)),
                  pl.BlockSpec(vec.shape, const), pl.BlockSpec(wa.shape, const),
                  pl.BlockSpec(g2.shape, const), pl.BlockSpec(bd.shape, const),
                  pl.BlockSpec(tri.shape, const)],
        out_specs=[pl.BlockSpec((1, tt, BRANCH_W), lambda i, j: (i, j, 0)),
                   pl.BlockSpec((1, N_HEADS, HEAD_DIM, HEAD_DIM), lambda i, j: (i, 0, 0, 0))],
        out_shape=[jax.ShapeDtypeStruct((b, t, BRANCH_W), BF16),
                   jax.ShapeDtypeStruct((b, N_HEADS, HEAD_DIM, HEAD_DIM), F32)],
        scratch_shapes=[pltpu.VMEM((1, w), F32)] + [scr] * 7 + [pltpu.VMEM((tt // RWKV_CHUNK, BRANCH_W), F32)],
        compiler_params=_params("parallel", "arbitrary"), name="rwkv_seq",
    )(z, shift, s0, vec, wa, g2, bd, tri)


def _rwkv_step_kernel(z_ref, shift_ref, s0_ref, vec_ref, wa_ref, g2_ref, bd_ref, o_ref, s_ref):
    rows = 8
    z = jnp.broadcast_to(z_ref[0], (rows, z_ref.shape[2]))
    zprev = jnp.broadcast_to(shift_ref[0], z.shape)
    r, k2, v, kk, b, ld, g, bonus = _rwkv_prep(z, zprev, vec_ref, wa_ref, g2_ref, bd_ref)
    decay = jnp.exp(ld)
    eye = _iota((HEAD_DIM, HEAD_DIM), 0) == _iota((HEAD_DIM, HEAD_DIM), 1)
    ys = []
    for h in range(N_HEADS):
        hs = slice(h * HEAD_DIM, (h + 1) * HEAD_DIM)
        s_h = s0_ref[0, h]
        sk = jnp.sum(s_h * kk[0:1, hs], axis=1, keepdims=True)
        v_col = jnp.sum(jnp.where(eye, v[0:1, hs], 0.0), axis=1, keepdims=True)
        s_new = s_h * decay[0:1, hs] - sk * b[0:1, hs] + v_col * k2[0:1, hs]
        s_ref[0, h] = s_new
        y_col = jnp.sum(s_new * r[0:1, hs], axis=1, keepdims=True)
        ys.append(jnp.sum(jnp.where(eye, y_col, 0.0), axis=0, keepdims=True))
    y = jnp.broadcast_to(jnp.concatenate(ys, axis=1), (rows, BRANCH_W))
    o_ref[0] = _rwkv_finish(y, bonus, g, vec_ref, bd_ref)[0:1].astype(o_ref.dtype)


def rwkv_step(z, shift, s0, vec, wa, g2, bd):
    b, _, w = z.shape
    const = lambda i: (0, 0)
    return pl.pallas_call(
        _rwkv_step_kernel, grid=(b,),
        in_specs=[pl.BlockSpec((1, 1, w), lambda i: (i, 0, 0)),
                  pl.BlockSpec((1, 1, w), lambda i: (i, 0, 0)),
                  pl.BlockSpec((1, N_HEADS, HEAD_DIM, HEAD_DIM), lambda i: (i, 0, 0, 0)),
                  pl.BlockSpec(vec.shape, const), pl.BlockSpec(wa.shape, const),
                  pl.BlockSpec(g2.shape, const), pl.BlockSpec(bd.shape, const)],
        out_specs=[pl.BlockSpec((1, 1, BRANCH_W), lambda i: (i, 0, 0)),
                   pl.BlockSpec((1, N_HEADS, HEAD_DIM, HEAD_DIM), lambda i: (i, 0, 0, 0))],
        out_shape=[jax.ShapeDtypeStruct((b, 1, BRANCH_W), BF16),
                   jax.ShapeDtypeStruct((b, N_HEADS, HEAD_DIM, HEAD_DIM), F32)],
        compiler_params=_params("parallel"), name="rwkv_step",
    )(z, shift, s0, vec, wa, g2, bd)


def _stack_heads(x):
    rows = [x[:, h * HEAD_DIM:(h + 1) * HEAD_DIM] for h in range(N_HEADS)]
    return jnp.concatenate(rows + [jnp.zeros((8 - N_HEADS, HEAD_DIM), x.dtype)], axis=0)


def _unstack_heads(x):
    return jnp.concatenate([x[h:h + 1, :] for h in range(N_HEADS)], axis=1)


def _column(row):
    n = row.shape[1]
    eye = _iota((n, n), 0) == _iota((n, n), 1)
    return jnp.sum(jnp.where(eye, row, 0.0), axis=1, keepdims=True)


def _to_row(col):
    n = col.shape[0]
    eye = _iota((n, n), 0) == _iota((n, n), 1)
    return jnp.sum(jnp.where(eye, col, 0.0), axis=0, keepdims=True)


def _page_scores(k_page, q_cols, s_ref, n_pages):
    def body(p, c):
        for h, qc in enumerate(q_cols):
            s_ref[h, pl.ds(p, 1), :] = jnp.sum(k_page(p, h) * qc, axis=0, keepdims=True)
        return c

    lax.fori_loop(0, n_pages, body, 0, unroll=4)


def _page_values(v_page, p_ref, n_pages):
    def body(p, accs):
        return tuple(acc + v_page(p, h) * p_ref[h, pl.ds(p, 1), :] for h, acc in enumerate(accs))

    init = tuple(jnp.zeros((HEAD_DIM, p_ref.shape[2]), F32) for _ in range(N_HEADS))
    return [jnp.sum(a, axis=1, keepdims=True) for a in lax.fori_loop(0, n_pages, body, init, unroll=4)]


def _softmax_with_new(s, mask, s_new):
    s = jnp.where(mask, s, NEG)
    m = jnp.maximum(jnp.max(jnp.max(s, axis=1, keepdims=True), axis=0, keepdims=True), s_new)
    p = jnp.where(mask, jnp.exp(s - m), 0.0)
    p_new = jnp.exp(s_new - m)
    den = jnp.maximum(jnp.sum(jnp.sum(p, axis=1, keepdims=True), axis=0, keepdims=True) + p_new, 1e-30)
    return p / den, p_new / den


def _nsa_sample_kernel(pt_ref, z_ref, tab_ref, win_ref, w1_ref, pe_ref, w2_ref, cover_ref, pj_ref, hr_ref,
                       cmp_hbm, slc_hbm, o_ref, slc_row_ref, win_row_ref,
                       cbuf, sbuf, rows_s, lines_s, s_s, p_s, sem,
                       *, base, n_pages, page, past, n_cmp, n_slc, k_sel):
    bi = pl.program_id(0)
    n_lines = past // CMP_STRIDE
    slot = lax.rem(bi, 2)

    def copies(seq, sl, p):
        pg = pt_ref[seq * n_pages + p] + base
        return (pltpu.make_async_copy(cmp_hbm.at[pg], cbuf.at[sl, p], sem.at[sl, 0]),
                pltpu.make_async_copy(slc_hbm.at[pg], sbuf.at[sl, p], sem.at[sl, 1]))

    def start_fetch(seq, sl):
        for p in range(n_pages):
            for cp in copies(seq, sl, p):
                cp.start()

    @pl.when(bi == 0)
    def _():
        start_fetch(0, 0)

    @pl.when(bi + 1 < pl.num_programs(0))
    def _():
        start_fetch(bi + 1, 1 - slot)

    z = z_ref[0]
    tab = tab_ref[...]
    q = z[:, :BRANCH_W] * SCALE
    q4 = _stack_heads(q)
    q4r_f32 = _stack_heads(_rope_heads(q, tab))
    q4r = q4r_f32.astype(BF16)
    slc_row = _rope(z[:, 384:512], tab, half_only=True)
    win_row = _rope(z[:, 512:640], tab, half_only=True)
    slc_row_ref[0] = slc_row
    win_row_ref[0] = win_row

    for p in range(n_pages):
        for cp in copies(bi, slot, p):
            cp.wait()

    def xpose(p, c):
        rows_s[pl.ds(pl.multiple_of(p * page, page), page), :] = cbuf[slot, p].T
        return c

    lax.fori_loop(0, n_pages, xpose, 0, unroll=8)
    first = _iota((n_lines, LANES), 1) < HEAD_DIM
    for i in range(CMP_STRIDE // 2):
        a = rows_s[pl.ds(2 * i, n_lines, stride=CMP_STRIDE), :]
        b = rows_s[pl.ds(2 * i + 1, n_lines, stride=CMP_STRIDE), :]
        lines_s[0, :, i * LANES:(i + 1) * LANES] = jnp.where(first, a, pltpu.roll(b, HEAD_DIM, 1))
        lines_s[1, :, i * LANES:(i + 1) * LANES] = jnp.where(first, pltpu.roll(a, HEAD_DIM, 1), b)

    cmpkv = _compress((lines_s[0], lines_s[1]), w1_ref, pe_ref, w2_ref)
    ck = cmpkv[:, :HEAD_DIM].astype(BF16)
    cv = cmpkv[:, HEAD_DIM:].astype(BF16)
    n = _iota((1, cmpkv.shape[0]), 1)
    cmask = (n * CMP_STRIDE + (CMP_BLOCK - 1) <= past) & (n < n_cmp)
    p_cmp = _msoftmax(_dot_nt(q4.astype(BF16), ck), cmask)
    o_cmp = _dot(p_cmp.astype(BF16), cv)
    psum = jnp.sum(jnp.where(_iota(p_cmp.shape, 0) < N_HEADS, p_cmp, 0.0), axis=0, keepdims=True)
    imp = _dot2_exact_rhs(jnp.broadcast_to(psum, p_cmp.shape), cover_ref[...])[0:1]
    t_pos = jnp.full((1, 1), past, jnp.int32)
    sel = _select_blocks(imp, t_pos, n_slc, k_sel)

    def with_new_key(s_past, mask, new_row, v_past):
        s_new = jnp.sum(q4r_f32 * new_row[:, :HEAD_DIM], axis=1, keepdims=True)
        s_past = jnp.where(mask, s_past, NEG)
        m = jnp.maximum(jnp.max(s_past, axis=1, keepdims=True), s_new)
        p_past = jnp.where(mask, jnp.exp(s_past - m), 0.0)
        p_new = jnp.exp(s_new - m)
        den = jnp.maximum(jnp.sum(p_past, axis=1, keepdims=True) + p_new, 1e-30)
        return _dot((p_past / den).astype(BF16), v_past) + (p_new / den) * new_row[:, HEAD_DIM:]

    sel_pages = _dot((pj_ref[...] * sel.astype(F32)).astype(BF16), hr_ref[...]) > 0.5
    _page_scores(lambda p, h: sbuf[slot, p, 0], [_column(q4r_f32[h:h + 1, :]) for h in range(N_HEADS)], s_s,
                 n_pages)
    p_new = []
    for h in range(N_HEADS):
        s_new = jnp.sum(q4r_f32[h:h + 1, :] * slc_row[:, :HEAD_DIM], axis=1, keepdims=True)
        p_s[h], pn = _softmax_with_new(s_s[h], sel_pages, s_new)
        p_new.append(pn)
    o_slc = [_to_row(col) + p_new[h] * slc_row[:, HEAD_DIM:]
             for h, col in enumerate(_page_values(lambda p, h: sbuf[slot, p, 1], p_s, n_pages))]
    wrows = win_ref[0]
    wk = wrows[:, :HEAD_DIM].astype(BF16)
    wv = wrows[:, HEAD_DIM:].astype(BF16)
    o_win = with_new_key(_dot_nt(q4r, wk), jnp.full((1, wrows.shape[0]), True), win_row, wv)

    g = _sigmoid(z[:, 640:640 + 3 * N_HEADS])
    outs = []
    for h in range(N_HEADS):
        outs.append(g[:, 3 * h:3 * h + 1] * o_cmp[h:h + 1] + g[:, 3 * h + 1:3 * h + 2] * o_slc[h]
                    + g[:, 3 * h + 2:3 * h + 3] * o_win[h:h + 1])
    o_ref[0] = jnp.concatenate(outs, axis=1).astype(o_ref.dtype)


def nsa_sample(page_table, z_nsa, tab, win, w1c, pe2, w2c, cover, pj, hr, cache_cmp, cache_slc,
               layer, n_pool, page, n_cmp, n_slc):
    b = z_nsa.shape[0]
    n_pages = page_table.shape[1]
    past = n_pages * page
    kern = functools.partial(_nsa_sample_kernel, base=layer * n_pool, n_pages=n_pages, page=page, past=past,
                             n_cmp=n_cmp, n_slc=n_slc, k_sel=min(N_SELECT, n_slc))
    const = lambda i, pt: (0, 0)
    row = lambda i, pt: (i, 0, 0)
    lay = lambda i, pt: (layer, 0, 0, 0)
    grid_spec = pltpu.PrefetchScalarGridSpec(
        num_scalar_prefetch=1, grid=(b,),
        in_specs=[pl.BlockSpec((1, 1, z_nsa.shape[2]), row),
                  pl.BlockSpec(tab.shape, lambda i, pt: (0, 0, 0)),
                  pl.BlockSpec((1,) + win.shape[1:], row),
                  pl.BlockSpec((None,) + w1c.shape[1:], lay), pl.BlockSpec((None,) + pe2.shape[1:], lay),
                  pl.BlockSpec((None,) + w2c.shape[1:], lay), pl.BlockSpec(cover.shape, const),
                  pl.BlockSpec(pj.shape, const), pl.BlockSpec(hr.shape, const),
                  pl.BlockSpec(memory_space=pl.ANY), pl.BlockSpec(memory_space=pl.ANY)],
        out_specs=[pl.BlockSpec((1, 1, BRANCH_W), row), pl.BlockSpec((1, 1, LANES), row),
                   pl.BlockSpec((1, 1, LANES), row)],
        scratch_shapes=[pltpu.VMEM((2, n_pages, 2 * HEAD_DIM, page), F32),
                        pltpu.VMEM((2, n_pages, 2, HEAD_DIM, page), F32),
                        pltpu.VMEM((past, 2 * HEAD_DIM), F32),
                        pltpu.VMEM((2, past // CMP_STRIDE, CMP_STRIDE * HEAD_DIM), F32),
                        pltpu.VMEM((N_HEADS, n_pages, page), F32),
                        pltpu.VMEM((N_HEADS, n_pages, page), F32),
                        pltpu.SemaphoreType.DMA((2, 2))])
    return pl.pallas_call(
        kern, grid_spec=grid_spec,
        out_shape=[jax.ShapeDtypeStruct((b, 1, BRANCH_W), BF16),
                   jax.ShapeDtypeStruct((b, 1, LANES), F32), jax.ShapeDtypeStruct((b, 1, LANES), F32)],
        compiler_params=_params("arbitrary"), name="nsa_sample",
    )(page_table.reshape(-1), z_nsa, tab, win, w1c, pe2, w2c, cover, pj, hr, cache_cmp, cache_slc)


def _moba_sample_kernel(pt_ref, z_ref, tab_ref, kv_hbm, o_ref, row_ref, buf, s_s, p_s, sem,
                        *, base, n_pages, pages_per_block, topk):
    bi = pl.program_id(0)
    slot = lax.rem(bi, 2)

    def copy(seq, sl, p):
        pg = pt_ref[seq * n_pages + p] + base
        return pltpu.make_async_copy(kv_hbm.at[pg], buf.at[sl, p], sem.at[sl])

    @pl.when(bi == 0)
    def _():
        for p in range(n_pages):
            copy(0, 0, p).start()

    @pl.when(bi + 1 < pl.num_programs(0))
    def _():
        for p in range(n_pages):
            copy(bi + 1, 1 - slot, p).start()

    z = z_ref[0]
    tab = tab_ref[...]
    q = z[:, :BRANCH_W] * SCALE
    qr = _rope_heads(q, tab)
    k_new = _rope_heads(z[:, BRANCH_W:2 * BRANCH_W], tab)
    v_new = z[:, 2 * BRANCH_W:3 * BRANCH_W]
    row_ref[0] = jnp.concatenate([k_new, v_new], axis=1)

    for p in range(n_pages):
        copy(bi, slot, p).wait()

    heads = [slice(h * HEAD_DIM, (h + 1) * HEAD_DIM) for h in range(N_HEADS)]
    _page_scores(lambda p, h: buf[slot, p, h], [_column(qr[:, hs]) for hs in heads], s_s, n_pages)
    pi = _iota((n_pages, n_pages), 0)
    pj = _iota((n_pages, n_pages), 1)
    shift = pages_per_block.bit_length() - 1
    same_block = lax.shift_right_logical(pi, shift) == lax.shift_right_logical(pj, shift)
    p_new = []
    for h, hs in enumerate(heads):
        s = s_s[h]
        page_sum = jnp.sum(s, axis=1, keepdims=True)
        page_sum_row = jnp.sum(jnp.where(pi == pj, page_sum, 0.0), axis=0, keepdims=True)
        gate_col = jnp.sum(jnp.where(same_block, page_sum_row, 0.0), axis=1, keepdims=True)
        gate_row = jnp.sum(jnp.where(same_block, page_sum, 0.0), axis=0, keepdims=True)
        beats = ((gate_row > gate_col) | ((gate_row == gate_col) & (pj < pi))) & jnp.logical_not(same_block)
        picked = jnp.sum(beats.astype(F32), axis=1, keepdims=True) < pages_per_block * topk - 0.5
        s_new = jnp.sum(qr[:, hs] * k_new[:, hs], axis=1, keepdims=True)
        p_s[h], pn = _softmax_with_new(s, picked, s_new)
        p_new.append(pn)
    cols = _page_values(lambda p, h: buf[slot, p, N_HEADS + h], p_s, n_pages)
    outs = [_to_row(cols[h]) + p_new[h] * v_new[:, hs] for h, hs in enumerate(heads)]
    o_ref[0] = jnp.concatenate(outs, axis=1).astype(o_ref.dtype)


def moba_sample(page_table, z_moba, tab, cache, layer, n_pool, page):
    b = z_moba.shape[0]
    n_pages = page_table.shape[1]
    past = n_pages * page
    assert past % MOBA_BLOCK == 0 and MOBA_BLOCK % page == 0
    ppb = MOBA_BLOCK // page
    assert ppb & (ppb - 1) == 0
    kern = functools.partial(_moba_sample_kernel, base=layer * n_pool, n_pages=n_pages, pages_per_block=ppb,
                             topk=min(MOBA_TOPK, past // MOBA_BLOCK + 1))
    row = lambda i, pt: (i, 0, 0)
    grid_spec = pltpu.PrefetchScalarGridSpec(
        num_scalar_prefetch=1, grid=(b,),
        in_specs=[pl.BlockSpec((1, 1, z_moba.shape[2]), row),
                  pl.BlockSpec(tab.shape, lambda i, pt: (0, 0, 0)),
                  pl.BlockSpec(memory_space=pl.ANY)],
        out_specs=[pl.BlockSpec((1, 1, BRANCH_W), row), pl.BlockSpec((1, 1, 2 * BRANCH_W), row)],
        scratch_shapes=[pltpu.VMEM((2, n_pages, 2 * N_HEADS, HEAD_DIM, page), F32),
                        pltpu.VMEM((N_HEADS, n_pages, page), F32),
                        pltpu.VMEM((N_HEADS, n_pages, page), F32),
                        pltpu.SemaphoreType.DMA((2,))])
    return pl.pallas_call(
        kern, grid_spec=grid_spec,
        out_shape=[jax.ShapeDtypeStruct((b, 1, BRANCH_W), BF16),
                   jax.ShapeDtypeStruct((b, 1, 2 * BRANCH_W), F32)],
        compiler_params=_params("arbitrary"), name="moba_sample",
    )(page_table.reshape(-1), z_moba, tab, cache)


def _rope_table(pos):
    inv = ROPE_THETA ** (-jnp.arange(ROT_HALF, dtype=F32) / ROT_HALF)
    ang = pos.astype(F32)[:, None] * inv[None, :]
    cos, sin = jnp.cos(ang), jnp.sin(ang)
    t = pos.shape[0]
    ones = jnp.ones((t, HEAD_DIM - 2 * ROT_HALF), F32)
    zeros = jnp.zeros((t, HEAD_DIM - 2 * ROT_HALF), F32)
    z8 = jnp.zeros((t, ROT_HALF), F32)
    c = jnp.concatenate([cos, cos, ones], axis=1)
    sa = jnp.concatenate([-sin, z8, zeros], axis=1)
    sb = jnp.concatenate([z8, sin, zeros], axis=1)
    return jnp.stack([jnp.tile(c, (1, 2)), jnp.tile(sa, (1, 2)), jnp.tile(sb, (1, 2))], axis=0)


def _cover_matrix(n_rows, n_cmp, n_slc, width):
    starts = np.arange(n_rows) * CMP_STRIDE
    blk = np.arange(width) * SLC_BLOCK
    cover = (starts[:, None] < blk[None, :] + SLC_BLOCK) & (starts[:, None] + CMP_BLOCK > blk[None, :])
    cover &= (np.arange(n_rows)[:, None] < n_cmp) & (np.arange(width)[None, :] < n_slc)
    return jnp.asarray(cover, BF16)


def _expand_matrix(n_blocks, block, n_keys):
    e = np.arange(n_blocks)[:, None] == (np.arange(n_keys)[None, :] // block)
    return jnp.asarray(e, BF16)


def _round_up(n, m):
    return -(-n // m) * m


COL_NSA, COL_MOBA, COL_CROSS, COL_RWKV, COL_MERGE = 0, 768, 1536, 2304, 4096


def _pack_w_in(w_in):
    def zeros(n):
        return jnp.zeros(w_in.shape[:2] + (n,), w_in.dtype)

    parts = [w_in[..., 0:652], zeros(NSA_PAD - 652), w_in[..., 1708:2476], w_in[..., 2476:2732],
             zeros(COL_RWKV - COL_CROSS - BRANCH_W), w_in[..., 652:1708], zeros(RWKV_PAD - RWKV_IN),
             zeros(COL_MERGE - COL_RWKV - RWKV_PAD), w_in[..., 2732:]]
    return jnp.concatenate(parts, axis=-1).astype(BF16)


def _compress_weights(nsa_pe, nsa_phi_w1, nsa_phi_w2):
    depth, _, _, hidden = nsa_phi_w1.shape
    line = CMP_STRIDE * HEAD_DIM
    w1 = nsa_phi_w1.reshape(depth, 2, 2, line, hidden)
    w1 = jnp.transpose(w1, (0, 1, 3, 2, 4)).reshape(depth, 2, line, 2 * hidden).astype(BF16)
    pe = nsa_pe.reshape(depth, 2, CMP_STRIDE, 2, HEAD_DIM)
    pe = jnp.transpose(pe, (0, 3, 1, 2, 4)).reshape(depth, 2, 2, line)
    return w1, jnp.pad(pe, ((0, 0), (0, 0), (0, 6), (0, 0))), nsa_phi_w2.astype(BF16)


def _layer_weights(l, rwkv_mu, rwkv_w0, rwkv_w2, rwkv_a0, rwkv_a2,
                   rwkv_g2, rwkv_k_k, rwkv_k_a, rwkv_r_k, rwkv_ln_w, rwkv_ln_b):
    w = {}
    vec = jnp.zeros((8, RWKV_PAD), F32)
    vec = vec.at[0, :RWKV_IN].set(rwkv_mu[l])
    vec = vec.at[1, 0:256].set(rwkv_w0[l]).at[1, 256:512].set(rwkv_a0[l])
    for i, p in enumerate((rwkv_k_k, rwkv_k_a, rwkv_r_k, rwkv_ln_w, rwkv_ln_b)):
        vec = vec.at[2 + i, 0:256].set(p[l])
    w['rwkv_vec'] = vec
    wa = jnp.zeros((128, 512), F32).at[0:64, 0:256].set(rwkv_w2[l]).at[64:128, 256:512].set(rwkv_a2[l])
    w['rwkv_wa'] = wa.astype(BF16)
    w['rwkv_g2'] = jnp.pad(rwkv_g2[l], ((0, 256 - rwkv_g2.shape[1]), (0, 0))).astype(BF16)
    return w


def _project(h, w_all, l, d):
    z_nsa = mm(h, w_all, l, COL_NSA, NSA_PAD)
    z_rwkv = mm(h, w_all, l, COL_RWKV, RWKV_PAD)
    z_moba = mm(h, w_all, l, COL_MOBA, 3 * BRANCH_W)
    z_cross = mm(h, w_all, l, COL_CROSS, BRANCH_W)
    gates = mm(h, w_all, l, COL_MERGE, 4 * d, out_dtype=BF16, gate=True)
    return z_nsa, z_rwkv, z_moba, z_cross, gates


def _mixers_prompt(h, w_all, l, w, tab, mem_kv, consts, b, t):
    z_nsa, z_rwkv, z_moba, z_cross, gates = _project(h, w_all, l, h.shape[1])
    z_nsa = z_nsa.reshape(b, t, NSA_PAD)
    z_rwkv = z_rwkv.reshape(b, t, RWKV_PAD)
    z_moba = z_moba.reshape(b, t, 3 * BRANCH_W)
    z_cross = z_cross.reshape(b, t, BRANCH_W)

    new_cmp = z_nsa[:, :, 256:384]
    new_slc, new_win = nsa_rope(z_nsa, tab)
    lines = [new_cmp[:, :, j * HEAD_DIM:(j + 1) * HEAD_DIM].reshape(b, t // CMP_STRIDE, CMP_STRIDE * HEAD_DIM)
             for j in range(2)]
    cmp_kv = compress(lines[0], lines[1], *consts['cmp_w'], l)
    o_nsa = nsa_attn(z_nsa, tab, cmp_kv, new_slc, new_win, consts['cover_p'], consts['n_cmp_p'],
                     consts['n_slc_p'])

    zero_shift = jnp.zeros((b, 1, RWKV_PAD), F32)
    zero_state = jnp.zeros((b, N_HEADS, HEAD_DIM, HEAD_DIM), F32)
    o_rwkv, new_state = rwkv_seq(z_rwkv, zero_shift, zero_state, w['rwkv_vec'], w['rwkv_wa'], w['rwkv_g2'],
                                 consts['bd'], consts['tri'])
    new_shift = z_rwkv[:, t - 1, :RWKV_IN]

    new_moba, means = moba_rope(z_moba, tab)
    o_moba = moba_attn(z_moba, tab, new_moba, means.reshape(b, -1, BRANCH_W))
    o_cross = cross_attn(z_cross, mem_kv)

    outs = [o.reshape(b * t, BRANCH_W) for o in (o_nsa, o_rwkv, o_moba, o_cross)]
    win_keep = min(WINDOW, t)
    caches = (new_cmp.reshape(b, t, 2, HEAD_DIM), new_slc.reshape(b, t, 2, HEAD_DIM),
              new_win[:, t - win_keep:].reshape(b, win_keep, 2, HEAD_DIM),
              new_moba.reshape(b, t, 2, N_HEADS, HEAD_DIM), new_state, new_shift)
    return outs, gates, caches


def _mixers_sample(h, w_all, l, w, tab, page_table, caches_in, consts, b):
    cmp_pages, slc_pages, cache_win, moba_pages, cache_mem, state_rwkv, state_shift, n_pool, page = caches_in
    z_nsa, z_rwkv, z_moba, z_cross, gates = _project(h, w_all, l, h.shape[1])
    z_nsa = z_nsa.reshape(b, 1, NSA_PAD)
    z_rwkv = z_rwkv.reshape(b, 1, RWKV_PAD)
    z_moba = z_moba.reshape(b, 1, 3 * BRANCH_W)
    z_cross = z_cross.reshape(b, 1, BRANCH_W)

    win = cache_win[l].reshape(b, -1, LANES)
    o_nsa, slc_row, win_row = nsa_sample(
        page_table, z_nsa, tab, win, *consts['cmp_w'], consts['cover_s'], consts['pj_s'],
        consts['hr_s'], cmp_pages, slc_pages, l, n_pool, page, consts['n_cmp_s'], consts['n_slc_s'])
    new_cmp = z_nsa[:, :, 256:384]
    full_win = jnp.concatenate([win, win_row], axis=1)
    keep = min(WINDOW, full_win.shape[1])
    new_buf = full_win[:, full_win.shape[1] - keep:]

    shift = jnp.pad(state_shift[l], ((0, 0), (0, RWKV_PAD - RWKV_IN))).reshape(b, 1, RWKV_PAD)
    o_rwkv, new_state = rwkv_step(z_rwkv, shift, state_rwkv[l], w['rwkv_vec'], w['rwkv_wa'], w['rwkv_g2'],
                                  consts['bd'])
    new_shift = z_rwkv[:, 0, :RWKV_IN]

    o_moba, moba_row = moba_sample(page_table, z_moba, tab, moba_pages, l, n_pool, page)
    o_cross = cross_attn(z_cross, cache_mem[l].reshape(b, -1, 2 * BRANCH_W))

    outs = [o.reshape(b, BRANCH_W) for o in (o_nsa, o_rwkv, o_moba, o_cross)]
    caches = (new_cmp.reshape(b, 1, 2, HEAD_DIM), slc_row.reshape(b, 1, 2, HEAD_DIM),
              new_buf.reshape(b, keep, 2, HEAD_DIM), moba_row.reshape(b, 1, 2, N_HEADS, HEAD_DIM),
              new_state, new_shift)
    return outs, gates, caches


def kernel(x_prompt, x_sample, cache_nsa_cmp, cache_nsa_slc, cache_nsa_win, cache_moba, cache_mem, state_rwkv, state_shift, page_table, mem_prompt, g_pre_mix, g_post_mix, g_pre_ffn, g_post_ffn, g_mem, w_in, nsa_pe, nsa_phi_w1, nsa_phi_w2, rwkv_mu, rwkv_w0, rwkv_w2, rwkv_a0, rwkv_a2, rwkv_g2, rwkv_k_k, rwkv_k_a, rwkv_r_k, rwkv_ln_w, rwkv_ln_b, w_mem_kv, w_branch, w_out, w_ffn_gate, w_ffn_up, w_ffn_down):
    bp, t, d = x_prompt.shape
    bs = x_sample.shape[0]
    assert x_sample.shape[1] == 1
    depth = w_in.shape[0]
    page = cache_nsa_cmp.shape[2]
    past = page_table.shape[1] * page
    n_mem = mem_prompt.shape[1]
    assert t % MOBA_BLOCK == 0 and past % MOBA_BLOCK == 0 and page % SLC_BLOCK == 0

    n_cmp_p = (t - CMP_BLOCK) // CMP_STRIDE + 1
    n_slc_p = -(-t // SLC_BLOCK)
    n_cmp_s = (past + 1 - CMP_BLOCK) // CMP_STRIDE + 1
    n_slc_s = -(-(past + 1) // SLC_BLOCK)
    consts = {
        'cmp_w': _compress_weights(nsa_pe, nsa_phi_w1, nsa_phi_w2),
        'n_cmp_p': n_cmp_p, 'n_slc_p': n_slc_p, 'n_cmp_s': n_cmp_s, 'n_slc_s': n_slc_s,
        'cover_p': _cover_matrix(t // CMP_STRIDE, n_cmp_p, n_slc_p, _round_up(n_slc_p, 8)).T,
        'cover_s': _cover_matrix(past // CMP_STRIDE, n_cmp_s, n_slc_s, _round_up(n_slc_s, LANES)),
        'pj_s': jnp.asarray(np.arange(_round_up(n_slc_s, LANES))[None, :] // (page // SLC_BLOCK)
                            == np.arange(past // page)[:, None], F32),
        'hr_s': jnp.asarray(np.arange(_round_up(n_slc_s, LANES))[:, None] % (page // SLC_BLOCK)
                            == np.arange(page)[None, :] // SLC_BLOCK, BF16),
        'bd': jnp.asarray(np.arange(BRANCH_W)[:, None] // HEAD_DIM == np.arange(BRANCH_W)[None, :] // HEAD_DIM, BF16),
        'tri': jnp.asarray(np.tril(np.ones((RWKV_CHUNK, RWKV_CHUNK), np.float32))),
    }
    tab_p = _rope_table(jnp.arange(t, dtype=jnp.int32))
    tab_s = _rope_table(jnp.full((1,), past, jnp.int32))

    xp = x_prompt.reshape(bp * t, d)
    xs = x_sample.reshape(bs, d)
    mem_flat = mem_prompt.reshape(bp * n_mem, d)
    n_pool = cache_nsa_cmp.shape[1]
    cmp_pages = jnp.transpose(cache_nsa_cmp, (0, 1, 3, 4, 2)).reshape(depth * n_pool, 2 * HEAD_DIM, page)
    slc_pages = jnp.transpose(cache_nsa_slc, (0, 1, 3, 4, 2)).reshape(depth * n_pool, 2, HEAD_DIM, page)
    moba_pages = jnp.transpose(cache_moba, (0, 1, 3, 4, 5, 2)).reshape(depth * n_pool, 2 * N_HEADS, HEAD_DIM, page)
    sample_caches = (cmp_pages, slc_pages, cache_nsa_win, moba_pages, cache_mem, state_rwkv, state_shift,
                     n_pool, page)
    w_all = _pack_w_in(w_in)
    w_mem, w_br, w_o = w_mem_kv.astype(BF16), w_branch.astype(BF16), w_out.astype(BF16)
    w_fg, w_fu, w_fd = w_ffn_gate.astype(BF16), w_ffn_up.astype(BF16), w_ffn_down.astype(BF16)
    outs_p, outs_s, mem_out = [], [], []
    for l in range(depth):
        w = _layer_weights(l, rwkv_mu, rwkv_w0, rwkv_w2, rwkv_a0,
                           rwkv_a2, rwkv_g2, rwkv_k_k, rwkv_k_a, rwkv_r_k, rwkv_ln_w, rwkv_ln_b)
        mem_kv = mm(rms_cast(mem_flat, g_mem[l]), w_mem, l, 0, 2 * BRANCH_W).reshape(bp, n_mem, 2 * BRANCH_W)
        mem_out.append(mem_kv.reshape(bp, n_mem, 2, N_HEADS, HEAD_DIM))

        o_p, gates_p, new_p = _mixers_prompt(rms_cast(xp, g_pre_mix[l]), w_all, l, w, tab_p, mem_kv, consts, bp, t)
        xp = merge(xp, o_p, gates_p, w_br, w_o, g_post_mix[l], l)
        xp = ffn(xp, g_pre_ffn[l], w_fg, w_fu, w_fd, g_post_ffn[l], l)

        o_s, gates_s, new_s = _mixers_sample(rms_cast(xs, g_pre_mix[l]), w_all, l, w, tab_s, page_table,
                                             sample_caches, consts, bs)
        xs = merge(xs, o_s, gates_s, w_br, w_o, g_post_mix[l], l)
        xs = ffn(xs, g_pre_ffn[l], w_fg, w_fu, w_fd, g_post_ffn[l], l)
        outs_p.append(new_p)
        outs_s.append(new_s)

    def stack(outs, i):
        return jnp.stack([o[i] for o in outs], axis=0)

    return (xp.reshape(bp, t, d), xs.reshape(bs, 1, d),
            stack(outs_p, 0), stack(outs_s, 0), stack(outs_p, 1), stack(outs_s, 1),
            stack(outs_p, 2), stack(outs_s, 2), stack(outs_p, 3), stack(outs_s, 3),
            jnp.stack(mem_out, axis=0),
            stack(outs_p, 4), stack(outs_s, 4), stack(outs_p, 5), stack(outs_s, 5))
```

```python
import functools

import jax
import jax.numpy as jnp
import numpy as np
from jax import lax
from jax.experimental import pallas as pl
from jax.experimental.pallas import tpu as pltpu

F32 = jnp.float32
BF16 = jnp.bfloat16

HEAD_DIM = 64
N_HEADS = 4
BRANCH_W = N_HEADS * HEAD_DIM
ROT_HALF = 8
ROPE_THETA = 500000.0
CMP_BLOCK = 32
CMP_STRIDE = 16
SLC_BLOCK = 64
N_SELECT = 16
WINDOW = 512
MOBA_BLOCK = 256
MOBA_TOPK = 3
RWKV_IN = 1056
RWKV_PAD = 1152
NSA_PAD = 768
RWKV_GN_EPS = 64e-5
RWKV_CHUNK = 64
NEG = -1e30
SCALE = HEAD_DIM ** -0.5
LANES = 128
ATTN_KEY_TILE = 256
FFN_TILE = 1408
VMEM_LIMIT = 48 * 1024 * 1024


def _iota(shape, dim):
    return lax.broadcasted_iota(jnp.int32, shape, dim)


def _dot(a, b):
    return lax.dot_general(a, b, (((1,), (0,)), ((), ())), preferred_element_type=F32)


def _dot_nt(a, b):
    return lax.dot_general(a, b, (((1,), (1,)), ((), ())), preferred_element_type=F32)


def _dot_tn(a, b):
    return lax.dot_general(a, b, (((0,), (0,)), ((), ())), preferred_element_type=F32)


def _split(x):
    hi = x.astype(BF16)
    lo = (x - hi.astype(F32)).astype(BF16)
    return hi, lo


def _dot3(a, b, dot=_dot):
    ah, al = _split(a)
    bh, bl = _split(b)
    return dot(ah, bh) + dot(al, bh) + dot(ah, bl)


def _dot2_exact_rhs(a, b_bf16):
    ah, al = _split(a)
    return _dot(ah, b_bf16) + _dot(al, b_bf16)


def _sigmoid(x):
    return 1.0 / (1.0 + jnp.exp(-x))


def _rms(x, g, eps=1e-6):
    return x * lax.rsqrt(jnp.mean(x * x, axis=-1, keepdims=True) + eps) * g


def _msoftmax(s, mask):
    s = jnp.where(mask, s, NEG)
    p = jnp.where(mask, jnp.exp(s - jnp.max(s, axis=-1, keepdims=True)), 0.0)
    return p / jnp.maximum(jnp.sum(p, axis=-1, keepdims=True), 1e-30)


def _flash_step(q, k, v, mask, m, l, acc):
    s = jnp.where(mask, _dot_nt(q, k), NEG)
    m_new = jnp.maximum(m, jnp.max(s, axis=-1, keepdims=True))
    p = jnp.where(mask, jnp.exp(s - m_new), 0.0)
    alpha = jnp.exp(m - m_new)
    l = alpha * l + jnp.sum(p, axis=-1, keepdims=True)
    acc = alpha * acc + _dot(p.astype(BF16), v)
    return m_new, l, acc


def _with_ones(v_t):
    return jnp.concatenate([v_t, jnp.ones((8, v_t.shape[1]), v_t.dtype)], axis=0)


def _flash_heads_t(q_ts, ks, v_ts, biases, valid, carry):
    ss = []
    for q_t, k, bias in zip(q_ts, ks, biases):
        s = _dot(k, q_t)
        s = s if bias is None else s + bias
        ss.append(s if valid is None else jnp.where(valid, s, NEG))
    m_news = [jnp.maximum(c[0], jnp.max(s, axis=0, keepdims=True)) for c, s in zip(carry, ss)]
    pvs = [_dot(v_t, jnp.exp(s - m_new).astype(BF16)) for v_t, s, m_new in zip(v_ts, ss, m_news)]
    return tuple((m_new, jnp.exp(c[0] - m_new) * c[1] + pv) for c, m_new, pv in zip(carry, m_news, pvs))


def _flash_finish(carry):
    acc = carry[1]
    d = acc.shape[0] - 8
    return acc[:d] / jnp.maximum(acc[d:d + 1], 1e-30)


def _rope(x, tab, half_only=False):
    c, sa, sb = tab[0], tab[1], tab[2]
    if half_only:
        first = _iota(x.shape, 1) < HEAD_DIM
        c = jnp.where(first, c, 1.0)
        sa = jnp.where(first, sa, 0.0)
        sb = jnp.where(first, sb, 0.0)
    return x * c + pltpu.roll(x, LANES - ROT_HALF, 1) * sa + pltpu.roll(x, ROT_HALF, 1) * sb


def _rope_heads(x, tab):
    return jnp.concatenate([_rope(x[:, :LANES], tab), _rope(x[:, LANES:], tab)], axis=1)


def _rank_lt(score, n_real, k):
    lane = _iota(score.shape, 1)
    cnt = jnp.zeros(score.shape, jnp.int32)
    for i in range(n_real):
        ci = score[:, i:i + 1]
        beats = (ci > score) | ((ci == score) & (lane > i))
        cnt = cnt + beats.astype(jnp.int32)
    return cnt < k


def _pick_tile(n, target, mult=8):
    if n <= target:
        return n
    for t in range(target, 0, -1):
        if n % t == 0 and t % mult == 0:
            return t
    return n


def _params(*sem):
    return pltpu.CompilerParams(dimension_semantics=sem, vmem_limit_bytes=VMEM_LIMIT)


def _rms_cast_kernel(x_ref, g_ref, o_ref):
    o_ref[...] = _rms(x_ref[...], g_ref[...]).astype(BF16)


def rms_cast(x, g):
    m, d = x.shape
    tm = _pick_tile(m, 2048, 16)
    return pl.pallas_call(
        _rms_cast_kernel, grid=(m // tm,),
        in_specs=[pl.BlockSpec((tm, d), lambda i: (i, 0)), pl.BlockSpec((1, d), lambda i: (0, 0))],
        out_specs=pl.BlockSpec((tm, d), lambda i: (i, 0)),
        out_shape=jax.ShapeDtypeStruct((m, d), BF16),
        compiler_params=_params("parallel"), name="rms_cast",
    )(x, g.reshape(1, d))


def _mm_kernel(a_ref, b_ref, o_ref, *, gate):
    y = _dot(a_ref[...], b_ref[...])
    o_ref[...] = (_sigmoid(y) if gate else y).astype(o_ref.dtype)


def mm(a, w, layer, col0, n, out_dtype=F32, gate=False):
    m, k = a.shape
    tm = _pick_tile(m, 1024, 16)
    tn = n if n <= 1536 else _pick_tile(n, 1024, LANES)
    assert col0 % tn == 0 and n % tn == 0
    c0 = col0 // tn
    return pl.pallas_call(
        functools.partial(_mm_kernel, gate=gate), grid=(m // tm, n // tn),
        in_specs=[pl.BlockSpec((tm, k), lambda i, j: (i, 0)),
                  pl.BlockSpec((None, k, tn), lambda i, j: (layer, 0, c0 + j))],
        out_specs=pl.BlockSpec((tm, tn), lambda i, j: (i, j)),
        out_shape=jax.ShapeDtypeStruct((m, n), out_dtype),
        compiler_params=_params("parallel", "parallel"), name="mm",
    )(a, w)


def _ffn_kernel(x_ref, g1_ref, wg_ref, wu_ref, wd_ref, g2_ref, o_ref, h_ref, acc_ref):
    f = pl.program_id(1)

    @pl.when(f == 0)
    def _():
        h_ref[...] = _rms(x_ref[...], g1_ref[...]).astype(BF16)
        acc_ref[...] = jnp.zeros_like(acc_ref)

    h = h_ref[...]
    gate = _dot(h, wg_ref[...])
    up = _dot(h, wu_ref[...])
    act = (gate * _sigmoid(gate) * up).astype(BF16)
    acc_ref[...] += _dot(act, wd_ref[...])

    @pl.when(f == pl.num_programs(1) - 1)
    def _():
        o_ref[...] = x_ref[...] + _rms(acc_ref[...], g2_ref[...])


def ffn(x, g_pre, wg, wu, wd, g_post, layer):
    m, d = x.shape
    dff = wg.shape[2]
    tm = _pick_tile(m, 512, 16)
    tf = _pick_tile(dff, FFN_TILE, LANES)
    return pl.pallas_call(
        _ffn_kernel, grid=(m // tm, dff // tf),
        in_specs=[pl.BlockSpec((tm, d), lambda i, f: (i, 0)),
                  pl.BlockSpec((1, d), lambda i, f: (0, 0)),
                  pl.BlockSpec((None, d, tf), lambda i, f: (layer, 0, f)),
                  pl.BlockSpec((None, d, tf), lambda i, f: (layer, 0, f)),
                  pl.BlockSpec((None, tf, d), lambda i, f: (layer, f, 0)),
                  pl.BlockSpec((1, d), lambda i, f: (0, 0))],
        out_specs=pl.BlockSpec((tm, d), lambda i, f: (i, 0)),
        out_shape=jax.ShapeDtypeStruct((m, d), F32),
        scratch_shapes=[pltpu.VMEM((tm, d), BF16), pltpu.VMEM((tm, d), F32)],
        compiler_params=_params("parallel", "arbitrary"), name="ffn",
    )(x, g_pre.reshape(1, d), wg, wu, wd, g_post.reshape(1, d))


def _merge_kernel(x_ref, o0_ref, o1_ref, o2_ref, o3_ref, gate_ref, wb_ref, wo_ref, g_ref, out_ref):
    d = x_ref.shape[1]
    merged = None
    for b, o_ref in enumerate((o0_ref, o1_ref, o2_ref, o3_ref)):
        term = gate_ref[:, b * d:(b + 1) * d].astype(F32) * _dot(o_ref[...], wb_ref[b])
        merged = term if merged is None else merged + term
    y = _dot(merged.astype(BF16), wo_ref[...])
    out_ref[...] = x_ref[...] + _rms(y, g_ref[...])


def merge(x, outs, gates, w_branch, w_out, g_post, layer):
    m, d = x.shape
    bw = outs[0].shape[1]
    tm = _pick_tile(m, 512, 16)
    row = lambda i: (i, 0)
    return pl.pallas_call(
        _merge_kernel, grid=(m // tm,),
        in_specs=[pl.BlockSpec((tm, d), row)] + [pl.BlockSpec((tm, bw), row)] * 4 + [
            pl.BlockSpec((tm, 4 * d), row),
            pl.BlockSpec((None, 4, bw, d), lambda i: (layer, 0, 0, 0)),
            pl.BlockSpec((None, d, d), lambda i: (layer, 0, 0)),
            pl.BlockSpec((1, d), lambda i: (0, 0))],
        out_specs=pl.BlockSpec((tm, d), row),
        out_shape=jax.ShapeDtypeStruct((m, d), F32),
        compiler_params=_params("parallel"), name="merge",
    )(x, *outs, gates, w_branch, w_out, g_post.reshape(1, d))


def _nsa_rope_kernel(z_ref, tab_ref, slc_ref, win_ref):
    z = z_ref[0]
    tab = tab_ref[...]
    slc_ref[0] = _rope(z[:, 384:512], tab, half_only=True)
    win_ref[0] = _rope(z[:, 512:640], tab, half_only=True)


def nsa_rope(z_nsa, tab):
    b, t, w = z_nsa.shape
    tt = _pick_tile(t, 2048)
    out = jax.ShapeDtypeStruct((b, t, LANES), F32)
    return pl.pallas_call(
        _nsa_rope_kernel, grid=(b, t // tt),
        in_specs=[pl.BlockSpec((1, tt, w), lambda i, j: (i, j, 0)),
                  pl.BlockSpec((3, tt, LANES), lambda i, j: (0, j, 0))],
        out_specs=[pl.BlockSpec((1, tt, LANES), lambda i, j: (i, j, 0))] * 2,
        out_shape=[out, out],
        compiler_params=_params("parallel", "parallel"), name="nsa_rope",
    )(z_nsa, tab)


def _moba_rope_kernel(z_ref, tab_ref, kv_ref, mean_ref):
    z = z_ref[0]
    kr = _rope_heads(z[:, BRANCH_W:2 * BRANCH_W], tab_ref[...])
    kv_ref[0] = jnp.concatenate([kr, z[:, 2 * BRANCH_W:3 * BRANCH_W]], axis=1)
    for u in range(mean_ref.shape[1]):
        mean_ref[0, u] = jnp.sum(kr[u * MOBA_BLOCK:(u + 1) * MOBA_BLOCK], axis=0, keepdims=True) * (1.0 / MOBA_BLOCK)


def moba_rope(z_moba, tab):
    b, t, w = z_moba.shape
    nb = t // MOBA_BLOCK
    per_step = 4 if nb % 4 == 0 else 1
    rows = per_step * MOBA_BLOCK
    return pl.pallas_call(
        _moba_rope_kernel, grid=(b, nb // per_step),
        in_specs=[pl.BlockSpec((1, rows, w), lambda i, j: (i, j, 0)),
                  pl.BlockSpec((3, rows, LANES), lambda i, j: (0, j, 0))],
        out_specs=[pl.BlockSpec((1, rows, 2 * BRANCH_W), lambda i, j: (i, j, 0)),
                   pl.BlockSpec((1, per_step, 1, BRANCH_W), lambda i, j: (i, j, 0, 0))],
        out_shape=[jax.ShapeDtypeStruct((b, t, 2 * BRANCH_W), F32),
                   jax.ShapeDtypeStruct((b, nb, 1, BRANCH_W), F32)],
        compiler_params=_params("parallel", "parallel"), name="moba_rope",
    )(z_moba, tab)


def _compress(lines, w1_ref, pe_ref, w2_ref):
    outs = []
    for j, g in enumerate(lines):
        r = g.shape[0]
        ab = _dot(g.astype(BF16), w1_ref[j])
        pe = _dot(pe_ref[j].astype(BF16), w1_ref[j])
        half = ab.shape[1] // 2
        nxt = pltpu.roll(ab[:, half:], r - 1, 0)
        pre = ab[:, :half] + nxt + pe[0:1, :half] + pe[1:2, half:]
        hid = pre * _sigmoid(pre)
        outs.append(_dot(hid.astype(BF16), w2_ref[j]))
    return jnp.concatenate(outs, axis=1)


def _compress_kernel(gk_ref, gv_ref, w1_ref, pe_ref, w2_ref, o_ref):
    o_ref[0] = _compress((gk_ref[0], gv_ref[0]), w1_ref, pe_ref, w2_ref)


def compress(gk, gv, w1, pe2, w2, layer):
    b, r, w = gk.shape
    lay = lambda i: (layer, 0, 0, 0)
    return pl.pallas_call(
        _compress_kernel, grid=(b,),
        in_specs=[pl.BlockSpec((1, r, w), lambda i: (i, 0, 0)), pl.BlockSpec((1, r, w), lambda i: (i, 0, 0)),
                  pl.BlockSpec((None,) + w1.shape[1:], lay), pl.BlockSpec((None,) + pe2.shape[1:], lay),
                  pl.BlockSpec((None,) + w2.shape[1:], lay)],
        out_specs=pl.BlockSpec((1, r, LANES), lambda i: (i, 0, 0)),
        out_shape=jax.ShapeDtypeStruct((b, r, LANES), F32),
        compiler_params=_params("parallel"), name="nsa_compress",
    )(gk, gv, w1, pe2, w2)


def _rank_lt_rows(score, n_real, k):
    row = _iota(score.shape, 0)
    cnt = jnp.zeros(score.shape, jnp.int32)
    for i in range(n_real):
        ri = score[i:i + 1, :]
        beats = (ri > score) | ((ri == score) & (row > i))
        cnt = cnt + beats.astype(jnp.int32)
    return cnt < k


def _select_blocks(imp, t_pos, n_slc, k_sel):
    j = _iota(imp.shape, 1)
    own = lax.shift_right_logical(t_pos, 6)
    causal = (j * SLC_BLOCK <= t_pos) & (j < n_slc)
    forced = (j == 0) | (j == own) | (j == own - 1)
    score = jnp.where(forced, -NEG, jnp.where(causal, imp, NEG))
    return _rank_lt(score, n_slc, k_sel) & causal


def _nsa_attn_kernel(z_ref, tab_ref, cmp_ref, slc_ref, win_ref, cover_ref, o_ref, sel_s,
                     *, tq, tk, n_cmp, n_slc, k_sel):
    qi = pl.program_id(1)
    z = z_ref[0]
    q = z[:, :BRANCH_W] * SCALE
    qr = _rope_heads(q, tab_ref[...])
    heads = [slice(h * HEAD_DIM, (h + 1) * HEAD_DIM) for h in range(N_HEADS)]

    cmpkv = cmp_ref[0]
    ck = cmpkv[:, :HEAD_DIM].astype(BF16)
    cv = cmpkv[:, HEAD_DIM:].astype(BF16)
    t_row = qi * tq + _iota((1, tq), 1)
    n = _iota((cmpkv.shape[0], 1), 0)
    cmask = (n * CMP_STRIDE + (CMP_BLOCK - 1) <= t_row) & (n < n_cmp)
    o_cmp, psum = [], None
    for hs in heads:
        s = jnp.where(cmask, _dot_nt(ck, q[:, hs].astype(BF16)), NEG)
        p = jnp.where(cmask, jnp.exp(s - jnp.max(s, axis=0, keepdims=True)), 0.0)
        p = p / jnp.maximum(jnp.sum(p, axis=0, keepdims=True), 1e-30)
        o_cmp.append(_dot_tn(cv, p.astype(BF16)))
        psum = p if psum is None else psum + p
    p_hi, p_lo = _split(psum)
    imp = _dot(cover_ref[...], p_hi) + _dot(cover_ref[...], p_lo)
    j = _iota(imp.shape, 0)
    own = lax.shift_right_logical(t_row, 6)
    causal = (j * SLC_BLOCK <= t_row) & (j < n_slc)
    forced = (j == 0) | (j == own) | (j == own - 1)
    score = jnp.where(forced, -NEG, jnp.where(causal, imp, NEG))
    picked = _rank_lt_rows(score, n_slc, k_sel) & causal
    sel_s[...] = jnp.where(picked, 0.0, NEG)

    qr_t = qr.T.astype(BF16)
    qrb = [qr_t[hs, :] for hs in heads]
    init = tuple((jnp.full((1, tq), NEG, F32), jnp.zeros((HEAD_DIM + 8, tq), F32)) for _ in heads)
    blocks_per_tile = tk // SLC_BLOCK
    tiles_per_q = tq // tk

    def tile(ref, kt):
        rows = ref[0, pl.ds(pl.multiple_of(kt * tk, tk), tk), :]
        return [rows[:, :HEAD_DIM].astype(BF16)] * N_HEADS, [_with_ones(rows.T[HEAD_DIM:, :].astype(BF16))] * N_HEADS

    def slc_body(kt, carry, diagonal):
        ks, v_ts = tile(slc_ref, kt)
        bias = jnp.concatenate(
            [jnp.broadcast_to(sel_s[pl.ds(kt * blocks_per_tile + u, 1), :], (SLC_BLOCK, tq))
             for u in range(blocks_per_tile)], axis=0)
        valid = (kt * tk + _iota((tk, 1), 0) <= t_row) if diagonal else None
        return _flash_heads_t(qrb, ks, v_ts, [bias] * N_HEADS, valid, carry)

    def win_body(kt, carry, banded):
        ks, v_ts = tile(win_ref, kt)
        d = t_row - (kt * tk + _iota((tk, 1), 0))
        valid = ((d >= 0) & (d <= WINDOW)) if banded else None
        return _flash_heads_t(qrb, ks, v_ts, [None] * N_HEADS, valid, carry)

    first_diag = qi * tiles_per_q
    hi = first_diag + tiles_per_q
    slc = lax.fori_loop(0, first_diag, functools.partial(slc_body, diagonal=False), init)
    slc = lax.fori_loop(first_diag, hi, functools.partial(slc_body, diagonal=True), slc)
    win_lo = jnp.maximum(first_diag - WINDOW // tk, 0)
    win_free = jnp.maximum(first_diag - (WINDOW - tq) // tk, 0)
    win = lax.fori_loop(win_lo, win_free, functools.partial(win_body, banded=True), init)
    win = lax.fori_loop(win_free, first_diag, functools.partial(win_body, banded=False), win)
    win = lax.fori_loop(first_diag, hi, functools.partial(win_body, banded=True), win)

    g = _sigmoid(z[:, 640:640 + LANES]).T
    outs = []
    for h in range(N_HEADS):
        outs.append(g[3 * h:3 * h + 1, :] * o_cmp[h] + g[3 * h + 1:3 * h + 2, :] * _flash_finish(slc[h])
                    + g[3 * h + 2:3 * h + 3, :] * _flash_finish(win[h]))
    o_ref[0] = jnp.concatenate(outs, axis=0).T.astype(o_ref.dtype)


def nsa_attn(z_nsa, tab, cmp_kv, new_slc, new_win, cover, n_cmp, n_slc):
    b, t, w = z_nsa.shape
    tq = _pick_tile(t, 256)
    tk = ATTN_KEY_TILE
    assert tq % tk == 0 and tk % SLC_BLOCK == 0 and WINDOW % tk == 0 and WINDOW >= tq
    kern = functools.partial(_nsa_attn_kernel, tq=tq, tk=tk, n_cmp=n_cmp, n_slc=n_slc,
                             k_sel=min(N_SELECT, n_slc))
    full = lambda i, j: (i, 0, 0)
    const = lambda i, j: (0, 0)
    return pl.pallas_call(
        kern, grid=(b, t // tq),
        in_specs=[pl.BlockSpec((1, tq, w), lambda i, j: (i, j, 0)),
                  pl.BlockSpec((3, tq, LANES), lambda i, j: (0, j, 0)),
                  pl.BlockSpec((1,) + cmp_kv.shape[1:], full),
                  pl.BlockSpec((1, t, LANES), full), pl.BlockSpec((1, t, LANES), full),
                  pl.BlockSpec(cover.shape, const)],
        out_specs=pl.BlockSpec((1, tq, BRANCH_W), lambda i, j: (i, j, 0)),
        out_shape=jax.ShapeDtypeStruct((b, t, BRANCH_W), BF16),
        scratch_shapes=[pltpu.VMEM((cover.shape[0], tq), F32)],
        compiler_params=_params("parallel", "parallel"), name="nsa_attn",
    )(z_nsa, tab, cmp_kv, new_slc, new_win, cover)


def _moba_attn_kernel(z_ref, tab_ref, kv_ref, mean_ref, o_ref, pick_s, *, tq, tk, nb, topk):
    qi = pl.program_id(1)
    q = z_ref[0][:, :BRANCH_W] * SCALE
    qr = _rope_heads(q, tab_ref[...])
    means = mean_ref[0]
    heads = [slice(h * HEAD_DIM, (h + 1) * HEAD_DIM) for h in range(N_HEADS)]

    t_row = qi * tq + _iota((1, tq), 1)
    own = lax.shift_right_logical(t_row, 8)
    past = _iota((means.shape[0], 1), 0) < own
    for h, hs in enumerate(heads):
        gate = _dot3(means[:, hs], qr[:, hs], _dot_nt)
        picked = _rank_lt_rows(jnp.where(past, gate, NEG), nb, topk) & past
        pick_s[h] = jnp.where(picked, 0.0, NEG)

    qr_t = qr.T.astype(BF16)
    qrb = [qr_t[hs, :] for hs in heads]
    init = tuple((jnp.full((1, tq), NEG, F32), jnp.zeros((HEAD_DIM + 8, tq), F32)) for _ in heads)
    tiles_per_block = MOBA_BLOCK // tk

    def body(kt, carry, own_block):
        rows = kv_ref[0, pl.ds(pl.multiple_of(kt * tk, tk), tk), :]
        v_t = rows[:, BRANCH_W:].T.astype(BF16)
        ks = [rows[:, hs].astype(BF16) for hs in heads]
        v_ts = [_with_ones(v_t[hs, :]) for hs in heads]
        if own_block:
            return _flash_heads_t(qrb, ks, v_ts, [None] * N_HEADS, kt * tk + _iota((tk, 1), 0) <= t_row, carry)
        blk = kt // tiles_per_block
        return _flash_heads_t(qrb, ks, v_ts, [pick_s[h, pl.ds(blk, 1), :] for h in range(N_HEADS)], None, carry)

    first_own = qi * (tq // tk)
    res = lax.fori_loop(0, first_own, functools.partial(body, own_block=False), init)
    res = lax.fori_loop(first_own, first_own + tq // tk, functools.partial(body, own_block=True), res)
    o_ref[0] = jnp.concatenate([_flash_finish(r) for r in res], axis=0).T.astype(o_ref.dtype)


def moba_attn(z_moba, tab, new_moba, means):
    b, t, w = z_moba.shape
    assert t % MOBA_BLOCK == 0
    tq = MOBA_BLOCK
    tk = ATTN_KEY_TILE
    assert MOBA_BLOCK % tk == 0
    nb = t // MOBA_BLOCK
    kern = functools.partial(_moba_attn_kernel, tq=tq, tk=tk, nb=nb, topk=min(MOBA_TOPK, nb))
    full = lambda i, j: (i, 0, 0)
    return pl.pallas_call(
        kern, grid=(b, t // tq),
        in_specs=[pl.BlockSpec((1, tq, w), lambda i, j: (i, j, 0)),
                  pl.BlockSpec((3, tq, LANES), lambda i, j: (0, j, 0)),
                  pl.BlockSpec((1, t, 2 * BRANCH_W), full),
                  pl.BlockSpec((1, nb, BRANCH_W), full)],
        out_specs=pl.BlockSpec((1, tq, BRANCH_W), lambda i, j: (i, j, 0)),
        out_shape=jax.ShapeDtypeStruct((b, t, BRANCH_W), BF16),
        scratch_shapes=[pltpu.VMEM((N_HEADS, nb, tq), F32)],
        compiler_params=_params("parallel", "parallel"), name="moba_attn",
    )(z_moba, tab, new_moba, means)


def _cross_attn_kernel(q_ref, mem_ref, o_ref):
    q = q_ref[0] * SCALE
    mem = mem_ref[0]
    outs = []
    for h in range(N_HEADS):
        hs = slice(h * HEAD_DIM, (h + 1) * HEAD_DIM)
        s = _dot_nt(q[:, hs].astype(BF16), mem[:, hs].astype(BF16))
        p = jnp.exp(s - jnp.max(s, axis=-1, keepdims=True))
        p = p / jnp.sum(p, axis=-1, keepdims=True)
        v = mem[:, BRANCH_W + h * HEAD_DIM:BRANCH_W + (h + 1) * HEAD_DIM].astype(BF16)
        outs.append(_dot(p.astype(BF16), v))
    o_ref[0] = jnp.concatenate(outs, axis=1).astype(o_ref.dtype)


def _cross_attn_t_kernel(q_ref, mem_ref, o_ref):
    q_t = (q_ref[0] * SCALE).T.astype(BF16)
    mem = mem_ref[0]
    v_t = mem[:, BRANCH_W:].T.astype(BF16)
    heads = [slice(h * HEAD_DIM, (h + 1) * HEAD_DIM) for h in range(N_HEADS)]
    ss = [_dot(mem[:, hs].astype(BF16), q_t[hs, :]) for hs in heads]
    ps = [jnp.exp(s - jnp.max(s, axis=0, keepdims=True)).astype(BF16) for s in ss]
    accs = [_dot(_with_ones(v_t[hs, :]), p) for hs, p in zip(heads, ps)]
    outs = [a[:HEAD_DIM] / a[HEAD_DIM:HEAD_DIM + 1] for a in accs]
    o_ref[0] = jnp.concatenate(outs, axis=0).T.astype(o_ref.dtype)


def cross_attn(q, mem_kv):
    b, t, w = q.shape
    n_mem = mem_kv.shape[1]
    lanes_ok = t % (2 * LANES) == 0 and n_mem % LANES == 0
    tq = 2 * LANES if lanes_ok else _pick_tile(t, 512)
    return pl.pallas_call(
        _cross_attn_t_kernel if lanes_ok else _cross_attn_kernel, grid=(b, t // tq),
        in_specs=[pl.BlockSpec((1, tq, w), lambda i, j: (i, j, 0)),
                  pl.BlockSpec((1, n_mem, 2 * BRANCH_W), lambda i, j: (i, 0, 0))],
        out_specs=pl.BlockSpec((1, tq, BRANCH_W), lambda i, j: (i, j, 0)),
        out_shape=jax.ShapeDtypeStruct((b, t, BRANCH_W), BF16),
        compiler_params=_params("parallel", "parallel"), name="cross_attn",
    )(q, mem_kv)


def _head_sum(x, bd_ref):
    return _dot(x.astype(BF16), bd_ref[...])


def _rwkv_prep(z, zprev, vec_ref, wa_ref, g2_ref, bd_ref):
    mu = vec_ref[0:1, :]
    zm = z + (zprev - z) * mu
    r = zm[:, 0:256]
    k = zm[:, 256:512]
    v = zm[:, 512:768]
    lora = zm[:, 768:896]
    lora = jnp.where(_iota(lora.shape, 1) < 64, jnp.tanh(lora), lora)
    wa = _dot(lora.astype(BF16), wa_ref[...])
    wl = vec_ref[1:2, 0:256] + wa[:, :256]
    x = -wl
    softplus = jnp.maximum(x, 0.0) + jnp.log(1.0 + jnp.exp(-jnp.abs(x)))
    logdecay = -jnp.exp(-softplus - 0.5)
    a = _sigmoid(vec_ref[1:2, 256:512] + wa[:, 256:])
    g = _dot(_sigmoid(zm[:, 896:1152]).astype(BF16), g2_ref[...])
    kkr = k * vec_ref[2:3, 0:256]
    kk = kkr * lax.rsqrt(jnp.maximum(_head_sum(kkr * kkr, bd_ref), 1e-24))
    k2 = k * (1.0 + (a - 1.0) * vec_ref[3:4, 0:256])
    bonus = _head_sum(r * k2 * vec_ref[4:5, 0:256], bd_ref) * v
    return r, k2, v, kk, kk * a, logdecay, g, bonus


def _rwkv_finish(y, bonus, g, vec_ref, bd_ref):
    mu = _head_sum(y, bd_ref) * (1.0 / HEAD_DIM)
    d = y - mu
    var = _head_sum(d * d, bd_ref) * (1.0 / HEAD_DIM)
    yn = d * lax.rsqrt(var + RWKV_GN_EPS) * vec_ref[5:6, 0:256] + vec_ref[6:7, 0:256]
    return (yn + bonus) * g


def _dotc(a, b, dot=_dot):
    return dot(a.astype(BF16), b.astype(BF16))


def _tri_inv_all(lows):
    c = lows[0].shape[0]
    eye = (_iota((c, c), 0) == _iota((c, c), 1)).astype(F32)
    ts = [eye - low for low in lows]
    lps = [_dotc(low, low) for low in lows]
    n = 2
    while n < c:
        ts = [t + _dotc(t, lp) for t, lp in zip(ts, lps)]
        n *= 2
        if n < c:
            lps = [_dotc(lp, lp) for lp in lps]
    return ts


RWKV_GROUP = 8


def _rwkv_seq_kernel(z_ref, shift_ref, s0_ref, vec_ref, wa_ref, g2_ref, bd_ref, tri_ref,
                     o_ref, s_ref, last_ref, r_s, k_s, v_s, kk_s, b_s, ld_s, y_s, pe_s, *, tt):
    ti = pl.program_id(1)

    @pl.when(ti == 0)
    def _():
        last_ref[...] = shift_ref[0]
        s_ref[0] = s0_ref[0]

    z = z_ref[0]
    rolled = pltpu.roll(z, 1, 0)
    zprev = jnp.where(_iota(z.shape, 0) == 0, last_ref[...], rolled)
    last_ref[...] = z[tt - 1:tt, :]
    r, k2, v, kk, b, ld, g, bonus = _rwkv_prep(z, zprev, vec_ref, wa_ref, g2_ref, bd_ref)
    r_s[...] = r
    k_s[...] = k2
    v_s[...] = v
    kk_s[...] = kk
    b_s[...] = b
    ld_s[...] = ld

    c = RWKV_CHUNK
    ri = _iota((c, c), 0)
    ci = _iota((c, c), 1)
    strict = ri > ci
    incl = ri >= ci

    eye = (ri == ci).astype(F32)
    heads = [slice(h * HEAD_DIM, (h + 1) * HEAD_DIM) for h in range(N_HEADS)]
    n_chunks = tt // c
    group = RWKV_GROUP if n_chunks % RWKV_GROUP == 0 else 1

    def prepare(gi, carry):
        probs = []
        for j in range(group):
            ck = gi * group + j
            rows = pl.ds(pl.multiple_of(ck * c, c), c)
            ldc = ld_s[rows, :]
            cum = _dot3(tri_ref[...], ldc)
            p_in = jnp.exp(cum)
            inv_p = jnp.exp(-cum)
            kkt = kk_s[rows, :] * jnp.exp(cum - ldc)
            bt = b_s[rows, :] * inv_p
            kt = k_s[rows, :] * inv_p
            rt = r_s[rows, :] * p_in
            vv = v_s[rows, :]
            p_end = p_in[c - 1:c, :]
            pe_s[pl.ds(ck, 1), :] = p_end
            for hs in heads:
                probs.append(dict(kkt=kkt[:, hs], bt=bt[:, hs], kt=kt[:, hs], rt=rt[:, hs], v=vv[:, hs],
                                  p_end=p_end[:, hs]))
        blocks = [_dotc(jnp.concatenate([p['kkt'], p['rt']], axis=0),
                        jnp.concatenate([p['bt'], p['kt']], axis=0), _dot_nt) for p in probs]
        lows = [jnp.where(strict, bl[:c, :c], 0.0) for bl in blocks]
        gvs = [_dotc(jnp.where(strict, bl[:c, c:], 0.0), p['v']) for bl, p in zip(blocks, probs)]
        tinvs = _tri_inv_all(lows)
        ke_ue = [_dotc(t, jnp.concatenate([p['kkt'], gv], axis=1)) for t, p, gv in zip(tinvs, probs, gvs)]
        corr = [_dotc(jnp.where(incl, bl[c:, :c], 0.0), ku) for bl, ku in zip(blocks, ke_ue)]
        avs = [_dotc(jnp.where(incl, bl[c:, c:], 0.0), p['v']) for bl, p in zip(blocks, probs)]
        xps = [_dotc(ku[:, :HEAD_DIM], p['bt'], _dot_tn) * p['p_end'] for ku, p in zip(ke_ue, probs)]
        bcs = [_dotc(jnp.concatenate([p['v'], ku[:, HEAD_DIM:]], axis=0),
                     jnp.concatenate([p['kt'], -p['bt']], axis=0), _dot_tn) * p['p_end']
               for ku, p in zip(ke_ue, probs)]
        res = [p['rt'] - cr[:, :HEAD_DIM] for p, cr in zip(probs, corr)]
        y0s = [av - cr[:, HEAD_DIM:] for av, cr in zip(avs, corr)]
        for j in range(group):
            rows = pl.ds(pl.multiple_of((gi * group + j) * c, c), c)
            sl = slice(j * N_HEADS, (j + 1) * N_HEADS)
            r_s[rows, :] = jnp.concatenate(res[sl], axis=1)
            v_s[rows, :] = jnp.concatenate(y0s[sl], axis=1)
            kk_s[rows, :] = jnp.concatenate(xps[sl], axis=1)
            b_s[rows, :] = jnp.concatenate(bcs[sl], axis=1)
        return carry

    lax.fori_loop(0, n_chunks // group, prepare, 0)

    def advance(ck, carry):
        rows = pl.ds(pl.multiple_of(ck * c, c), c)
        re, y0, xp, bc = r_s[rows, :], v_s[rows, :], kk_s[rows, :], b_s[rows, :]
        p_end = pe_s[pl.ds(ck, 1), :]
        states = [s_ref[0, h] for h in range(N_HEADS)]
        ys = [_dotc(re[:, hs], s_h, _dot_nt) + y0[:, hs] for hs, s_h in zip(heads, states)]
        for h, (hs, s_h) in enumerate(zip(heads, states)):
            s_ref[0, h] = s_h * p_end[:, hs] - _dotc(s_h, xp[:, hs]) + bc[:, hs]
        y_s[rows, :] = jnp.concatenate(ys, axis=1)
        return carry

    lax.fori_loop(0, n_chunks, advance, 0, unroll=True)
    o_ref[0] = _rwkv_finish(y_s[...], bonus, g, vec_ref, bd_ref).astype(o_ref.dtype)


def rwkv_seq(z, shift, s0, vec, wa, g2, bd, tri):
    b, t, w = z.shape
    tt = _pick_tile(t, 512, RWKV_CHUNK)
    assert tt % RWKV_CHUNK == 0
    const = lambda i, j: (0, 0)
    scr = pltpu.VMEM((tt, BRANCH_W), F32)
    return pl.pallas_call(
        functools.partial(_rwkv_seq_kernel, tt=tt), grid=(b, t // tt),
        in_specs=[pl.BlockSpec((1, tt, w), lambda i, j: (i, j, 0)),
                  pl.BlockSpec((1, 1, w), lambda i, j: (i, 0, 0)),
                  pl.BlockSpec((1, N_HEADS, HEAD_DIM, HEAD_DIM), lambda i, j: (i, 0, 0, 0)),
                  pl.BlockSpec(vec.shape, const), pl.BlockSpec(wa.shape, const),
                  pl.BlockSpec(g2.shape, const), pl.BlockSpec(bd.shape, const),
                  pl.BlockSpec(tri.shape, const)],
        out_specs=[pl.BlockSpec((1, tt, BRANCH_W), lambda i, j: (i, j, 0)),
                   pl.BlockSpec((1, N_HEADS, HEAD_DIM, HEAD_DIM), lambda i, j: (i, 0, 0, 0))],
        out_shape=[jax.ShapeDtypeStruct((b, t, BRANCH_W), BF16),
                   jax.ShapeDtypeStruct((b, N_HEADS, HEAD_DIM, HEAD_DIM), F32)],
        scratch_shapes=[pltpu.VMEM((1, w), F32)] + [scr] * 7 + [pltpu.VMEM((tt // RWKV_CHUNK, BRANCH_W), F32)],
        compiler_params=_params("parallel", "arbitrary"), name="rwkv_seq",
    )(z, shift, s0, vec, wa, g2, bd, tri)


def _rwkv_step_kernel(z_ref, shift_ref, s0_ref, vec_ref, wa_ref, g2_ref, bd_ref, o_ref, s_ref):
    rows = 8
    z = jnp.broadcast_to(z_ref[0], (rows, z_ref.shape[2]))
    zprev = jnp.broadcast_to(shift_ref[0], z.shape)
    r, k2, v, kk, b, ld, g, bonus = _rwkv_prep(z, zprev, vec_ref, wa_ref, g2_ref, bd_ref)
    decay = jnp.exp(ld)
    eye = _iota((HEAD_DIM, HEAD_DIM), 0) == _iota((HEAD_DIM, HEAD_DIM), 1)
    ys = []
    for h in range(N_HEADS):
        hs = slice(h * HEAD_DIM, (h + 1) * HEAD_DIM)
        s_h = s0_ref[0, h]
        sk = jnp.sum(s_h * kk[0:1, hs], axis=1, keepdims=True)
        v_col = jnp.sum(jnp.where(eye, v[0:1, hs], 0.0), axis=1, keepdims=True)
        s_new = s_h * decay[0:1, hs] - sk * b[0:1, hs] + v_col * k2[0:1, hs]
        s_ref[0, h] = s_new
        y_col = jnp.sum(s_new * r[0:1, hs], axis=1, keepdims=True)
        ys.append(jnp.sum(jnp.where(eye, y_col, 0.0), axis=0, keepdims=True))
    y = jnp.broadcast_to(jnp.concatenate(ys, axis=1), (rows, BRANCH_W))
    o_ref[0] = _rwkv_finish(y, bonus, g, vec_ref, bd_ref)[0:1].astype(o_ref.dtype)


def rwkv_step(z, shift, s0, vec, wa, g2, bd):
    b, _, w = z.shape
    const = lambda i: (0, 0)
    return pl.pallas_call(
        _rwkv_step_kernel, grid=(b,),
        in_specs=[pl.BlockSpec((1, 1, w), lambda i: (i, 0, 0)),
                  pl.BlockSpec((1, 1, w), lambda i: (i, 0, 0)),
                  pl.BlockSpec((1, N_HEADS, HEAD_DIM, HEAD_DIM), lambda i: (i, 0, 0, 0)),
                  pl.BlockSpec(vec.shape, const), pl.BlockSpec(wa.shape, const),
                  pl.BlockSpec(g2.shape, const), pl.BlockSpec(bd.shape, const)],
        out_specs=[pl.BlockSpec((1, 1, BRANCH_W), lambda i: (i, 0, 0)),
                   pl.BlockSpec((1, N_HEADS, HEAD_DIM, HEAD_DIM), lambda i: (i, 0, 0, 0))],
        out_shape=[jax.ShapeDtypeStruct((b, 1, BRANCH_W), BF16),
                   jax.ShapeDtypeStruct((b, N_HEADS, HEAD_DIM, HEAD_DIM), F32)],
        compiler_params=_params("parallel"), name="rwkv_step",
    )(z, shift, s0, vec, wa, g2, bd)


def _stack_heads(x):
    rows = [x[:, h * HEAD_DIM:(h + 1) * HEAD_DIM] for h in range(N_HEADS)]
    return jnp.concatenate(rows + [jnp.zeros((8 - N_HEADS, HEAD_DIM), x.dtype)], axis=0)


def _unstack_heads(x):
    return jnp.concatenate([x[h:h + 1, :] for h in range(N_HEADS)], axis=1)


def _column(row):
    n = row.shape[1]
    eye = _iota((n, n), 0) == _iota((n, n), 1)
    return jnp.sum(jnp.where(eye, row, 0.0), axis=1, keepdims=True)


def _to_row(col):
    n = col.shape[0]
    eye = _iota((n, n), 0) == _iota((n, n), 1)
    return jnp.sum(jnp.where(eye, col, 0.0), axis=0, keepdims=True)


def _page_scores(k_page, q_cols, s_ref, n_pages):
    def body(p, c):
        for h, qc in enumerate(q_cols):
            s_ref[h, pl.ds(p, 1), :] = jnp.sum(k_page(p, h) * qc, axis=0, keepdims=True)
        return c

    lax.fori_loop(0, n_pages, body, 0, unroll=4)


def _page_values(v_page, p_ref, n_pages):
    def body(p, accs):
        return tuple(acc + v_page(p, h) * p_ref[h, pl.ds(p, 1), :] for h, acc in enumerate(accs))

    init = tuple(jnp.zeros((HEAD_DIM, p_ref.shape[2]), F32) for _ in range(N_HEADS))
    return [jnp.sum(a, axis=1, keepdims=True) for a in lax.fori_loop(0, n_pages, body, init, unroll=4)]


def _softmax_with_new(s, mask, s_new):
    s = jnp.where(mask, s, NEG)
    m = jnp.maximum(jnp.max(jnp.max(s, axis=1, keepdims=True), axis=0, keepdims=True), s_new)
    p = jnp.where(mask, jnp.exp(s - m), 0.0)
    p_new = jnp.exp(s_new - m)
    den = jnp.maximum(jnp.sum(jnp.sum(p, axis=1, keepdims=True), axis=0, keepdims=True) + p_new, 1e-30)
    return p / den, p_new / den


def _nsa_sample_kernel(pt_ref, z_ref, tab_ref, win_ref, w1_ref, pe_ref, w2_ref, cover_ref, pj_ref, hr_ref,
                       cmp_hbm, slc_hbm, o_ref, slc_row_ref, win_row_ref,
                       cbuf, sbuf, rows_s, lines_s, s_s, p_s, sem,
                       *, base, n_pages, page, past, n_cmp, n_slc, k_sel):
    bi = pl.program_id(0)
    n_lines = past // CMP_STRIDE
    slot = lax.rem(bi, 2)

    def copies(seq, sl, p):
        pg = pt_ref[seq * n_pages + p] + base
        return (pltpu.make_async_copy(cmp_hbm.at[pg], cbuf.at[sl, p], sem.at[sl, 0]),
                pltpu.make_async_copy(slc_hbm.at[pg], sbuf.at[sl, p], sem.at[sl, 1]))

    def start_fetch(seq, sl):
        for p in range(n_pages):
            for cp in copies(seq, sl, p):
                cp.start()

    @pl.when(bi == 0)
    def _():
        start_fetch(0, 0)

    @pl.when(bi + 1 < pl.num_programs(0))
    def _():
        start_fetch(bi + 1, 1 - slot)

    z = z_ref[0]
    tab = tab_ref[...]
    q = z[:, :BRANCH_W] * SCALE
    q4 = _stack_heads(q)
    q4r_f32 = _stack_heads(_rope_heads(q, tab))
    q4r = q4r_f32.astype(BF16)
    slc_row = _rope(z[:, 384:512], tab, half_only=True)
    win_row = _rope(z[:, 512:640], tab, half_only=True)
    slc_row_ref[0] = slc_row
    win_row_ref[0] = win_row

    for p in range(n_pages):
        for cp in copies(bi, slot, p):
            cp.wait()

    def xpose(p, c):
        rows_s[pl.ds(pl.multiple_of(p * page, page), page), :] = cbuf[slot, p].T
        return c

    lax.fori_loop(0, n_pages, xpose, 0, unroll=8)
    first = _iota((n_lines, LANES), 1) < HEAD_DIM
    for i in range(CMP_STRIDE // 2):
        a = rows_s[pl.ds(2 * i, n_lines, stride=CMP_STRIDE), :]
        b = rows_s[pl.ds(2 * i + 1, n_lines, stride=CMP_STRIDE), :]
        lines_s[0, :, i * LANES:(i + 1) * LANES] = jnp.where(first, a, pltpu.roll(b, HEAD_DIM, 1))
        lines_s[1, :, i * LANES:(i + 1) * LANES] = jnp.where(first, pltpu.roll(a, HEAD_DIM, 1), b)

    cmpkv = _compress((lines_s[0], lines_s[1]), w1_ref, pe_ref, w2_ref)
    ck = cmpkv[:, :HEAD_DIM].astype(BF16)
    cv = cmpkv[:, HEAD_DIM:].astype(BF16)
    n = _iota((1, cmpkv.shape[0]), 1)
    cmask = (n * CMP_STRIDE + (CMP_BLOCK - 1) <= past) & (n < n_cmp)
    p_cmp = _msoftmax(_dot_nt(q4.astype(BF16), ck), cmask)
    o_cmp = _dot(p_cmp.astype(BF16), cv)
    psum = jnp.sum(jnp.where(_iota(p_cmp.shape, 0) < N_HEADS, p_cmp, 0.0), axis=0, keepdims=True)
    imp = _dot2_exact_rhs(jnp.broadcast_to(psum, p_cmp.shape), cover_ref[...])[0:1]
    t_pos = jnp.full((1, 1), past, jnp.int32)
    sel = _select_blocks(imp, t_pos, n_slc, k_sel)

    def with_new_key(s_past, mask, new_row, v_past):
        s_new = jnp.sum(q4r_f32 * new_row[:, :HEAD_DIM], axis=1, keepdims=True)
        s_past = jnp.where(mask, s_past, NEG)
        m = jnp.maximum(jnp.max(s_past, axis=1, keepdims=True), s_new)
        p_past = jnp.where(mask, jnp.exp(s_past - m), 0.0)
        p_new = jnp.exp(s_new - m)
        den = jnp.maximum(jnp.sum(p_past, axis=1, keepdims=True) + p_new, 1e-30)
        return _dot((p_past / den).astype(BF16), v_past) + (p_new / den) * new_row[:, HEAD_DIM:]

    sel_pages = _dot((pj_ref[...] * sel.astype(F32)).astype(BF16), hr_ref[...]) > 0.5
    _page_scores(lambda p, h: sbuf[slot, p, 0], [_column(q4r_f32[h:h + 1, :]) for h in range(N_HEADS)], s_s,
                 n_pages)
    p_new = []
    for h in range(N_HEADS):
        s_new = jnp.sum(q4r_f32[h:h + 1, :] * slc_row[:, :HEAD_DIM], axis=1, keepdims=True)
        p_s[h], pn = _softmax_with_new(s_s[h], sel_pages, s_new)
        p_new.append(pn)
    o_slc = [_to_row(col) + p_new[h] * slc_row[:, HEAD_DIM:]
             for h, col in enumerate(_page_values(lambda p, h: sbuf[slot, p, 1], p_s, n_pages))]
    wrows = win_ref[0]
    wk = wrows[:, :HEAD_DIM].astype(BF16)
    wv = wrows[:, HEAD_DIM:].astype(BF16)
    o_win = with_new_key(_dot_nt(q4r, wk), jnp.full((1, wrows.shape[0]), True), win_row, wv)

    g = _sigmoid(z[:, 640:640 + 3 * N_HEADS])
    outs = []
    for h in range(N_HEADS):
        outs.append(g[:, 3 * h:3 * h + 1] * o_cmp[h:h + 1] + g[:, 3 * h + 1:3 * h + 2] * o_slc[h]
                    + g[:, 3 * h + 2:3 * h + 3] * o_win[h:h + 1])
    o_ref[0] = jnp.concatenate(outs, axis=1).astype(o_ref.dtype)


def nsa_sample(page_table, z_nsa, tab, win, w1c, pe2, w2c, cover, pj, hr, cache_cmp, cache_slc,
               layer, n_pool, page, n_cmp, n_slc):
    b = z_nsa.shape[0]
    n_pages = page_table.shape[1]
    past = n_pages * page
    kern = functools.partial(_nsa_sample_kernel, base=layer * n_pool, n_pages=n_pages, page=page, past=past,
                             n_cmp=n_cmp, n_slc=n_slc, k_sel=min(N_SELECT, n_slc))
    const = lambda i, pt: (0, 0)
    row = lambda i, pt: (i, 0, 0)
    lay = lambda i, pt: (layer, 0, 0, 0)
    grid_spec = pltpu.PrefetchScalarGridSpec(
        num_scalar_prefetch=1, grid=(b,),
        in_specs=[pl.BlockSpec((1, 1, z_nsa.shape[2]), row),
                  pl.BlockSpec(tab.shape, lambda i, pt: (0, 0, 0)),
                  pl.BlockSpec((1,) + win.shape[1:], row),
                  pl.BlockSpec((None,) + w1c.shape[1:], lay), pl.BlockSpec((None,) + pe2.shape[1:], lay),
                  pl.BlockSpec((None,) + w2c.shape[1:], lay), pl.BlockSpec(cover.shape, const),
                  pl.BlockSpec(pj.shape, const), pl.BlockSpec(hr.shape, const),
                  pl.BlockSpec(memory_space=pl.ANY), pl.BlockSpec(memory_space=pl.ANY)],
        out_specs=[pl.BlockSpec((1, 1, BRANCH_W), row), pl.BlockSpec((1, 1, LANES), row),
                   pl.BlockSpec((1, 1, LANES), row)],
        scratch_shapes=[pltpu.VMEM((2, n_pages, 2 * HEAD_DIM, page), F32),
                        pltpu.VMEM((2, n_pages, 2, HEAD_DIM, page), F32),
                        pltpu.VMEM((past, 2 * HEAD_DIM), F32),
                        pltpu.VMEM((2, past // CMP_STRIDE, CMP_STRIDE * HEAD_DIM), F32),
                        pltpu.VMEM((N_HEADS, n_pages, page), F32),
                        pltpu.VMEM((N_HEADS, n_pages, page), F32),
                        pltpu.SemaphoreType.DMA((2, 2))])
    return pl.pallas_call(
        kern, grid_spec=grid_spec,
        out_shape=[jax.ShapeDtypeStruct((b, 1, BRANCH_W), BF16),
                   jax.ShapeDtypeStruct((b, 1, LANES), F32), jax.ShapeDtypeStruct((b, 1, LANES), F32)],
        compiler_params=_params("arbitrary"), name="nsa_sample",
    )(page_table.reshape(-1), z_nsa, tab, win, w1c, pe2, w2c, cover, pj, hr, cache_cmp, cache_slc)


def _moba_sample_kernel(pt_ref, z_ref, tab_ref, kv_hbm, o_ref, row_ref, buf, s_s, p_s, sem,
                        *, base, n_pages, pages_per_block, topk):
    bi = pl.program_id(0)
    slot = lax.rem(bi, 2)

    def copy(seq, sl, p):
        pg = pt_ref[seq * n_pages + p] + base
        return pltpu.make_async_copy(kv_hbm.at[pg], buf.at[sl, p], sem.at[sl])

    @pl.when(bi == 0)
    def _():
        for p in range(n_pages):
            copy(0, 0, p).start()

    @pl.when(bi + 1 < pl.num_programs(0))
    def _():
        for p in range(n_pages):
            copy(bi + 1, 1 - slot, p).start()

    z = z_ref[0]
    tab = tab_ref[...]
    q = z[:, :BRANCH_W] * SCALE
    qr = _rope_heads(q, tab)
    k_new = _rope_heads(z[:, BRANCH_W:2 * BRANCH_W], tab)
    v_new = z[:, 2 * BRANCH_W:3 * BRANCH_W]
    row_ref[0] = jnp.concatenate([k_new, v_new], axis=1)

    for p in range(n_pages):
        copy(bi, slot, p).wait()

    heads = [slice(h * HEAD_DIM, (h + 1) * HEAD_DIM) for h in range(N_HEADS)]
    _page_scores(lambda p, h: buf[slot, p, h], [_column(qr[:, hs]) for hs in heads], s_s, n_pages)
    pi = _iota((n_pages, n_pages), 0)
    pj = _iota((n_pages, n_pages), 1)
    shift = pages_per_block.bit_length() - 1
    same_block = lax.shift_right_logical(pi, shift) == lax.shift_right_logical(pj, shift)
    p_new = []
    for h, hs in enumerate(heads):
        s = s_s[h]
        page_sum = jnp.sum(s, axis=1, keepdims=True)
        page_sum_row = jnp.sum(jnp.where(pi == pj, page_sum, 0.0), axis=0, keepdims=True)
        gate_col = jnp.sum(jnp.where(same_block, page_sum_row, 0.0), axis=1, keepdims=True)
        gate_row = jnp.sum(jnp.where(same_block, page_sum, 0.0), axis=0, keepdims=True)
        beats = ((gate_row > gate_col) | ((gate_row == gate_col) & (pj < pi))) & jnp.logical_not(same_block)
        picked = jnp.sum(beats.astype(F32), axis=1, keepdims=True) < pages_per_block * topk - 0.5
        s_new = jnp.sum(qr[:, hs] * k_new[:, hs], axis=1, keepdims=True)
        p_s[h], pn = _softmax_with_new(s, picked, s_new)
        p_new.append(pn)
    cols = _page_values(lambda p, h: buf[slot, p, N_HEADS + h], p_s, n_pages)
    outs = [_to_row(cols[h]) + p_new[h] * v_new[:, hs] for h, hs in enumerate(heads)]
    o_ref[0] = jnp.concatenate(outs, axis=1).astype(o_ref.dtype)


def moba_sample(page_table, z_moba, tab, cache, layer, n_pool, page):
    b = z_moba.shape[0]
    n_pages = page_table.shape[1]
    past = n_pages * page
    assert past % MOBA_BLOCK == 0 and MOBA_BLOCK % page == 0
    ppb = MOBA_BLOCK // page
    assert ppb & (ppb - 1) == 0
    kern = functools.partial(_moba_sample_kernel, base=layer * n_pool, n_pages=n_pages, pages_per_block=ppb,
                             topk=min(MOBA_TOPK, past // MOBA_BLOCK + 1))
    row = lambda i, pt: (i, 0, 0)
    grid_spec = pltpu.PrefetchScalarGridSpec(
        num_scalar_prefetch=1, grid=(b,),
        in_specs=[pl.BlockSpec((1, 1, z_moba.shape[2]), row),
                  pl.BlockSpec(tab.shape, lambda i, pt: (0, 0, 0)),
                  pl.BlockSpec(memory_space=pl.ANY)],
        out_specs=[pl.BlockSpec((1, 1, BRANCH_W), row), pl.BlockSpec((1, 1, 2 * BRANCH_W), row)],
        scratch_shapes=[pltpu.VMEM((2, n_pages, 2 * N_HEADS, HEAD_DIM, page), F32),
                        pltpu.VMEM((N_HEADS, n_pages, page), F32),
                        pltpu.VMEM((N_HEADS, n_pages, page), F32),
                        pltpu.SemaphoreType.DMA((2,))])
    return pl.pallas_call(
        kern, grid_spec=grid_spec,
        out_shape=[jax.ShapeDtypeStruct((b, 1, BRANCH_W), BF16),
                   jax.ShapeDtypeStruct((b, 1, 2 * BRANCH_W), F32)],
        compiler_params=_params("arbitrary"), name="moba_sample",
    )(page_table.reshape(-1), z_moba, tab, cache)


def _rope_table(pos):
    inv = ROPE_THETA ** (-jnp.arange(ROT_HALF, dtype=F32) / ROT_HALF)
    ang = pos.astype(F32)[:, None] * inv[None, :]
    cos, sin = jnp.cos(ang), jnp.sin(ang)
    t = pos.shape[0]
    ones = jnp.ones((t, HEAD_DIM - 2 * ROT_HALF), F32)
    zeros = jnp.zeros((t, HEAD_DIM - 2 * ROT_HALF), F32)
    z8 = jnp.zeros((t, ROT_HALF), F32)
    c = jnp.concatenate([cos, cos, ones], axis=1)
    sa = jnp.concatenate([-sin, z8, zeros], axis=1)
    sb = jnp.concatenate([z8, sin, zeros], axis=1)
    return jnp.stack([jnp.tile(c, (1, 2)), jnp.tile(sa, (1, 2)), jnp.tile(sb, (1, 2))], axis=0)


def _cover_matrix(n_rows, n_cmp, n_slc, width):
    starts = np.arange(n_rows) * CMP_STRIDE
    blk = np.arange(width) * SLC_BLOCK
    cover = (starts[:, None] < blk[None, :] + SLC_BLOCK) & (starts[:, None] + CMP_BLOCK > blk[None, :])
    cover &= (np.arange(n_rows)[:, None] < n_cmp) & (np.arange(width)[None, :] < n_slc)
    return jnp.asarray(cover, BF16)


def _expand_matrix(n_blocks, block, n_keys):
    e = np.arange(n_blocks)[:, None] == (np.arange(n_keys)[None, :] // block)
    return jnp.asarray(e, BF16)


def _round_up(n, m):
    return -(-n // m) * m


COL_NSA, COL_MOBA, COL_CROSS, COL_RWKV, COL_MERGE = 0, 768, 1536, 2304, 4096


def _pack_w_in(w_in):
    def zeros(n):
        return jnp.zeros(w_in.shape[:2] + (n,), w_in.dtype)

    parts = [w_in[..., 0:652], zeros(NSA_PAD - 652), w_in[..., 1708:2476], w_in[..., 2476:2732],
             zeros(COL_RWKV - COL_CROSS - BRANCH_W), w_in[..., 652:1708], zeros(RWKV_PAD - RWKV_IN),
             zeros(COL_MERGE - COL_RWKV - RWKV_PAD), w_in[..., 2732:]]
    return jnp.concatenate(parts, axis=-1).astype(BF16)


def _compress_weights(nsa_pe, nsa_phi_w1, nsa_phi_w2):
    depth, _, _, hidden = nsa_phi_w1.shape
    line = CMP_STRIDE * HEAD_DIM
    w1 = nsa_phi_w1.reshape(depth, 2, 2, line, hidden)
    w1 = jnp.transpose(w1, (0, 1, 3, 2, 4)).reshape(depth, 2, line, 2 * hidden).astype(BF16)
    pe = nsa_pe.reshape(depth, 2, CMP_STRIDE, 2, HEAD_DIM)
    pe = jnp.transpose(pe, (0, 3, 1, 2, 4)).reshape(depth, 2, 2, line)
    return w1, jnp.pad(pe, ((0, 0), (0, 0), (0, 6), (0, 0))), nsa_phi_w2.astype(BF16)


def _layer_weights(l, rwkv_mu, rwkv_w0, rwkv_w2, rwkv_a0, rwkv_a2,
                   rwkv_g2, rwkv_k_k, rwkv_k_a, rwkv_r_k, rwkv_ln_w, rwkv_ln_b):
    w = {}
    vec = jnp.zeros((8, RWKV_PAD), F32)
    vec = vec.at[0, :RWKV_IN].set(rwkv_mu[l])
    vec = vec.at[1, 0:256].set(rwkv_w0[l]).at[1, 256:512].set(rwkv_a0[l])
    for i, p in enumerate((rwkv_k_k, rwkv_k_a, rwkv_r_k, rwkv_ln_w, rwkv_ln_b)):
        vec = vec.at[2 + i, 0:256].set(p[l])
    w['rwkv_vec'] = vec
    wa = jnp.zeros((128, 512), F32).at[0:64, 0:256].set(rwkv_w2[l]).at[64:128, 256:512].set(rwkv_a2[l])
    w['rwkv_wa'] = wa.astype(BF16)
    w['rwkv_g2'] = jnp.pad(rwkv_g2[l], ((0, 256 - rwkv_g2.shape[1]), (0, 0))).astype(BF16)
    return w


def _project(h, w_all, l, d):
    z_nsa = mm(h, w_all, l, COL_NSA, NSA_PAD)
    z_rwkv = mm(h, w_all, l, COL_RWKV, RWKV_PAD)
    z_moba = mm(h, w_all, l, COL_MOBA, 3 * BRANCH_W)
    z_cross = mm(h, w_all, l, COL_CROSS, BRANCH_W)
    gates = mm(h, w_all, l, COL_MERGE, 4 * d, out_dtype=BF16, gate=True)
    return z_nsa, z_rwkv, z_moba, z_cross, gates


def _mixers_prompt(h, w_all, l, w, tab, mem_kv, consts, b, t):
    z_nsa, z_rwkv, z_moba, z_cross, gates = _project(h, w_all, l, h.shape[1])
    z_nsa = z_nsa.reshape(b, t, NSA_PAD)
    z_rwkv = z_rwkv.reshape(b, t, RWKV_PAD)
    z_moba = z_moba.reshape(b, t, 3 * BRANCH_W)
    z_cross = z_cross.reshape(b, t, BRANCH_W)

    new_cmp = z_nsa[:, :, 256:384]
    new_slc, new_win = nsa_rope(z_nsa, tab)
    lines = [new_cmp[:, :, j * HEAD_DIM:(j + 1) * HEAD_DIM].reshape(b, t // CMP_STRIDE, CMP_STRIDE * HEAD_DIM)
             for j in range(2)]
    cmp_kv = compress(lines[0], lines[1], *consts['cmp_w'], l)
    o_nsa = nsa_attn(z_nsa, tab, cmp_kv, new_slc, new_win, consts['cover_p'], consts['n_cmp_p'],
                     consts['n_slc_p'])

    zero_shift = jnp.zeros((b, 1, RWKV_PAD), F32)
    zero_state = jnp.zeros((b, N_HEADS, HEAD_DIM, HEAD_DIM), F32)
    o_rwkv, new_state = rwkv_seq(z_rwkv, zero_shift, zero_state, w['rwkv_vec'], w['rwkv_wa'], w['rwkv_g2'],
                                 consts['bd'], consts['tri'])
    new_shift = z_rwkv[:, t - 1, :RWKV_IN]

    new_moba, means = moba_rope(z_moba, tab)
    o_moba = moba_attn(z_moba, tab, new_moba, means.reshape(b, -1, BRANCH_W))
    o_cross = cross_attn(z_cross, mem_kv)

    outs = [o.reshape(b * t, BRANCH_W) for o in (o_nsa, o_rwkv, o_moba, o_cross)]
    win_keep = min(WINDOW, t)
    caches = (new_cmp.reshape(b, t, 2, HEAD_DIM), new_slc.reshape(b, t, 2, HEAD_DIM),
              new_win[:, t - win_keep:].reshape(b, win_keep, 2, HEAD_DIM),
              new_moba.reshape(b, t, 2, N_HEADS, HEAD_DIM), new_state, new_shift)
    return outs, gates, caches


def _mixers_sample(h, w_all, l, w, tab, page_table, caches_in, consts, b):
    cmp_pages, slc_pages, cache_win, moba_pages, cache_mem, state_rwkv, state_shift, n_pool, page = caches_in
    z_nsa, z_rwkv, z_moba, z_cross, gates = _project(h, w_all, l, h.shape[1])
    z_nsa = z_nsa.reshape(b, 1, NSA_PAD)
    z_rwkv = z_rwkv.reshape(b, 1, RWKV_PAD)
    z_moba = z_moba.reshape(b, 1, 3 * BRANCH_W)
    z_cross = z_cross.reshape(b, 1, BRANCH_W)

    win = cache_win[l].reshape(b, -1, LANES)
    o_nsa, slc_row, win_row = nsa_sample(
        page_table, z_nsa, tab, win, *consts['cmp_w'], consts['cover_s'], consts['pj_s'],
        consts['hr_s'], cmp_pages, slc_pages, l, n_pool, page, consts['n_cmp_s'], consts['n_slc_s'])
    new_cmp = z_nsa[:, :, 256:384]
    full_win = jnp.concatenate([win, win_row], axis=1)
    keep = min(WINDOW, full_win.shape[1])
    new_buf = full_win[:, full_win.shape[1] - keep:]

    shift = jnp.pad(state_shift[l], ((0, 0), (0, RWKV_PAD - RWKV_IN))).reshape(b, 1, RWKV_PAD)
    o_rwkv, new_state = rwkv_step(z_rwkv, shift, state_rwkv[l], w['rwkv_vec'], w['rwkv_wa'], w['rwkv_g2'],
                                  consts['bd'])
    new_shift = z_rwkv[:, 0, :RWKV_IN]

    o_moba, moba_row = moba_sample(page_table, z_moba, tab, moba_pages, l, n_pool, page)
    o_cross = cross_attn(z_cross, cache_mem[l].reshape(b, -1, 2 * BRANCH_W))

    outs = [o.reshape(b, BRANCH_W) for o in (o_nsa, o_rwkv, o_moba, o_cross)]
    caches = (new_cmp.reshape(b, 1, 2, HEAD_DIM), slc_row.reshape(b, 1, 2, HEAD_DIM),
              new_buf.reshape(b, keep, 2, HEAD_DIM), moba_row.reshape(b, 1, 2, N_HEADS, HEAD_DIM),
              new_state, new_shift)
    return outs, gates, caches


def kernel(x_prompt, x_sample, cache_nsa_cmp, cache_nsa_slc, cache_nsa_win, cache_moba, cache_mem, state_rwkv, state_shift, page_table, mem_prompt, g_pre_mix, g_post_mix, g_pre_ffn, g_post_ffn, g_mem, w_in, nsa_pe, nsa_phi_w1, nsa_phi_w2, rwkv_mu, rwkv_w0, rwkv_w2, rwkv_a0, rwkv_a2, rwkv_g2, rwkv_k_k, rwkv_k_a, rwkv_r_k, rwkv_ln_w, rwkv_ln_b, w_mem_kv, w_branch, w_out, w_ffn_gate, w_ffn_up, w_ffn_down):
    bp, t, d = x_prompt.shape
    bs = x_sample.shape[0]
    assert x_sample.shape[1] == 1
    depth = w_in.shape[0]
    page = cache_nsa_cmp.shape[2]
    past = page_table.shape[1] * page
    n_mem = mem_prompt.shape[1]
    assert t % MOBA_BLOCK == 0 and past % MOBA_BLOCK == 0 and page % SLC_BLOCK == 0

    n_cmp_p = (t - CMP_BLOCK) // CMP_STRIDE + 1
    n_slc_p = -(-t // SLC_BLOCK)
    n_cmp_s = (past + 1 - CMP_BLOCK) // CMP_STRIDE + 1
    n_slc_s = -(-(past + 1) // SLC_BLOCK)
    consts = {
        'cmp_w': _compress_weights(nsa_pe, nsa_phi_w1, nsa_phi_w2),
        'n_cmp_p': n_cmp_p, 'n_slc_p': n_slc_p, 'n_cmp_s': n_cmp_s, 'n_slc_s': n_slc_s,
        'cover_p': _cover_matrix(t // CMP_STRIDE, n_cmp_p, n_slc_p, _round_up(n_slc_p, 8)).T,
        'cover_s': _cover_matrix(past // CMP_STRIDE, n_cmp_s, n_slc_s, _round_up(n_slc_s, LANES)),
        'pj_s': jnp.asarray(np.arange(_round_up(n_slc_s, LANES))[None, :] // (page // SLC_BLOCK)
                            == np.arange(past // page)[:, None], F32),
        'hr_s': jnp.asarray(np.arange(_round_up(n_slc_s, LANES))[:, None] % (page // SLC_BLOCK)
                            == np.arange(page)[None, :] // SLC_BLOCK, BF16),
        'bd': jnp.asarray(np.arange(BRANCH_W)[:, None] // HEAD_DIM == np.arange(BRANCH_W)[None, :] // HEAD_DIM, BF16),
        'tri': jnp.asarray(np.tril(np.ones((RWKV_CHUNK, RWKV_CHUNK), np.float32))),
    }
    tab_p = _rope_table(jnp.arange(t, dtype=jnp.int32))
    tab_s = _rope_table(jnp.full((1,), past, jnp.int32))

    xp = x_prompt.reshape(bp * t, d)
    xs = x_sample.reshape(bs, d)
    mem_flat = mem_prompt.reshape(bp * n_mem, d)
    n_pool = cache_nsa_cmp.shape[1]
    cmp_pages = jnp.transpose(cache_nsa_cmp, (0, 1, 3, 4, 2)).reshape(depth * n_pool, 2 * HEAD_DIM, page)
    slc_pages = jnp.transpose(cache_nsa_slc, (0, 1, 3, 4, 2)).reshape(depth * n_pool, 2, HEAD_DIM, page)
    moba_pages = jnp.transpose(cache_moba, (0, 1, 3, 4, 5, 2)).reshape(depth * n_pool, 2 * N_HEADS, HEAD_DIM, page)
    sample_caches = (cmp_pages, slc_pages, cache_nsa_win, moba_pages, cache_mem, state_rwkv, state_shift,
                     n_pool, page)
    w_all = _pack_w_in(w_in)
    w_mem, w_br, w_o = w_mem_kv.astype(BF16), w_branch.astype(BF16), w_out.astype(BF16)
    w_fg, w_fu, w_fd = w_ffn_gate.astype(BF16), w_ffn_up.astype(BF16), w_ffn_down.astype(BF16)
    outs_p, outs_s, mem_out = [], [], []
    for l in range(depth):
        w = _layer_weights(l, rwkv_mu, rwkv_w0, rwkv_w2, rwkv_a0,
                           rwkv_a2, rwkv_g2, rwkv_k_k, rwkv_k_a, rwkv_r_k, rwkv_ln_w, rwkv_ln_b)
        mem_kv = mm(rms_cast(mem_flat, g_mem[l]), w_mem, l, 0, 2 * BRANCH_W).reshape(bp, n_mem, 2 * BRANCH_W)
        mem_out.append(mem_kv.reshape(bp, n_mem, 2, N_HEADS, HEAD_DIM))

        o_p, gates_p, new_p = _mixers_prompt(rms_cast(xp, g_pre_mix[l]), w_all, l, w, tab_p, mem_kv, consts, bp, t)
        xp = merge(xp, o_p, gates_p, w_br, w_o, g_post_mix[l], l)
        xp = ffn(xp, g_pre_ffn[l], w_fg, w_fu, w_fd, g_post_ffn[l], l)

        o_s, gates_s, new_s = _mixers_sample(rms_cast(xs, g_pre_mix[l]), w_all, l, w, tab_s, page_table,
                                             sample_caches, consts, bs)
        xs = merge(xs, o_s, gates_s, w_br, w_o, g_post_mix[l], l)
        xs = ffn(xs, g_pre_ffn[l], w_fg, w_fu, w_fd, g_post_ffn[l], l)
        outs_p.append(new_p)
        outs_s.append(new_s)

    def stack(outs, i):
        return jnp.stack([o[i] for o in outs], axis=0)

    return (xp.reshape(bp, t, d), xs.reshape(bs, 1, d),
            stack(outs_p, 0), stack(outs_s, 0), stack(outs_p, 1), stack(outs_s, 1),
            stack(outs_p, 2), stack(outs_s, 2), stack(outs_p, 3), stack(outs_s, 3),
            jnp.stack(mem_out, axis=0),
            stack(outs_p, 4), stack(outs_s, 4), stack(outs_p, 5), stack(outs_s, 5))
```

```python
import functools

import jax
import jax.numpy as jnp
import numpy as np
from jax import lax
from jax.experimental import pallas as pl
from jax.experimental.pallas import tpu as pltpu

F32 = jnp.float32
BF16 = jnp.bfloat16

HEAD_DIM = 64
N_HEADS = 4
BRANCH_W = N_HEADS * HEAD_DIM
ROT_HALF = 8
ROPE_THETA = 500000.0
CMP_BLOCK = 32
CMP_STRIDE = 16
SLC_BLOCK = 64
N_SELECT = 16
WINDOW = 512
MOBA_BLOCK = 256
MOBA_TOPK = 3
RWKV_IN = 1056
RWKV_PAD = 1152
NSA_PAD = 768
RWKV_GN_EPS = 64e-5
RWKV_CHUNK = 64
NEG = -1e30
SCALE = HEAD_DIM ** -0.5
LANES = 128
ATTN_KEY_TILE = 256
FFN_TILE = 1408
VMEM_LIMIT = 48 * 1024 * 1024


def _iota(shape, dim):
    return lax.broadcasted_iota(jnp.int32, shape, dim)


def _dot(a, b):
    return lax.dot_general(a, b, (((1,), (0,)), ((), ())), preferred_element_type=F32)


def _dot_nt(a, b):
    return lax.dot_general(a, b, (((1,), (1,)), ((), ())), preferred_element_type=F32)


def _dot_tn(a, b):
    return lax.dot_general(a, b, (((0,), (0,)), ((), ())), preferred_element_type=F32)


def _split(x):
    hi = x.astype(BF16)
    lo = (x - hi.astype(F32)).astype(BF16)
    return hi, lo


def _dot3(a, b, dot=_dot):
    ah, al = _split(a)
    bh, bl = _split(b)
    return dot(ah, bh) + dot(al, bh) + dot(ah, bl)


def _dot2_exact_rhs(a, b_bf16):
    ah, al = _split(a)
    return _dot(ah, b_bf16) + _dot(al, b_bf16)


def _sigmoid(x):
    return 1.0 / (1.0 + jnp.exp(-x))


def _rms(x, g, eps=1e-6):
    return x * lax.rsqrt(jnp.mean(x * x, axis=-1, keepdims=True) + eps) * g


def _msoftmax(s, mask):
    s = jnp.where(mask, s, NEG)
    p = jnp.where(mask, jnp.exp(s - jnp.max(s, axis=-1, keepdims=True)), 0.0)
    return p / jnp.maximum(jnp.sum(p, axis=-1, keepdims=True), 1e-30)


def _with_ones(v_t):
    return jnp.concatenate([v_t, jnp.ones((8, v_t.shape[1]), v_t.dtype)], axis=0)


def _flash_heads_t(q_ts, ks, v_ts, biases, valid, carry):
    ss = []
    for q_t, k, bias in zip(q_ts, ks, biases):
        s = _dot(k, q_t)
        s = s if bias is None else s + bias
        ss.append(s if valid is None else jnp.where(valid, s, NEG))
    m_news = [jnp.maximum(c[0], jnp.max(s, axis=0, keepdims=True)) for c, s in zip(carry, ss)]
    pvs = [_dot(v_t, jnp.exp(s - m_new).astype(BF16)) for v_t, s, m_new in zip(v_ts, ss, m_news)]
    return tuple((m_new, jnp.exp(c[0] - m_new) * c[1] + pv) for c, m_new, pv in zip(carry, m_news, pvs))


def _flash_finish(carry):
    acc = carry[1]
    d = acc.shape[0] - 8
    return acc[:d] / jnp.maximum(acc[d:d + 1], 1e-30)


def _rope(x, tab, half_only=False):
    c, sa, sb = tab[0], tab[1], tab[2]
    if half_only:
        first = _iota(x.shape, 1) < HEAD_DIM
        c = jnp.where(first, c, 1.0)
        sa = jnp.where(first, sa, 0.0)
        sb = jnp.where(first, sb, 0.0)
    return x * c + pltpu.roll(x, LANES - ROT_HALF, 1) * sa + pltpu.roll(x, ROT_HALF, 1) * sb


def _rope_heads(x, tab):
    return jnp.concatenate([_rope(x[:, :LANES], tab), _rope(x[:, LANES:], tab)], axis=1)


def _rank_lt(score, n_real, k):
    lane = _iota(score.shape, 1)
    cnt = jnp.zeros(score.shape, jnp.int32)
    for i in range(n_real):
        ci = score[:, i:i + 1]
        beats = (ci > score) | ((ci == score) & (lane > i))
        cnt = cnt + beats.astype(jnp.int32)
    return cnt < k


def _pick_tile(n, target, mult=8):
    if n <= target:
        return n
    for t in range(target, 0, -1):
        if n % t == 0 and t % mult == 0:
            return t
    return n


def _params(*sem):
    return pltpu.CompilerParams(dimension_semantics=sem, vmem_limit_bytes=VMEM_LIMIT)


def _rms_cast_kernel(x_ref, g_ref, o_ref):
    o_ref[...] = _rms(x_ref[...], g_ref[...]).astype(BF16)


def rms_cast(x, g):
    m, d = x.shape
    tm = _pick_tile(m, 2048, 16)
    return pl.pallas_call(
        _rms_cast_kernel, grid=(m // tm,),
        in_specs=[pl.BlockSpec((tm, d), lambda i: (i, 0)), pl.BlockSpec((1, d), lambda i: (0, 0))],
        out_specs=pl.BlockSpec((tm, d), lambda i: (i, 0)),
        out_shape=jax.ShapeDtypeStruct((m, d), BF16),
        compiler_params=_params("parallel"), name="rms_cast",
    )(x, g.reshape(1, d))


def _mm_kernel(a_ref, b_ref, o_ref, *, gate):
    y = _dot(a_ref[...], b_ref[...])
    o_ref[...] = (_sigmoid(y) if gate else y).astype(o_ref.dtype)


def mm(a, w, layer, col0, n, out_dtype=F32, gate=False):
    m, k = a.shape
    tm = _pick_tile(m, 1024, 16)
    tn = n if n <= 1536 else _pick_tile(n, 1024, LANES)
    assert col0 % tn == 0 and n % tn == 0
    c0 = col0 // tn
    return pl.pallas_call(
        functools.partial(_mm_kernel, gate=gate), grid=(m // tm, n // tn),
        in_specs=[pl.BlockSpec((tm, k), lambda i, j: (i, 0)),
                  pl.BlockSpec((None, k, tn), lambda i, j: (layer, 0, c0 + j))],
        out_specs=pl.BlockSpec((tm, tn), lambda i, j: (i, j)),
        out_shape=jax.ShapeDtypeStruct((m, n), out_dtype),
        compiler_params=_params("parallel", "parallel"), name="mm",
    )(a, w)


def _ffn_kernel(x_ref, g1_ref, wg_ref, wu_ref, wd_ref, g2_ref, o_ref, h_ref, acc_ref):
    f = pl.program_id(1)

    @pl.when(f == 0)
    def _():
        h_ref[...] = _rms(x_ref[...], g1_ref[...]).astype(BF16)
        acc_ref[...] = jnp.zeros_like(acc_ref)

    h = h_ref[...]
    gate = _dot(h, wg_ref[...])
    up = _dot(h, wu_ref[...])
    act = (gate * _sigmoid(gate) * up).astype(BF16)
    acc_ref[...] += _dot(act, wd_ref[...])

    @pl.when(f == pl.num_programs(1) - 1)
    def _():
        o_ref[...] = x_ref[...] + _rms(acc_ref[...], g2_ref[...])


def ffn(x, g_pre, wg, wu, wd, g_post, layer):
    m, d = x.shape
    dff = wg.shape[2]
    tm = _pick_tile(m, 512, 16)
    tf = _pick_tile(dff, FFN_TILE, LANES)
    return pl.pallas_call(
        _ffn_kernel, grid=(m // tm, dff // tf),
        in_specs=[pl.BlockSpec((tm, d), lambda i, f: (i, 0)),
                  pl.BlockSpec((1, d), lambda i, f: (0, 0)),
                  pl.BlockSpec((None, d, tf), lambda i, f: (layer, 0, f)),
                  pl.BlockSpec((None, d, tf), lambda i, f: (layer, 0, f)),
                  pl.BlockSpec((None, tf, d), lambda i, f: (layer, f, 0)),
                  pl.BlockSpec((1, d), lambda i, f: (0, 0))],
        out_specs=pl.BlockSpec((tm, d), lambda i, f: (i, 0)),
        out_shape=jax.ShapeDtypeStruct((m, d), F32),
        scratch_shapes=[pltpu.VMEM((tm, d), BF16), pltpu.VMEM((tm, d), F32)],
        compiler_params=_params("parallel", "arbitrary"), name="ffn",
    )(x, g_pre.reshape(1, d), wg, wu, wd, g_post.reshape(1, d))


def _merge_kernel(x_ref, o0_ref, o1_ref, o2_ref, o3_ref, gate_ref, wb_ref, wo_ref, g_ref, out_ref):
    d = x_ref.shape[1]
    merged = None
    for b, o_ref in enumerate((o0_ref, o1_ref, o2_ref, o3_ref)):
        term = gate_ref[:, b * d:(b + 1) * d].astype(F32) * _dot(o_ref[...], wb_ref[b])
        merged = term if merged is None else merged + term
    y = _dot(merged.astype(BF16), wo_ref[...])
    out_ref[...] = x_ref[...] + _rms(y, g_ref[...])


def merge(x, outs, gates, w_branch, w_out, g_post, layer):
    m, d = x.shape
    bw = outs[0].shape[1]
    tm = _pick_tile(m, 512, 16)
    row = lambda i: (i, 0)
    return pl.pallas_call(
        _merge_kernel, grid=(m // tm,),
        in_specs=[pl.BlockSpec((tm, d), row)] + [pl.BlockSpec((tm, bw), row)] * 4 + [
            pl.BlockSpec((tm, 4 * d), row),
            pl.BlockSpec((None, 4, bw, d), lambda i: (layer, 0, 0, 0)),
            pl.BlockSpec((None, d, d), lambda i: (layer, 0, 0)),
            pl.BlockSpec((1, d), lambda i: (0, 0))],
        out_specs=pl.BlockSpec((tm, d), row),
        out_shape=jax.ShapeDtypeStruct((m, d), F32),
        compiler_params=_params("parallel"), name="merge",
    )(x, *outs, gates, w_branch, w_out, g_post.reshape(1, d))


def _nsa_rope_kernel(z_ref, tab_ref, slc_ref, win_ref):
    z = z_ref[0]
    tab = tab_ref[...]
    slc_ref[0] = _rope(z[:, 384:512], tab, half_only=True)
    win_ref[0] = _rope(z[:, 512:640], tab, half_only=True)


def nsa_rope(z_nsa, tab):
    b, t, w = z_nsa.shape
    tt = _pick_tile(t, 2048)
    out = jax.ShapeDtypeStruct((b, t, LANES), F32)
    return pl.pallas_call(
        _nsa_rope_kernel, grid=(b, t // tt),
        in_specs=[pl.BlockSpec((1, tt, w), lambda i, j: (i, j, 0)),
                  pl.BlockSpec((3, tt, LANES), lambda i, j: (0, j, 0))],
        out_specs=[pl.BlockSpec((1, tt, LANES), lambda i, j: (i, j, 0))] * 2,
        out_shape=[out, out],
        compiler_params=_params("parallel", "parallel"), name="nsa_rope",
    )(z_nsa, tab)


def _moba_rope_kernel(z_ref, tab_ref, kv_ref, mean_ref):
    z = z_ref[0]
    kr = _rope_heads(z[:, BRANCH_W:2 * BRANCH_W], tab_ref[...])
    kv_ref[0] = jnp.concatenate([kr, z[:, 2 * BRANCH_W:3 * BRANCH_W]], axis=1)
    for u in range(mean_ref.shape[1]):
        mean_ref[0, u] = jnp.sum(kr[u * MOBA_BLOCK:(u + 1) * MOBA_BLOCK], axis=0, keepdims=True) * (1.0 / MOBA_BLOCK)


def moba_rope(z_moba, tab):
    b, t, w = z_moba.shape
    nb = t // MOBA_BLOCK
    per_step = 4 if nb % 4 == 0 else 1
    rows = per_step * MOBA_BLOCK
    return pl.pallas_call(
        _moba_rope_kernel, grid=(b, nb // per_step),
        in_specs=[pl.BlockSpec((1, rows, w), lambda i, j: (i, j, 0)),
                  pl.BlockSpec((3, rows, LANES), lambda i, j: (0, j, 0))],
        out_specs=[pl.BlockSpec((1, rows, 2 * BRANCH_W), lambda i, j: (i, j, 0)),
                   pl.BlockSpec((1, per_step, 1, BRANCH_W), lambda i, j: (i, j, 0, 0))],
        out_shape=[jax.ShapeDtypeStruct((b, t, 2 * BRANCH_W), F32),
                   jax.ShapeDtypeStruct((b, nb, 1, BRANCH_W), F32)],
        compiler_params=_params("parallel", "parallel"), name="moba_rope",
    )(z_moba, tab)


def _compress(lines, w1_ref, pe_ref, w2_ref):
    outs = []
    for j, g in enumerate(lines):
        r = g.shape[0]
        ab = _dot(g.astype(BF16), w1_ref[j])
        pe = _dot(pe_ref[j].astype(BF16), w1_ref[j])
        half = ab.shape[1] // 2
        nxt = pltpu.roll(ab[:, half:], r - 1, 0)
        pre = ab[:, :half] + nxt + pe[0:1, :half] + pe[1:2, half:]
        hid = pre * _sigmoid(pre)
        outs.append(_dot(hid.astype(BF16), w2_ref[j]))
    return jnp.concatenate(outs, axis=1)


def _compress_kernel(gk_ref, gv_ref, w1_ref, pe_ref, w2_ref, o_ref):
    o_ref[0] = _compress((gk_ref[0], gv_ref[0]), w1_ref, pe_ref, w2_ref)


def compress(gk, gv, w1, pe2, w2, layer):
    b, r, w = gk.shape
    lay = lambda i: (layer, 0, 0, 0)
    return pl.pallas_call(
        _compress_kernel, grid=(b,),
        in_specs=[pl.BlockSpec((1, r, w), lambda i: (i, 0, 0)), pl.BlockSpec((1, r, w), lambda i: (i, 0, 0)),
                  pl.BlockSpec((None,) + w1.shape[1:], lay), pl.BlockSpec((None,) + pe2.shape[1:], lay),
                  pl.BlockSpec((None,) + w2.shape[1:], lay)],
        out_specs=pl.BlockSpec((1, r, LANES), lambda i: (i, 0, 0)),
        out_shape=jax.ShapeDtypeStruct((b, r, LANES), F32),
        compiler_params=_params("parallel"), name="nsa_compress",
    )(gk, gv, w1, pe2, w2)


def _rank_lt_rows(score, n_real, k):
    row = _iota(score.shape, 0)
    cnt = jnp.zeros(score.shape, jnp.int32)
    for i in range(n_real):
        ri = score[i:i + 1, :]
        beats = (ri > score) | ((ri == score) & (row > i))
        cnt = cnt + beats.astype(jnp.int32)
    return cnt < k


def _select_blocks(imp, t_pos, n_slc, k_sel):
    j = _iota(imp.shape, 1)
    own = lax.shift_right_logical(t_pos, 6)
    causal = (j * SLC_BLOCK <= t_pos) & (j < n_slc)
    forced = (j == 0) | (j == own) | (j == own - 1)
    score = jnp.where(forced, -NEG, jnp.where(causal, imp, NEG))
    return _rank_lt(score, n_slc, k_sel) & causal


def _nsa_attn_kernel(z_ref, tab_ref, cmp_ref, slc_ref, win_ref, cover_ref, o_ref, sel_s,
                     *, tq, tk, n_cmp, n_slc, k_sel):
    qi = pl.program_id(1)
    z = z_ref[0]
    q = z[:, :BRANCH_W] * SCALE
    qr = _rope_heads(q, tab_ref[...])
    heads = [slice(h * HEAD_DIM, (h + 1) * HEAD_DIM) for h in range(N_HEADS)]

    cmpkv = cmp_ref[0]
    ck = cmpkv[:, :HEAD_DIM].astype(BF16)
    cv = cmpkv[:, HEAD_DIM:].astype(BF16)
    t_row = qi * tq + _iota((1, tq), 1)
    n = _iota((cmpkv.shape[0], 1), 0)
    cmask = (n * CMP_STRIDE + (CMP_BLOCK - 1) <= t_row) & (n < n_cmp)
    o_cmp, psum = [], None
    for hs in heads:
        s = jnp.where(cmask, _dot_nt(ck, q[:, hs].astype(BF16)), NEG)
        p = jnp.where(cmask, jnp.exp(s - jnp.max(s, axis=0, keepdims=True)), 0.0)
        p = p / jnp.maximum(jnp.sum(p, axis=0, keepdims=True), 1e-30)
        o_cmp.append(_dot_tn(cv, p.astype(BF16)))
        psum = p if psum is None else psum + p
    p_hi, p_lo = _split(psum)
    imp = _dot(cover_ref[...], p_hi) + _dot(cover_ref[...], p_lo)
    j = _iota(imp.shape, 0)
    own = lax.shift_right_logical(t_row, 6)
    causal = (j * SLC_BLOCK <= t_row) & (j < n_slc)
    forced = (j == 0) | (j == own) | (j == own - 1)
    score = jnp.where(forced, -NEG, jnp.where(causal, imp, NEG))
    picked = _rank_lt_rows(score, n_slc, k_sel) & causal
    sel_s[...] = jnp.where(picked, 0.0, NEG)

    qr_t = qr.T.astype(BF16)
    qrb = [qr_t[hs, :] for hs in heads]
    init = tuple((jnp.full((1, tq), NEG, F32), jnp.zeros((HEAD_DIM + 8, tq), F32)) for _ in heads)
    blocks_per_tile = tk // SLC_BLOCK
    tiles_per_q = tq // tk

    def tile(ref, kt):
        rows = ref[0, pl.ds(pl.multiple_of(kt * tk, tk), tk), :]
        return [rows[:, :HEAD_DIM].astype(BF16)] * N_HEADS, [_with_ones(rows.T[HEAD_DIM:, :].astype(BF16))] * N_HEADS

    def slc_body(kt, carry, diagonal):
        ks, v_ts = tile(slc_ref, kt)
        bias = jnp.concatenate(
            [jnp.broadcast_to(sel_s[pl.ds(kt * blocks_per_tile + u, 1), :], (SLC_BLOCK, tq))
             for u in range(blocks_per_tile)], axis=0)
        valid = (kt * tk + _iota((tk, 1), 0) <= t_row) if diagonal else None
        return _flash_heads_t(qrb, ks, v_ts, [bias] * N_HEADS, valid, carry)

    def win_body(kt, carry, banded):
        ks, v_ts = tile(win_ref, kt)
        d = t_row - (kt * tk + _iota((tk, 1), 0))
        valid = ((d >= 0) & (d <= WINDOW)) if banded else None
        return _flash_heads_t(qrb, ks, v_ts, [None] * N_HEADS, valid, carry)

    first_diag = qi * tiles_per_q
    hi = first_diag + tiles_per_q
    slc = lax.fori_loop(0, first_diag, functools.partial(slc_body, diagonal=False), init)
    slc = lax.fori_loop(first_diag, hi, functools.partial(slc_body, diagonal=True), slc)
    win_lo = jnp.maximum(first_diag - WINDOW // tk, 0)
    win_free = jnp.maximum(first_diag - (WINDOW - tq) // tk, 0)
    win = lax.fori_loop(win_lo, win_free, functools.partial(win_body, banded=True), init)
    win = lax.fori_loop(win_free, first_diag, functools.partial(win_body, banded=False), win)
    win = lax.fori_loop(first_diag, hi, functools.partial(win_body, banded=True), win)

    g = _sigmoid(z[:, 640:640 + LANES]).T
    outs = []
    for h in range(N_HEADS):
        outs.append(g[3 * h:3 * h + 1, :] * o_cmp[h] + g[3 * h + 1:3 * h + 2, :] * _flash_finish(slc[h])
                    + g[3 * h + 2:3 * h + 3, :] * _flash_finish(win[h]))
    o_ref[0] = jnp.concatenate(outs, axis=0).T.astype(o_ref.dtype)


def nsa_attn(z_nsa, tab, cmp_kv, new_slc, new_win, cover, n_cmp, n_slc):
    b, t, w = z_nsa.shape
    tq = _pick_tile(t, 256)
    tk = ATTN_KEY_TILE
    assert tq % tk == 0 and tk % SLC_BLOCK == 0 and WINDOW % tk == 0 and WINDOW >= tq
    kern = functools.partial(_nsa_attn_kernel, tq=tq, tk=tk, n_cmp=n_cmp, n_slc=n_slc,
                             k_sel=min(N_SELECT, n_slc))
    full = lambda i, j: (i, 0, 0)
    const = lambda i, j: (0, 0)
    return pl.pallas_call(
        kern, grid=(b, t // tq),
        in_specs=[pl.BlockSpec((1, tq, w), lambda i, j: (i, j, 0)),
                  pl.BlockSpec((3, tq, LANES), lambda i, j: (0, j, 0)),
                  pl.BlockSpec((1,) + cmp_kv.shape[1:], full),
                  pl.BlockSpec((1, t, LANES), full), pl.BlockSpec((1, t, LANES), full),
                  pl.BlockSpec(cover.shape, const)],
        out_specs=pl.BlockSpec((1, tq, BRANCH_W), lambda i, j: (i, j, 0)),
        out_shape=jax.ShapeDtypeStruct((b, t, BRANCH_W), BF16),
        scratch_shapes=[pltpu.VMEM((cover.shape[0], tq), F32)],
        compiler_params=_params("parallel", "parallel"), name="nsa_attn",
    )(z_nsa, tab, cmp_kv, new_slc, new_win, cover)


def _moba_attn_kernel(z_ref, tab_ref, kv_ref, mean_ref, o_ref, pick_s, *, tq, tk, nb, topk):
    qi = pl.program_id(1)
    q = z_ref[0][:, :BRANCH_W] * SCALE
    qr = _rope_heads(q, tab_ref[...])
    means = mean_ref[0]
    heads = [slice(h * HEAD_DIM, (h + 1) * HEAD_DIM) for h in range(N_HEADS)]

    t_row = qi * tq + _iota((1, tq), 1)
    own = lax.shift_right_logical(t_row, 8)
    past = _iota((means.shape[0], 1), 0) < own
    for h, hs in enumerate(heads):
        gate = _dot3(means[:, hs], qr[:, hs], _dot_nt)
        picked = _rank_lt_rows(jnp.where(past, gate, NEG), nb, topk) & past
        pick_s[h] = jnp.where(picked, 0.0, NEG)

    qr_t = qr.T.astype(BF16)
    qrb = [qr_t[hs, :] for hs in heads]
    init = tuple((jnp.full((1, tq), NEG, F32), jnp.zeros((HEAD_DIM + 8, tq), F32)) for _ in heads)
    tiles_per_block = MOBA_BLOCK // tk

    def body(kt, carry, own_block):
        rows = kv_ref[0, pl.ds(pl.multiple_of(kt * tk, tk), tk), :]
        v_t = rows[:, BRANCH_W:].T.astype(BF16)
        ks = [rows[:, hs].astype(BF16) for hs in heads]
        v_ts = [_with_ones(v_t[hs, :]) for hs in heads]
        if own_block:
            return _flash_heads_t(qrb, ks, v_ts, [None] * N_HEADS, kt * tk + _iota((tk, 1), 0) <= t_row, carry)
        blk = kt // tiles_per_block
        return _flash_heads_t(qrb, ks, v_ts, [pick_s[h, pl.ds(blk, 1), :] for h in range(N_HEADS)], None, carry)

    first_own = qi * (tq // tk)
    res = lax.fori_loop(0, first_own, functools.partial(body, own_block=False), init)
    res = lax.fori_loop(first_own, first_own + tq // tk, functools.partial(body, own_block=True), res)
    o_ref[0] = jnp.concatenate([_flash_finish(r) for r in res], axis=0).T.astype(o_ref.dtype)


def moba_attn(z_moba, tab, new_moba, means):
    b, t, w = z_moba.shape
    assert t % MOBA_BLOCK == 0
    tq = MOBA_BLOCK
    tk = ATTN_KEY_TILE
    assert MOBA_BLOCK % tk == 0
    nb = t // MOBA_BLOCK
    kern = functools.partial(_moba_attn_kernel, tq=tq, tk=tk, nb=nb, topk=min(MOBA_TOPK, nb))
    full = lambda i, j: (i, 0, 0)
    return pl.pallas_call(
        kern, grid=(b, t // tq),
        in_specs=[pl.BlockSpec((1, tq, w), lambda i, j: (i, j, 0)),
                  pl.BlockSpec((3, tq, LANES), lambda i, j: (0, j, 0)),
                  pl.BlockSpec((1, t, 2 * BRANCH_W), full),
                  pl.BlockSpec((1, nb, BRANCH_W), full)],
        out_specs=pl.BlockSpec((1, tq, BRANCH_W), lambda i, j: (i, j, 0)),
        out_shape=jax.ShapeDtypeStruct((b, t, BRANCH_W), BF16),
        scratch_shapes=[pltpu.VMEM((N_HEADS, nb, tq), F32)],
        compiler_params=_params("parallel", "parallel"), name="moba_attn",
    )(z_moba, tab, new_moba, means)


def _cross_attn_kernel(q_ref, mem_ref, o_ref):
    q = q_ref[0] * SCALE
    mem = mem_ref[0]
    outs = []
    for h in range(N_HEADS):
        hs = slice(h * HEAD_DIM, (h + 1) * HEAD_DIM)
        s = _dot_nt(q[:, hs].astype(BF16), mem[:, hs].astype(BF16))
        p = jnp.exp(s - jnp.max(s, axis=-1, keepdims=True))
        p = p / jnp.sum(p, axis=-1, keepdims=True)
        v = mem[:, BRANCH_W + h * HEAD_DIM:BRANCH_W + (h + 1) * HEAD_DIM].astype(BF16)
        outs.append(_dot(p.astype(BF16), v))
    o_ref[0] = jnp.concatenate(outs, axis=1).astype(o_ref.dtype)


def _cross_attn_t_kernel(q_ref, mem_ref, o_ref):
    q_t = (q_ref[0] * SCALE).T.astype(BF16)
    mem = mem_ref[0]
    v_t = mem[:, BRANCH_W:].T.astype(BF16)
    heads = [slice(h * HEAD_DIM, (h + 1) * HEAD_DIM) for h in range(N_HEADS)]
    ss = [_dot(mem[:, hs].astype(BF16), q_t[hs, :]) for hs in heads]
    ps = [jnp.exp(s - jnp.max(s, axis=0, keepdims=True)).astype(BF16) for s in ss]
    accs = [_dot(_with_ones(v_t[hs, :]), p) for hs, p in zip(heads, ps)]
    outs = [a[:HEAD_DIM] / a[HEAD_DIM:HEAD_DIM + 1] for a in accs]
    o_ref[0] = jnp.concatenate(outs, axis=0).T.astype(o_ref.dtype)


def cross_attn(q, mem_kv):
    b, t, w = q.shape
    n_mem = mem_kv.shape[1]
    lanes_ok = t % (2 * LANES) == 0 and n_mem % LANES == 0
    tq = 2 * LANES if lanes_ok else _pick_tile(t, 512)
    return pl.pallas_call(
        _cross_attn_t_kernel if lanes_ok else _cross_attn_kernel, grid=(b, t // tq),
        in_specs=[pl.BlockSpec((1, tq, w), lambda i, j: (i, j, 0)),
                  pl.BlockSpec((1, n_mem, 2 * BRANCH_W), lambda i, j: (i, 0, 0))],
        out_specs=pl.BlockSpec((1, tq, BRANCH_W), lambda i, j: (i, j, 0)),
        out_shape=jax.ShapeDtypeStruct((b, t, BRANCH_W), BF16),
        compiler_params=_params("parallel", "parallel"), name="cross_attn",
    )(q, mem_kv)


def _head_sum(x, bd_ref):
    return _dot(x.astype(BF16), bd_ref[...])


def _rwkv_prep(z, zprev, vec_ref, wa_ref, g2_ref, bd_ref):
    mu = vec_ref[0:1, :]
    zm = z + (zprev - z) * mu
    r = zm[:, 0:256]
    k = zm[:, 256:512]
    v = zm[:, 512:768]
    lora = zm[:, 768:896]
    lora = jnp.where(_iota(lora.shape, 1) < 64, jnp.tanh(lora), lora)
    wa = _dot(lora.astype(BF16), wa_ref[...])
    wl = vec_ref[1:2, 0:256] + wa[:, :256]
    x = -wl
    softplus = jnp.maximum(x, 0.0) + jnp.log(1.0 + jnp.exp(-jnp.abs(x)))
    logdecay = -jnp.exp(-softplus - 0.5)
    a = _sigmoid(vec_ref[1:2, 256:512] + wa[:, 256:])
    g = _dot(_sigmoid(zm[:, 896:1152]).astype(BF16), g2_ref[...])
    kkr = k * vec_ref[2:3, 0:256]
    kk = kkr * lax.rsqrt(jnp.maximum(_head_sum(kkr * kkr, bd_ref), 1e-24))
    k2 = k * (1.0 + (a - 1.0) * vec_ref[3:4, 0:256])
    bonus = _head_sum(r * k2 * vec_ref[4:5, 0:256], bd_ref) * v
    return r, k2, v, kk, kk * a, logdecay, g, bonus


def _rwkv_finish(y, bonus, g, vec_ref, bd_ref):
    mu = _head_sum(y, bd_ref) * (1.0 / HEAD_DIM)
    d = y - mu
    var = _head_sum(d * d, bd_ref) * (1.0 / HEAD_DIM)
    yn = d * lax.rsqrt(var + RWKV_GN_EPS) * vec_ref[5:6, 0:256] + vec_ref[6:7, 0:256]
    return (yn + bonus) * g


def _dotc(a, b, dot=_dot):
    return dot(a.astype(BF16), b.astype(BF16))


def _tri_inv_all(lows):
    c = lows[0].shape[0]
    eye = (_iota((c, c), 0) == _iota((c, c), 1)).astype(F32)
    ts = [eye - low for low in lows]
    lps = [_dotc(low, low) for low in lows]
    n = 2
    while n < c:
        ts = [t + _dotc(t, lp) for t, lp in zip(ts, lps)]
        n *= 2
        if n < c:
            lps = [_dotc(lp, lp) for lp in lps]
    return ts


RWKV_GROUP = 8


def _rwkv_seq_kernel(z_ref, shift_ref, s0_ref, vec_ref, wa_ref, g2_ref, bd_ref, tri_ref,
                     o_ref, s_ref, last_ref, r_s, k_s, v_s, kk_s, b_s, ld_s, y_s, pe_s, *, tt):
    ti = pl.program_id(1)

    @pl.when(ti == 0)
    def _():
        last_ref[...] = shift_ref[0]
        s_ref[0] = s0_ref[0]

    z = z_ref[0]
    rolled = pltpu.roll(z, 1, 0)
    zprev = jnp.where(_iota(z.shape, 0) == 0, last_ref[...], rolled)
    last_ref[...] = z[tt - 1:tt, :]
    r, k2, v, kk, b, ld, g, bonus = _rwkv_prep(z, zprev, vec_ref, wa_ref, g2_ref, bd_ref)
    r_s[...] = r
    k_s[...] = k2
    v_s[...] = v
    kk_s[...] = kk
    b_s[...] = b
    ld_s[...] = ld

    c = RWKV_CHUNK
    ri = _iota((c, c), 0)
    ci = _iota((c, c), 1)
    strict = ri > ci
    incl = ri >= ci

    eye = (ri == ci).astype(F32)
    heads = [slice(h * HEAD_DIM, (h + 1) * HEAD_DIM) for h in range(N_HEADS)]
    n_chunks = tt // c
    group = RWKV_GROUP if n_chunks % RWKV_GROUP == 0 else 1

    def prepare(gi, carry):
        probs = []
        for j in range(group):
            ck = gi * group + j
            rows = pl.ds(pl.multiple_of(ck * c, c), c)
            ldc = ld_s[rows, :]
            cum = _dot3(tri_ref[...], ldc)
            p_in = jnp.exp(cum)
            inv_p = jnp.exp(-cum)
            kkt = kk_s[rows, :] * jnp.exp(cum - ldc)
            bt = b_s[rows, :] * inv_p
            kt = k_s[rows, :] * inv_p
            rt = r_s[rows, :] * p_in
            vv = v_s[rows, :]
            p_end = p_in[c - 1:c, :]
            pe_s[pl.ds(ck, 1), :] = p_end
            for hs in heads:
                probs.append(dict(kkt=kkt[:, hs], bt=bt[:, hs], kt=kt[:, hs], rt=rt[:, hs], v=vv[:, hs],
                                  p_end=p_end[:, hs]))
        blocks = [_dotc(jnp.concatenate([p['kkt'], p['rt']], axis=0),
                        jnp.concatenate([p['bt'], p['kt']], axis=0), _dot_nt) for p in probs]
        lows = [jnp.where(strict, bl[:c, :c], 0.0) for bl in blocks]
        gvs = [_dotc(jnp.where(strict, bl[:c, c:], 0.0), p['v']) for bl, p in zip(blocks, probs)]
        tinvs = _tri_inv_all(lows)
        ke_ue = [_dotc(t, jnp.concatenate([p['kkt'], gv], axis=1)) for t, p, gv in zip(tinvs, probs, gvs)]
        corr = [_dotc(jnp.where(incl, bl[c:, :c], 0.0), ku) for bl, ku in zip(blocks, ke_ue)]
        avs = [_dotc(jnp.where(incl, bl[c:, c:], 0.0), p['v']) for bl, p in zip(blocks, probs)]
        xps = [_dotc(ku[:, :HEAD_DIM], p['bt'], _dot_tn) * p['p_end'] for ku, p in zip(ke_ue, probs)]
        bcs = [_dotc(jnp.concatenate([p['v'], ku[:, HEAD_DIM:]], axis=0),
                     jnp.concatenate([p['kt'], -p['bt']], axis=0), _dot_tn) * p['p_end']
               for ku, p in zip(ke_ue, probs)]
        res = [p['rt'] - cr[:, :HEAD_DIM] for p, cr in zip(probs, corr)]
        y0s = [av - cr[:, HEAD_DIM:] for av, cr in zip(avs, corr)]
        for j in range(group):
            rows = pl.ds(pl.multiple_of((gi * group + j) * c, c), c)
            sl = slice(j * N_HEADS, (j + 1) * N_HEADS)
            r_s[rows, :] = jnp.concatenate(res[sl], axis=1)
            v_s[rows, :] = jnp.concatenate(y0s[sl], axis=1)
            kk_s[rows, :] = jnp.concatenate(xps[sl], axis=1)
            b_s[rows, :] = jnp.concatenate(bcs[sl], axis=1)
        return carry

    lax.fori_loop(0, n_chunks // group, prepare, 0)

    def advance(ck, carry):
        rows = pl.ds(pl.multiple_of(ck * c, c), c)
        re, y0, xp, bc = r_s[rows, :], v_s[rows, :], kk_s[rows, :], b_s[rows, :]
        p_end = pe_s[pl.ds(ck, 1), :]
        states = [s_ref[0, h] for h in range(N_HEADS)]
        ys = [_dotc(re[:, hs], s_h, _dot_nt) + y0[:, hs] for hs, s_h in zip(heads, states)]
        for h, (hs, s_h) in enumerate(zip(heads, states)):
            s_ref[0, h] = s_h * p_end[:, hs] - _dotc(s_h, xp[:, hs]) + bc[:, hs]
        y_s[rows, :] = jnp.concatenate(ys, axis=1)
        return carry

    lax.fori_loop(0, n_chunks, advance, 0, unroll=True)
    o_ref[0] = _rwkv_finish(y_s[...], bonus, g, vec_ref, bd_ref).astype(o_ref.dtype)


def rwkv_seq(z, shift, s0, vec, wa, g2, bd, tri):
    b, t, w = z.shape
    tt = _pick_tile(t, 512, RWKV_CHUNK)
    assert tt % RWKV_CHUNK == 0
    const = lambda i, j: (0, 0)
    scr = pltpu.VMEM((tt, BRANCH_W), F32)
    return pl.pallas_call(
        functools.partial(_rwkv_seq_kernel, tt=tt), grid=(b, t // tt),
        in_specs=[pl.BlockSpec((1, tt, w), lambda i, j: (i, j, 0)),
                  pl.BlockSpec((1, 1, w), lambda i, j: (i, 0, 0)),
                  pl.BlockSpec((1, N_HEADS, HEAD_DIM, HEAD_DIM), lambda i, j: (i, 0, 0, 0)),
                  pl.BlockSpec(vec.shape, const), pl.BlockSpec(wa.shape, const),
                  pl.BlockSpec(g2.shape, const), pl.BlockSpec(bd.shape, const),
                  pl.BlockSpec(tri.shape, const)],
        out_specs=[pl.BlockSpec((1, tt, BRANCH_W), lambda i, j: (i, j, 0)),
                   pl.BlockSpec((1, N_HEADS, HEAD_DIM, HEAD_DIM), lambda i, j: (i, 0, 0, 0))],
        out_shape=[jax.ShapeDtypeStruct((b, t, BRANCH_W), BF16),
                   jax.ShapeDtypeStruct((b, N_HEADS, HEAD_DIM, HEAD_DIM), F32)],
        scratch_shapes=[pltpu.VMEM((1, w), F32)] + [scr] * 7 + [pltpu.VMEM((tt // RWKV_CHUNK, BRANCH_W), F32)],
        compiler_params=_params("parallel", "arbitrary"), name="rwkv_seq",
    )(z, shift, s0, vec, wa, g2, bd, tri)


def _rwkv_step_kernel(z_ref, shift_ref, s0_ref, vec_ref, wa_ref, g2_ref, bd_ref, o_ref, s_ref):
    rows = 8
    z = jnp.broadcast_to(z_ref[0], (rows, z_ref.shape[2]))
    zprev = jnp.broadcast_to(shift_ref[0], z.shape)
    r, k2, v, kk, b, ld, g, bonus = _rwkv_prep(z, zprev, vec_ref, wa_ref, g2_ref, bd_ref)
    decay = jnp.exp(ld)
    eye = _iota((HEAD_DIM, HEAD_DIM), 0) == _iota((HEAD_DIM, HEAD_DIM), 1)
    ys = []
    for h in range(N_HEADS):
        hs = slice(h * HEAD_DIM, (h + 1) * HEAD_DIM)
        s_h = s0_ref[0, h]
        sk = jnp.sum(s_h * kk[0:1, hs], axis=1, keepdims=True)
        v_col = jnp.sum(jnp.where(eye, v[0:1, hs], 0.0), axis=1, keepdims=True)
        s_new = s_h * decay[0:1, hs] - sk * b[0:1, hs] + v_col * k2[0:1, hs]
        s_ref[0, h] = s_new
        y_col = jnp.sum(s_new * r[0:1, hs], axis=1, keepdims=True)
        ys.append(jnp.sum(jnp.where(eye, y_col, 0.0), axis=0, keepdims=True))
    y = jnp.broadcast_to(jnp.concatenate(ys, axis=1), (rows, BRANCH_W))
    o_ref[0] = _rwkv_finish(y, bonus, g, vec_ref, bd_ref)[0:1].astype(o_ref.dtype)


def rwkv_step(z, shift, s0, vec, wa, g2, bd):
    b, _, w = z.shape
    const = lambda i: (0, 0)
    return pl.pallas_call(
        _rwkv_step_kernel, grid=(b,),
        in_specs=[pl.BlockSpec((1, 1, w), lambda i: (i, 0, 0)),
                  pl.BlockSpec((1, 1, w), lambda i: (i, 0, 0)),
                  pl.BlockSpec((1, N_HEADS, HEAD_DIM, HEAD_DIM), lambda i: (i, 0, 0, 0)),
                  pl.BlockSpec(vec.shape, const), pl.BlockSpec(wa.shape, const),
                  pl.BlockSpec(g2.shape, const), pl.BlockSpec(bd.shape, const)],
        out_specs=[pl.BlockSpec((1, 1, BRANCH_W), lambda i: (i, 0, 0)),
                   pl.BlockSpec((1, N_HEADS, HEAD_DIM, HEAD_DIM), lambda i: (i, 0, 0, 0))],
        out_shape=[jax.ShapeDtypeStruct((b, 1, BRANCH_W), BF16),
                   jax.ShapeDtypeStruct((b, N_HEADS, HEAD_DIM, HEAD_DIM), F32)],
        compiler_params=_params("parallel"), name="rwkv_step",
    )(z, shift, s0, vec, wa, g2, bd)


def _stack_heads(x):
    rows = [x[:, h * HEAD_DIM:(h + 1) * HEAD_DIM] for h in range(N_HEADS)]
    return jnp.concatenate(rows + [jnp.zeros((8 - N_HEADS, HEAD_DIM), x.dtype)], axis=0)


def _column(row):
    n = row.shape[1]
    eye = _iota((n, n), 0) == _iota((n, n), 1)
    return jnp.sum(jnp.where(eye, row, 0.0), axis=1, keepdims=True)


def _to_row(col):
    n = col.shape[0]
    eye = _iota((n, n), 0) == _iota((n, n), 1)
    return jnp.sum(jnp.where(eye, col, 0.0), axis=0, keepdims=True)


def _page_scores(k_page, q_cols, s_ref, n_pages):
    def body(p, c):
        for h, qc in enumerate(q_cols):
            s_ref[h, pl.ds(p, 1), :] = jnp.sum(k_page(p, h) * qc, axis=0, keepdims=True)
        return c

    lax.fori_loop(0, n_pages, body, 0, unroll=4)


def _page_values(v_page, p_ref, n_pages):
    def body(p, accs):
        return tuple(acc + v_page(p, h) * p_ref[h, pl.ds(p, 1), :] for h, acc in enumerate(accs))

    init = tuple(jnp.zeros((HEAD_DIM, p_ref.shape[2]), F32) for _ in range(N_HEADS))
    return [jnp.sum(a, axis=1, keepdims=True) for a in lax.fori_loop(0, n_pages, body, init, unroll=4)]


def _softmax_with_new(s, mask, s_new):
    s = jnp.where(mask, s, NEG)
    m = jnp.maximum(jnp.max(jnp.max(s, axis=1, keepdims=True), axis=0, keepdims=True), s_new)
    p = jnp.where(mask, jnp.exp(s - m), 0.0)
    p_new = jnp.exp(s_new - m)
    den = jnp.maximum(jnp.sum(jnp.sum(p, axis=1, keepdims=True), axis=0, keepdims=True) + p_new, 1e-30)
    return p / den, p_new / den


def _nsa_sample_kernel(pt_ref, z_ref, tab_ref, win_ref, w1_ref, pe_ref, w2_ref, cover_ref, pj_ref, hr_ref,
                       cmp_hbm, slc_hbm, o_ref, slc_row_ref, win_row_ref,
                       cbuf, sbuf, rows_s, lines_s, s_s, p_s, sem,
                       *, base, n_pages, page, past, n_cmp, n_slc, k_sel):
    bi = pl.program_id(0)
    n_lines = past // CMP_STRIDE
    slot = lax.rem(bi, 2)

    def copies(seq, sl, p):
        pg = pt_ref[seq * n_pages + p] + base
        return (pltpu.make_async_copy(cmp_hbm.at[pg], cbuf.at[sl, p], sem.at[sl, 0]),
                pltpu.make_async_copy(slc_hbm.at[pg], sbuf.at[sl, p], sem.at[sl, 1]))

    def start_fetch(seq, sl):
        for p in range(n_pages):
            for cp in copies(seq, sl, p):
                cp.start()

    @pl.when(bi == 0)
    def _():
        start_fetch(0, 0)

    @pl.when(bi + 1 < pl.num_programs(0))
    def _():
        start_fetch(bi + 1, 1 - slot)

    z = z_ref[0]
    tab = tab_ref[...]
    q = z[:, :BRANCH_W] * SCALE
    q4 = _stack_heads(q)
    q4r_f32 = _stack_heads(_rope_heads(q, tab))
    q4r = q4r_f32.astype(BF16)
    slc_row = _rope(z[:, 384:512], tab, half_only=True)
    win_row = _rope(z[:, 512:640], tab, half_only=True)
    slc_row_ref[0] = slc_row
    win_row_ref[0] = win_row

    for p in range(n_pages):
        for cp in copies(bi, slot, p):
            cp.wait()

    def xpose(p, c):
        rows_s[pl.ds(pl.multiple_of(p * page, page), page), :] = cbuf[slot, p].T
        return c

    lax.fori_loop(0, n_pages, xpose, 0, unroll=8)
    first = _iota((n_lines, LANES), 1) < HEAD_DIM
    for i in range(CMP_STRIDE // 2):
        a = rows_s[pl.ds(2 * i, n_lines, stride=CMP_STRIDE), :]
        b = rows_s[pl.ds(2 * i + 1, n_lines, stride=CMP_STRIDE), :]
        lines_s[0, :, i * LANES:(i + 1) * LANES] = jnp.where(first, a, pltpu.roll(b, HEAD_DIM, 1))
        lines_s[1, :, i * LANES:(i + 1) * LANES] = jnp.where(first, pltpu.roll(a, HEAD_DIM, 1), b)

    cmpkv = _compress((lines_s[0], lines_s[1]), w1_ref, pe_ref, w2_ref)
    ck = cmpkv[:, :HEAD_DIM].astype(BF16)
    cv = cmpkv[:, HEAD_DIM:].astype(BF16)
    n = _iota((1, cmpkv.shape[0]), 1)
    cmask = (n * CMP_STRIDE + (CMP_BLOCK - 1) <= past) & (n < n_cmp)
    p_cmp = _msoftmax(_dot_nt(q4.astype(BF16), ck), cmask)
    o_cmp = _dot(p_cmp.astype(BF16), cv)
    psum = jnp.sum(jnp.where(_iota(p_cmp.shape, 0) < N_HEADS, p_cmp, 0.0), axis=0, keepdims=True)
    imp = _dot2_exact_rhs(jnp.broadcast_to(psum, p_cmp.shape), cover_ref[...])[0:1]
    t_pos = jnp.full((1, 1), past, jnp.int32)
    sel = _select_blocks(imp, t_pos, n_slc, k_sel)

    def with_new_key(s_past, mask, new_row, v_past):
        s_new = jnp.sum(q4r_f32 * new_row[:, :HEAD_DIM], axis=1, keepdims=True)
        s_past = jnp.where(mask, s_past, NEG)
        m = jnp.maximum(jnp.max(s_past, axis=1, keepdims=True), s_new)
        p_past = jnp.where(mask, jnp.exp(s_past - m), 0.0)
        p_new = jnp.exp(s_new - m)
        den = jnp.maximum(jnp.sum(p_past, axis=1, keepdims=True) + p_new, 1e-30)
        return _dot((p_past / den).astype(BF16), v_past) + (p_new / den) * new_row[:, HEAD_DIM:]

    sel_pages = _dot((pj_ref[...] * sel.astype(F32)).astype(BF16), hr_ref[...]) > 0.5
    _page_scores(lambda p, h: sbuf[slot, p, 0], [_column(q4r_f32[h:h + 1, :]) for h in range(N_HEADS)], s_s,
                 n_pages)
    p_new = []
    for h in range(N_HEADS):
        s_new = jnp.sum(q4r_f32[h:h + 1, :] * slc_row[:, :HEAD_DIM], axis=1, keepdims=True)
        p_s[h], pn = _softmax_with_new(s_s[h], sel_pages, s_new)
        p_new.append(pn)
    o_slc = [_to_row(col) + p_new[h] * slc_row[:, HEAD_DIM:]
             for h, col in enumerate(_page_values(lambda p, h: sbuf[slot, p, 1], p_s, n_pages))]
    wrows = win_ref[0]
    wk = wrows[:, :HEAD_DIM].astype(BF16)
    wv = wrows[:, HEAD_DIM:].astype(BF16)
    o_win = with_new_key(_dot_nt(q4r, wk), jnp.full((1, wrows.shape[0]), True), win_row, wv)

    g = _sigmoid(z[:, 640:640 + 3 * N_HEADS])
    outs = []
    for h in range(N_HEADS):
        outs.append(g[:, 3 * h:3 * h + 1] * o_cmp[h:h + 1] + g[:, 3 * h + 1:3 * h + 2] * o_slc[h]
                    + g[:, 3 * h + 2:3 * h + 3] * o_win[h:h + 1])
    o_ref[0] = jnp.concatenate(outs, axis=1).astype(o_ref.dtype)


def nsa_sample(page_table, z_nsa, tab, win, w1c, pe2, w2c, cover, pj, hr, cache_cmp, cache_slc,
               layer, n_pool, page, n_cmp, n_slc):
    b = z_nsa.shape[0]
    n_pages = page_table.shape[1]
    past = n_pages * page
    kern = functools.partial(_nsa_sample_kernel, base=layer * n_pool, n_pages=n_pages, page=page, past=past,
                             n_cmp=n_cmp, n_slc=n_slc, k_sel=min(N_SELECT, n_slc))
    const = lambda i, pt: (0, 0)
    row = lambda i, pt: (i, 0, 0)
    lay = lambda i, pt: (layer, 0, 0, 0)
    grid_spec = pltpu.PrefetchScalarGridSpec(
        num_scalar_prefetch=1, grid=(b,),
        in_specs=[pl.BlockSpec((1, 1, z_nsa.shape[2]), row),
                  pl.BlockSpec(tab.shape, lambda i, pt: (0, 0, 0)),
                  pl.BlockSpec((1,) + win.shape[1:], row),
                  pl.BlockSpec((None,) + w1c.shape[1:], lay), pl.BlockSpec((None,) + pe2.shape[1:], lay),
                  pl.BlockSpec((None,) + w2c.shape[1:], lay), pl.BlockSpec(cover.shape, const),
                  pl.BlockSpec(pj.shape, const), pl.BlockSpec(hr.shape, const),
                  pl.BlockSpec(memory_space=pl.ANY), pl.BlockSpec(memory_space=pl.ANY)],
        out_specs=[pl.BlockSpec((1, 1, BRANCH_W), row), pl.BlockSpec((1, 1, LANES), row),
                   pl.BlockSpec((1, 1, LANES), row)],
        scratch_shapes=[pltpu.VMEM((2, n_pages, 2 * HEAD_DIM, page), F32),
                        pltpu.VMEM((2, n_pages, 2, HEAD_DIM, page), F32),
                        pltpu.VMEM((past, 2 * HEAD_DIM), F32),
                        pltpu.VMEM((2, past // CMP_STRIDE, CMP_STRIDE * HEAD_DIM), F32),
                        pltpu.VMEM((N_HEADS, n_pages, page), F32),
                        pltpu.VMEM((N_HEADS, n_pages, page), F32),
                        pltpu.SemaphoreType.DMA((2, 2))])
    return pl.pallas_call(
        kern, grid_spec=grid_spec,
        out_shape=[jax.ShapeDtypeStruct((b, 1, BRANCH_W), BF16),
                   jax.ShapeDtypeStruct((b, 1, LANES), F32), jax.ShapeDtypeStruct((b, 1, LANES), F32)],
        compiler_params=_params("arbitrary"), name="nsa_sample",
    )(page_table.reshape(-1), z_nsa, tab, win, w1c, pe2, w2c, cover, pj, hr, cache_cmp, cache_slc)


def _moba_sample_kernel(pt_ref, z_ref, tab_ref, kv_hbm, o_ref, row_ref, buf, s_s, p_s, sem,
                        *, base, n_pages, pages_per_block, topk):
    bi = pl.program_id(0)
    slot = lax.rem(bi, 2)

    def copy(seq, sl, p):
        pg = pt_ref[seq * n_pages + p] + base
        return pltpu.make_async_copy(kv_hbm.at[pg], buf.at[sl, p], sem.at[sl])

    @pl.when(bi == 0)
    def _():
        for p in range(n_pages):
            copy(0, 0, p).start()

    @pl.when(bi + 1 < pl.num_programs(0))
    def _():
        for p in range(n_pages):
            copy(bi + 1, 1 - slot, p).start()

    z = z_ref[0]
    tab = tab_ref[...]
    q = z[:, :BRANCH_W] * SCALE
    qr = _rope_heads(q, tab)
    k_new = _rope_heads(z[:, BRANCH_W:2 * BRANCH_W], tab)
    v_new = z[:, 2 * BRANCH_W:3 * BRANCH_W]
    row_ref[0] = jnp.concatenate([k_new, v_new], axis=1)

    for p in range(n_pages):
        copy(bi, slot, p).wait()

    heads = [slice(h * HEAD_DIM, (h + 1) * HEAD_DIM) for h in range(N_HEADS)]
    _page_scores(lambda p, h: buf[slot, p, h], [_column(qr[:, hs]) for hs in heads], s_s, n_pages)
    pi = _iota((n_pages, n_pages), 0)
    pj = _iota((n_pages, n_pages), 1)
    shift = pages_per_block.bit_length() - 1
    same_block = lax.shift_right_logical(pi, shift) == lax.shift_right_logical(pj, shift)
    p_new = []
    for h, hs in enumerate(heads):
        s = s_s[h]
        page_sum = jnp.sum(s, axis=1, keepdims=True)
        page_sum_row = jnp.sum(jnp.where(pi == pj, page_sum, 0.0), axis=0, keepdims=True)
        gate_col = jnp.sum(jnp.where(same_block, page_sum_row, 0.0), axis=1, keepdims=True)
        gate_row = jnp.sum(jnp.where(same_block, page_sum, 0.0), axis=0, keepdims=True)
        beats = ((gate_row > gate_col) | ((gate_row == gate_col) & (pj < pi))) & jnp.logical_not(same_block)
        picked = jnp.sum(beats.astype(F32), axis=1, keepdims=True) < pages_per_block * topk - 0.5
        s_new = jnp.sum(qr[:, hs] * k_new[:, hs], axis=1, keepdims=True)
        p_s[h], pn = _softmax_with_new(s, picked, s_new)
        p_new.append(pn)
    cols = _page_values(lambda p, h: buf[slot, p, N_HEADS + h], p_s, n_pages)
    outs = [_to_row(cols[h]) + p_new[h] * v_new[:, hs] for h, hs in enumerate(heads)]
    o_ref[0] = jnp.concatenate(outs, axis=1).astype(o_ref.dtype)


def moba_sample(page_table, z_moba, tab, cache, layer, n_pool, page):
    b = z_moba.shape[0]
    n_pages = page_table.shape[1]
    past = n_pages * page
    assert past % MOBA_BLOCK == 0 and MOBA_BLOCK % page == 0
    ppb = MOBA_BLOCK // page
    assert ppb & (ppb - 1) == 0
    kern = functools.partial(_moba_sample_kernel, base=layer * n_pool, n_pages=n_pages, pages_per_block=ppb,
                             topk=min(MOBA_TOPK, past // MOBA_BLOCK + 1))
    row = lambda i, pt: (i, 0, 0)
    grid_spec = pltpu.PrefetchScalarGridSpec(
        num_scalar_prefetch=1, grid=(b,),
        in_specs=[pl.BlockSpec((1, 1, z_moba.shape[2]), row),
                  pl.BlockSpec(tab.shape, lambda i, pt: (0, 0, 0)),
                  pl.BlockSpec(memory_space=pl.ANY)],
        out_specs=[pl.BlockSpec((1, 1, BRANCH_W), row), pl.BlockSpec((1, 1, 2 * BRANCH_W), row)],
        scratch_shapes=[pltpu.VMEM((2, n_pages, 2 * N_HEADS, HEAD_DIM, page), F32),
                        pltpu.VMEM((N_HEADS, n_pages, page), F32),
                        pltpu.VMEM((N_HEADS, n_pages, page), F32),
                        pltpu.SemaphoreType.DMA((2,))])
    return pl.pallas_call(
        kern, grid_spec=grid_spec,
        out_shape=[jax.ShapeDtypeStruct((b, 1, BRANCH_W), BF16),
                   jax.ShapeDtypeStruct((b, 1, 2 * BRANCH_W), F32)],
        compiler_params=_params("arbitrary"), name="moba_sample",
    )(page_table.reshape(-1), z_moba, tab, cache)


def _rope_table(pos):
    inv = ROPE_THETA ** (-jnp.arange(ROT_HALF, dtype=F32) / ROT_HALF)
    ang = pos.astype(F32)[:, None] * inv[None, :]
    cos, sin = jnp.cos(ang), jnp.sin(ang)
    t = pos.shape[0]
    ones = jnp.ones((t, HEAD_DIM - 2 * ROT_HALF), F32)
    zeros = jnp.zeros((t, HEAD_DIM - 2 * ROT_HALF), F32)
    z8 = jnp.zeros((t, ROT_HALF), F32)
    c = jnp.concatenate([cos, cos, ones], axis=1)
    sa = jnp.concatenate([-sin, z8, zeros], axis=1)
    sb = jnp.concatenate([z8, sin, zeros], axis=1)
    return jnp.stack([jnp.tile(c, (1, 2)), jnp.tile(sa, (1, 2)), jnp.tile(sb, (1, 2))], axis=0)


def _cover_matrix(n_rows, n_cmp, n_slc, width):
    starts = np.arange(n_rows) * CMP_STRIDE
    blk = np.arange(width) * SLC_BLOCK
    cover = (starts[:, None] < blk[None, :] + SLC_BLOCK) & (starts[:, None] + CMP_BLOCK > blk[None, :])
    cover &= (np.arange(n_rows)[:, None] < n_cmp) & (np.arange(width)[None, :] < n_slc)
    return jnp.asarray(cover, BF16)


def _round_up(n, m):
    return -(-n // m) * m


COL_NSA, COL_MOBA, COL_CROSS, COL_RWKV, COL_MERGE = 0, 768, 1536, 2304, 4096


def _pack_w_in(w_in):
    def zeros(n):
        return jnp.zeros(w_in.shape[:2] + (n,), w_in.dtype)

    parts = [w_in[..., 0:652], zeros(NSA_PAD - 652), w_in[..., 1708:2476], w_in[..., 2476:2732],
             zeros(COL_RWKV - COL_CROSS - BRANCH_W), w_in[..., 652:1708], zeros(RWKV_PAD - RWKV_IN),
             zeros(COL_MERGE - COL_RWKV - RWKV_PAD), w_in[..., 2732:]]
    return jnp.concatenate(parts, axis=-1).astype(BF16)


def _compress_weights(nsa_pe, nsa_phi_w1, nsa_phi_w2):
    depth, _, _, hidden = nsa_phi_w1.shape
    line = CMP_STRIDE * HEAD_DIM
    w1 = nsa_phi_w1.reshape(depth, 2, 2, line, hidden)
    w1 = jnp.transpose(w1, (0, 1, 3, 2, 4)).reshape(depth, 2, line, 2 * hidden).astype(BF16)
    pe = nsa_pe.reshape(depth, 2, CMP_STRIDE, 2, HEAD_DIM)
    pe = jnp.transpose(pe, (0, 3, 1, 2, 4)).reshape(depth, 2, 2, line)
    return w1, jnp.pad(pe, ((0, 0), (0, 0), (0, 6), (0, 0))), nsa_phi_w2.astype(BF16)


def _layer_weights(l, rwkv_mu, rwkv_w0, rwkv_w2, rwkv_a0, rwkv_a2,
                   rwkv_g2, rwkv_k_k, rwkv_k_a, rwkv_r_k, rwkv_ln_w, rwkv_ln_b):
    w = {}
    vec = jnp.zeros((8, RWKV_PAD), F32)
    vec = vec.at[0, :RWKV_IN].set(rwkv_mu[l])
    vec = vec.at[1, 0:256].set(rwkv_w0[l]).at[1, 256:512].set(rwkv_a0[l])
    for i, p in enumerate((rwkv_k_k, rwkv_k_a, rwkv_r_k, rwkv_ln_w, rwkv_ln_b)):
        vec = vec.at[2 + i, 0:256].set(p[l])
    w['rwkv_vec'] = vec
    wa = jnp.zeros((128, 512), F32).at[0:64, 0:256].set(rwkv_w2[l]).at[64:128, 256:512].set(rwkv_a2[l])
    w['rwkv_wa'] = wa.astype(BF16)
    w['rwkv_g2'] = jnp.pad(rwkv_g2[l], ((0, 256 - rwkv_g2.shape[1]), (0, 0))).astype(BF16)
    return w


def _project(h, w_all, l, d):
    z_nsa = mm(h, w_all, l, COL_NSA, NSA_PAD)
    z_rwkv = mm(h, w_all, l, COL_RWKV, RWKV_PAD)
    z_moba = mm(h, w_all, l, COL_MOBA, 3 * BRANCH_W)
    z_cross = mm(h, w_all, l, COL_CROSS, BRANCH_W)
    gates = mm(h, w_all, l, COL_MERGE, 4 * d, out_dtype=BF16, gate=True)
    return z_nsa, z_rwkv, z_moba, z_cross, gates


def _mixers_prompt(h, w_all, l, w, tab, mem_kv, consts, b, t):
    z_nsa, z_rwkv, z_moba, z_cross, gates = _project(h, w_all, l, h.shape[1])
    z_nsa = z_nsa.reshape(b, t, NSA_PAD)
    z_rwkv = z_rwkv.reshape(b, t, RWKV_PAD)
    z_moba = z_moba.reshape(b, t, 3 * BRANCH_W)
    z_cross = z_cross.reshape(b, t, BRANCH_W)

    new_cmp = z_nsa[:, :, 256:384]
    new_slc, new_win = nsa_rope(z_nsa, tab)
    lines = [new_cmp[:, :, j * HEAD_DIM:(j + 1) * HEAD_DIM].reshape(b, t // CMP_STRIDE, CMP_STRIDE * HEAD_DIM)
             for j in range(2)]
    cmp_kv = compress(lines[0], lines[1], *consts['cmp_w'], l)
    o_nsa = nsa_attn(z_nsa, tab, cmp_kv, new_slc, new_win, consts['cover_p'], consts['n_cmp_p'],
                     consts['n_slc_p'])

    zero_shift = jnp.zeros((b, 1, RWKV_PAD), F32)
    zero_state = jnp.zeros((b, N_HEADS, HEAD_DIM, HEAD_DIM), F32)
    o_rwkv, new_state = rwkv_seq(z_rwkv, zero_shift, zero_state, w['rwkv_vec'], w['rwkv_wa'], w['rwkv_g2'],
                                 consts['bd'], consts['tri'])
    new_shift = z_rwkv[:, t - 1, :RWKV_IN]

    new_moba, means = moba_rope(z_moba, tab)
    o_moba = moba_attn(z_moba, tab, new_moba, means.reshape(b, -1, BRANCH_W))
    o_cross = cross_attn(z_cross, mem_kv)

    outs = [o.reshape(b * t, BRANCH_W) for o in (o_nsa, o_rwkv, o_moba, o_cross)]
    win_keep = min(WINDOW, t)
    caches = (new_cmp.reshape(b, t, 2, HEAD_DIM), new_slc.reshape(b, t, 2, HEAD_DIM),
              new_win[:, t - win_keep:].reshape(b, win_keep, 2, HEAD_DIM),
              new_moba.reshape(b, t, 2, N_HEADS, HEAD_DIM), new_state, new_shift)
    return outs, gates, caches


def _mixers_sample(h, w_all, l, w, tab, page_table, caches_in, consts, b):
    cmp_pages, slc_pages, cache_win, moba_pages, cache_mem, state_rwkv, state_shift, n_pool, page = caches_in
    z_nsa, z_rwkv, z_moba, z_cross, gates = _project(h, w_all, l, h.shape[1])
    z_nsa = z_nsa.reshape(b, 1, NSA_PAD)
    z_rwkv = z_rwkv.reshape(b, 1, RWKV_PAD)
    z_moba = z_moba.reshape(b, 1, 3 * BRANCH_W)
    z_cross = z_cross.reshape(b, 1, BRANCH_W)

    win = cache_win[l].reshape(b, -1, LANES)
    o_nsa, slc_row, win_row = nsa_sample(
        page_table, z_nsa, tab, win, *consts['cmp_w'], consts['cover_s'], consts['pj_s'],
        consts['hr_s'], cmp_pages, slc_pages, l, n_pool, page, consts['n_cmp_s'], consts['n_slc_s'])
    new_cmp = z_nsa[:, :, 256:384]
    full_win = jnp.concatenate([win, win_row], axis=1)
    keep = min(WINDOW, full_win.shape[1])
    new_buf = full_win[:, full_win.shape[1] - keep:]

    shift = jnp.pad(state_shift[l], ((0, 0), (0, RWKV_PAD - RWKV_IN))).reshape(b, 1, RWKV_PAD)
    o_rwkv, new_state = rwkv_step(z_rwkv, shift, state_rwkv[l], w['rwkv_vec'], w['rwkv_wa'], w['rwkv_g2'],
                                  consts['bd'])
    new_shift = z_rwkv[:, 0, :RWKV_IN]

    o_moba, moba_row = moba_sample(page_table, z_moba, tab, moba_pages, l, n_pool, page)
    o_cross = cross_attn(z_cross, cache_mem[l].reshape(b, -1, 2 * BRANCH_W))

    outs = [o.reshape(b, BRANCH_W) for o in (o_nsa, o_rwkv, o_moba, o_cross)]
    caches = (new_cmp.reshape(b, 1, 2, HEAD_DIM), slc_row.reshape(b, 1, 2, HEAD_DIM),
              new_buf.reshape(b, keep, 2, HEAD_DIM), moba_row.reshape(b, 1, 2, N_HEADS, HEAD_DIM),
              new_state, new_shift)
    return outs, gates, caches


def kernel(x_prompt, x_sample, cache_nsa_cmp, cache_nsa_slc, cache_nsa_win, cache_moba, cache_mem, state_rwkv, state_shift, page_table, mem_prompt, g_pre_mix, g_post_mix, g_pre_ffn, g_post_ffn, g_mem, w_in, nsa_pe, nsa_phi_w1, nsa_phi_w2, rwkv_mu, rwkv_w0, rwkv_w2, rwkv_a0, rwkv_a2, rwkv_g2, rwkv_k_k, rwkv_k_a, rwkv_r_k, rwkv_ln_w, rwkv_ln_b, w_mem_kv, w_branch, w_out, w_ffn_gate, w_ffn_up, w_ffn_down):
    bp, t, d = x_prompt.shape
    bs = x_sample.shape[0]
    assert x_sample.shape[1] == 1
    depth = w_in.shape[0]
    page = cache_nsa_cmp.shape[2]
    past = page_table.shape[1] * page
    n_mem = mem_prompt.shape[1]
    assert t % MOBA_BLOCK == 0 and past % MOBA_BLOCK == 0 and page % SLC_BLOCK == 0

    n_cmp_p = (t - CMP_BLOCK) // CMP_STRIDE + 1
    n_slc_p = -(-t // SLC_BLOCK)
    n_cmp_s = (past + 1 - CMP_BLOCK) // CMP_STRIDE + 1
    n_slc_s = -(-(past + 1) // SLC_BLOCK)
    consts = {
        'cmp_w': _compress_weights(nsa_pe, nsa_phi_w1, nsa_phi_w2),
        'n_cmp_p': n_cmp_p, 'n_slc_p': n_slc_p, 'n_cmp_s': n_cmp_s, 'n_slc_s': n_slc_s,
        'cover_p': _cover_matrix(t // CMP_STRIDE, n_cmp_p, n_slc_p, _round_up(n_slc_p, 8)).T,
        'cover_s': _cover_matrix(past // CMP_STRIDE, n_cmp_s, n_slc_s, _round_up(n_slc_s, LANES)),
        'pj_s': jnp.asarray(np.arange(_round_up(n_slc_s, LANES))[None, :] // (page // SLC_BLOCK)
                            == np.arange(past // page)[:, None], F32),
        'hr_s': jnp.asarray(np.arange(_round_up(n_slc_s, LANES))[:, None] % (page // SLC_BLOCK)
                            == np.arange(page)[None, :] // SLC_BLOCK, BF16),
        'bd': jnp.asarray(np.arange(BRANCH_W)[:, None] // HEAD_DIM == np.arange(BRANCH_W)[None, :] // HEAD_DIM, BF16),
        'tri': jnp.asarray(np.tril(np.ones((RWKV_CHUNK, RWKV_CHUNK), np.float32))),
    }
    tab_p = _rope_table(jnp.arange(t, dtype=jnp.int32))
    tab_s = _rope_table(jnp.full((1,), past, jnp.int32))

    xp = x_prompt.reshape(bp * t, d)
    xs = x_sample.reshape(bs, d)
    mem_flat = mem_prompt.reshape(bp * n_mem, d)
    n_pool = cache_nsa_cmp.shape[1]
    cmp_pages = jnp.transpose(cache_nsa_cmp, (0, 1, 3, 4, 2)).reshape(depth * n_pool, 2 * HEAD_DIM, page)
    slc_pages = jnp.transpose(cache_nsa_slc, (0, 1, 3, 4, 2)).reshape(depth * n_pool, 2, HEAD_DIM, page)
    moba_pages = jnp.transpose(cache_moba, (0, 1, 3, 4, 5, 2)).reshape(depth * n_pool, 2 * N_HEADS, HEAD_DIM, page)
    sample_caches = (cmp_pages, slc_pages, cache_nsa_win, moba_pages, cache_mem, state_rwkv, state_shift,
                     n_pool, page)
    w_all = _pack_w_in(w_in)
    w_mem, w_br, w_o = w_mem_kv.astype(BF16), w_branch.astype(BF16), w_out.astype(BF16)
    w_fg, w_fu, w_fd = w_ffn_gate.astype(BF16), w_ffn_up.astype(BF16), w_ffn_down.astype(BF16)
    outs_p, outs_s, mem_out = [], [], []
    for l in range(depth):
        w = _layer_weights(l, rwkv_mu, rwkv_w0, rwkv_w2, rwkv_a0,
                           rwkv_a2, rwkv_g2, rwkv_k_k, rwkv_k_a, rwkv_r_k, rwkv_ln_w, rwkv_ln_b)
        mem_kv = mm(rms_cast(mem_flat, g_mem[l]), w_mem, l, 0, 2 * BRANCH_W).reshape(bp, n_mem, 2 * BRANCH_W)
        mem_out.append(mem_kv.reshape(bp, n_mem, 2, N_HEADS, HEAD_DIM))

        o_p, gates_p, new_p = _mixers_prompt(rms_cast(xp, g_pre_mix[l]), w_all, l, w, tab_p, mem_kv, consts, bp, t)
        xp = merge(xp, o_p, gates_p, w_br, w_o, g_post_mix[l], l)
        xp = ffn(xp, g_pre_ffn[l], w_fg, w_fu, w_fd, g_post_ffn[l], l)

        o_s, gates_s, new_s = _mixers_sample(rms_cast(xs, g_pre_mix[l]), w_all, l, w, tab_s, page_table,
                                             sample_caches, consts, bs)
        xs = merge(xs, o_s, gates_s, w_br, w_o, g_post_mix[l], l)
        xs = ffn(xs, g_pre_ffn[l], w_fg, w_fu, w_fd, g_post_ffn[l], l)
        outs_p.append(new_p)
        outs_s.append(new_s)

    def stack(outs, i):
        return jnp.stack([o[i] for o in outs], axis=0)

    return (xp.reshape(bp, t, d), xs.reshape(bs, 1, d),
            stack(outs_p, 0), stack(outs_s, 0), stack(outs_p, 1), stack(outs_s, 1),
            stack(outs_p, 2), stack(outs_s, 2), stack(outs_p, 3), stack(outs_s, 3),
            jnp.stack(mem_out, axis=0),
            stack(outs_p, 4), stack(outs_s, 4), stack(outs_p, 5), stack(outs_s, 5))
```
